```python
import math
import jax
import jax.numpy as jnp
from jax import lax
import numpy as np


D_MODEL = 1024
BATCH = 8
SEQ = 4096
DEPTH = 1

CHUNK = 64
PLE_DIM = 256
SB_HEADS = 8
SB_HEAD_DIM = 64
SB_WIDTH = SB_HEADS * SB_HEAD_DIM
SGU_GROUPS = 8
SGU_WIDTH = 512
SGU_GROUP_DIM = SGU_WIDTH // SGU_GROUPS
SGU_BLOCK = 128
QBLK = 128
N_EXPERTS = 32
TOP_K = 4
D_EXPERT = 1024
SWIGLU_ALPHA = 1.702
SWIGLU_LIMIT = 7.0
MOE_BLOCK = 128
EPS = 1e-6
SPLITS = (SB_WIDTH, SB_WIDTH, SB_WIDTH, SGU_WIDTH, SGU_WIDTH, D_MODEL, D_MODEL)
IN_COLS = sum(SPLITS)

kernel_name = 'hybrid_stickbreak_sgu_moe_block'


def rmsnorm(x, g):
    xf = x.astype(jnp.float32)
    y = xf * lax.rsqrt(jnp.mean(xf * xf, axis=-1, keepdims=True) + EPS)
    return (y * g.astype(jnp.float32)).astype(x.dtype)


def layernorm(x, g):
    xf = x.astype(jnp.float32)
    mu = jnp.mean(xf, axis=-1, keepdims=True)
    var = jnp.mean(jnp.square(xf - mu), axis=-1, keepdims=True)
    return ((xf - mu) * lax.rsqrt(var + EPS) * g.astype(jnp.float32)).astype(x.dtype)


def stick_breaking_attention(q, k, v):
    S = q.shape[2]
    scale = 1.0 / math.sqrt(q.shape[-1])
    outs = []
    for qb in range(S // QBLK):
        L = (qb + 1) * QBLK
        qi = q[:, :, qb * QBLK:L]
        z = jnp.einsum('bhtd,bhsd->bhts', qi, k[:, :, :L]).astype(jnp.float32) * scale
        t_pos = qb * QBLK + jnp.arange(QBLK)
        s_pos = jnp.arange(L)
        mask = s_pos[None, :] < t_pos[:, None]
        log_1mb = jnp.where(mask, jax.nn.log_sigmoid(-z), 0.0)
        rest = lax.cumsum(log_1mb, axis=3, reverse=True) - log_1mb
        w = jnp.where(mask, jnp.exp(jax.nn.log_sigmoid(z) + rest), 0.0)
        outs.append(jnp.einsum('bhts,bhsd->bhtd', w.astype(v.dtype), v[:, :, :L]))
    return jnp.concatenate(outs, axis=2)


def spatial_gating(u, vs, norm_g, w_s, b_s):
    B, S, _ = u.shape
    u = jax.nn.gelu(u)
    vs = layernorm(jax.nn.gelu(vs), norm_g)
    pos = jnp.arange(SGU_BLOCK)
    cmask = (pos[None, :] // CHUNK) <= (pos[:, None] // CHUNK)
    w_m = jnp.where(cmask[None], w_s, 0.0).astype(vs.dtype)
    vg = vs.reshape(B, S // SGU_BLOCK, SGU_BLOCK, SGU_GROUPS, SGU_GROUP_DIM)
    sv = jnp.einsum('gts,bnsgc->bntgc', w_m, vg) + b_s.T[None, None, :, :, None]
    return u * sv.reshape(B, S, SGU_WIDTH)


def token_mixer(a, w_in, w_out_a, w_out_b, w_out, sgu_norm, sgu_w, sgu_b):
    B, S, _ = a.shape
    zz = a @ w_in
    idx = list(np.cumsum(SPLITS)[:-1])
    q, k, v, u, vs, ga, gb = jnp.split(zz, idx, axis=-1)
    heads = lambda t: t.reshape(B, S, SB_HEADS, SB_HEAD_DIM).transpose(0, 2, 1, 3)
    o_a = stick_breaking_attention(heads(q), heads(k), heads(v))
    o_a = o_a.transpose(0, 2, 1, 3).reshape(B, S, SB_WIDTH)
    o_b = spatial_gating(u, vs, sgu_norm, sgu_w, sgu_b)
    merged = jax.nn.sigmoid(ga) * (o_a @ w_out_a) + jax.nn.sigmoid(gb) * (o_b @ w_out_b)
    return merged @ w_out


def moe(c, w_router, b_router, w1, b1, w2, b2):
    B, S, D = c.shape
    T = B * S
    cf = c.reshape(T, D)
    logits = (cf @ w_router + b_router).astype(jnp.float32)
    top_vals, top_idx = lax.top_k(logits, TOP_K)
    gates = jax.nn.softmax(top_vals, axis=-1).astype(c.dtype)
    e_flat = top_idx.reshape(-1)
    g_flat = gates.reshape(-1)
    tok_flat = jnp.repeat(jnp.arange(T), TOP_K)
    order = jnp.argsort(e_flat)
    e_sorted = e_flat[order]
    counts = jnp.bincount(e_flat, length=N_EXPERTS)
    offsets = jnp.cumsum(counts) - counts
    padded = ((counts + MOE_BLOCK - 1) // MOE_BLOCK) * MOE_BLOCK
    pad_end = jnp.cumsum(padded)
    pad_off = pad_end - padded
    rank = jnp.arange(T * TOP_K) - offsets[e_sorted]
    dest = pad_off[e_sorted] + rank
    n_blocks = -(-(T * TOP_K + N_EXPERTS * (MOE_BLOCK - 1)) // MOE_BLOCK)
    n_slots = n_blocks * MOE_BLOCK
    slot_tok = jnp.zeros((n_slots,), jnp.int32).at[dest].set(tok_flat[order].astype(jnp.int32))
    slot_gate = jnp.zeros((n_slots,), c.dtype).at[dest].set(g_flat[order])
    starts = jnp.arange(n_blocks) * MOE_BLOCK
    block_e = jnp.minimum(jnp.sum(starts[:, None] >= pad_end[None, :], axis=1), N_EXPERTS - 1)
    xs = cf[slot_tok].reshape(n_blocks, MOE_BLOCK, D)

    def expert_block(args):
        xb, e = args
        hh = xb @ w1[e] + b1[e]
        x_glu = jnp.minimum(hh[:, ::2], SWIGLU_LIMIT)
        x_lin = jnp.clip(hh[:, 1::2], -SWIGLU_LIMIT, SWIGLU_LIMIT)
        act = x_glu * jax.nn.sigmoid(SWIGLU_ALPHA * x_glu) * (x_lin + 1.0)
        return act @ w2[e] + b2[e]

    ys = lax.map(expert_block, (xs, block_e)).reshape(n_slots, D)
    out = jnp.zeros((T, D), c.dtype).at[slot_tok].add(ys * slot_gate[:, None])
    return out.reshape(B, S, D)


def setup_inputs(seed: int = 0) -> dict:
    key = jax.random.key(seed)
    ks = jax.random.split(key, 24)
    f32 = jnp.float32
    nrm = lambda k, shape, s: jax.random.normal(k, shape, f32) * s
    gain = lambda k, shape: 1.0 + 0.02 * jax.random.normal(k, shape, f32)
    return {
        'x': jax.random.normal(ks[0], (BATCH, SEQ, D_MODEL), f32),
        'p': jax.random.normal(ks[1], (DEPTH, BATCH, SEQ, PLE_DIM), f32),
        'g_mix': gain(ks[2], (DEPTH, D_MODEL)),
        'w_in': nrm(ks[3], (DEPTH, D_MODEL, IN_COLS), D_MODEL ** -0.5),
        'w_out_a': nrm(ks[4], (DEPTH, SB_WIDTH, D_MODEL), SB_WIDTH ** -0.5),
        'w_out_b': nrm(ks[5], (DEPTH, SGU_WIDTH, D_MODEL), SGU_WIDTH ** -0.5),
        'w_out': nrm(ks[6], (DEPTH, D_MODEL, D_MODEL), D_MODEL ** -0.5),
        'sgu_norm': gain(ks[7], (DEPTH, SGU_WIDTH)),
        'sgu_w': nrm(ks[8], (DEPTH, SGU_GROUPS, SGU_BLOCK, SGU_BLOCK), SGU_BLOCK ** -0.5),
        'sgu_b': 1.0 + 0.02 * jax.random.normal(ks[9], (DEPTH, SGU_GROUPS, SGU_BLOCK), f32),
        'g_moe': gain(ks[10], (DEPTH, D_MODEL)),
        'w_router': nrm(ks[11], (DEPTH, D_MODEL, N_EXPERTS), D_MODEL ** -0.5),
        'b_router': nrm(ks[12], (DEPTH, N_EXPERTS), 0.01),
        'w1': nrm(ks[13], (DEPTH, N_EXPERTS, D_MODEL, 2 * D_EXPERT), D_MODEL ** -0.5),
        'b1': nrm(ks[14], (DEPTH, N_EXPERTS, 2 * D_EXPERT), 0.02),
        'w2': nrm(ks[15], (DEPTH, N_EXPERTS, D_EXPERT, D_MODEL), D_EXPERT ** -0.5),
        'b2': nrm(ks[16], (DEPTH, N_EXPERTS, D_MODEL), 0.02),
        'g_ple': gain(ks[17], (DEPTH, D_MODEL)),
        'w_ple_gate': nrm(ks[18], (DEPTH, D_MODEL, D_MODEL), D_MODEL ** -0.5),
        'w_ple_proj': nrm(ks[19], (DEPTH, PLE_DIM, D_MODEL), PLE_DIM ** -0.5),
        'g_final': gain(ks[20], (D_MODEL,)),
    }


def reference(x, p, g_mix, w_in, w_out_a, w_out_b, w_out, sgu_norm, sgu_w, sgu_b,
              g_moe, w_router, b_router, w1, b1, w2, b2, g_ple, w_ple_gate, w_ple_proj, g_final):
    h = x
    for i in range(DEPTH):
        a = rmsnorm(h, g_mix[i])
        h = h + token_mixer(a, w_in[i], w_out_a[i], w_out_b[i], w_out[i],
                            sgu_norm[i], sgu_w[i], sgu_b[i])
        c = rmsnorm(h, g_moe[i])
        h = h + moe(c, w_router[i], b_router[i], w1[i], b1[i], w2[i], b2[i])
        pg = jax.nn.sigmoid(rmsnorm(h, g_ple[i]) @ w_ple_gate[i])
        h = h + pg * (p[i] @ w_ple_proj[i])
    return rmsnorm(h, g_final)
```

```python
import functools
import math

import jax
import jax.numpy as jnp
from jax import lax
from jax.experimental import pallas as pl
from jax.experimental.pallas import tpu as pltpu

F32 = jnp.float32
BF16 = jnp.bfloat16

EPS = 1e-6
CHUNK = 64
SB_HEADS = 8
SB_HEAD_DIM = 64
SB_WIDTH = SB_HEADS * SB_HEAD_DIM
SGU_GROUPS = 8
SGU_WIDTH = 512
SGU_GROUP_DIM = SGU_WIDTH // SGU_GROUPS
SGU_BLOCK = 128
N_EXPERTS = 32
TOP_K = 4
SWIGLU_ALPHA = 1.702
SWIGLU_LIMIT = 7.0

LANES = 128
VMEM_LIMIT_BYTES = 56 * 1024 * 1024

TM_PROJ = 512
TQ_ATTN = 512
KB_ATTN = 128
TM_DISPATCH = 512
TB_EXPERT = 256
TM_COMBINE = 256


def _cparams(sem):
    return pltpu.CompilerParams(dimension_semantics=sem,
                                vmem_limit_bytes=VMEM_LIMIT_BYTES)


def _rms(x, g):
    ms = jnp.mean(x * x, axis=-1, keepdims=True)
    return x * lax.rsqrt(ms + EPS) * g


def _inproj_kernel(x_ref, g_ref, w_ref, sw_ref, sbias_ref, sn_ref,
                   qkv_ref, ob_ref, sga_ref, sgb_ref):
    tm = x_ref.shape[0]
    a = _rms(x_ref[...], g_ref[...]).astype(BF16)

    def proj(c0, width):
        return jnp.dot(a, w_ref[:, c0:c0 + width], preferred_element_type=F32)

    qkv_ref[:, 0:SB_WIDTH] = (proj(0, SB_WIDTH) * (1.0 / math.sqrt(SB_HEAD_DIM))).astype(BF16)
    qkv_ref[:, SB_WIDTH:2 * SB_WIDTH] = proj(SB_WIDTH, SB_WIDTH).astype(BF16)
    qkv_ref[:, 2 * SB_WIDTH:3 * SB_WIDTH] = proj(2 * SB_WIDTH, SB_WIDTH).astype(BF16)

    c_u = 3 * SB_WIDTH
    gu = jax.nn.gelu(proj(c_u, SGU_WIDTH))
    gv = jax.nn.gelu(proj(c_u + SGU_WIDTH, SGU_WIDTH))
    mu = jnp.mean(gv, axis=-1, keepdims=True)
    d = gv - mu
    var = jnp.mean(d * d, axis=-1, keepdims=True)
    vn = (d * lax.rsqrt(var + EPS) * sn_ref[...]).astype(BF16)

    rr = lax.broadcasted_iota(jnp.int32, (SGU_BLOCK, SGU_BLOCK), 0) // CHUNK
    cc = lax.broadcasted_iota(jnp.int32, (SGU_BLOCK, SGU_BLOCK), 1) // CHUNK
    cmask = cc <= rr
    lane = lax.broadcasted_iota(jnp.int32, (SGU_BLOCK, LANES), 1)
    lo_mask = lane < SGU_GROUP_DIM
    wcat = []
    for j in range(SGU_GROUPS // 2):
        w0 = jnp.where(cmask, sw_ref[2 * j], 0.0).astype(BF16)
        w1 = jnp.where(cmask, sw_ref[2 * j + 1], 0.0).astype(BF16)
        wcat.append(jnp.concatenate([w0, w1], axis=1))

    zero = jnp.zeros((), BF16)
    for blk in range(tm // SGU_BLOCK):
        r0 = blk * SGU_BLOCK
        for j in range(SGU_GROUPS // 2):
            vp = vn[r0:r0 + SGU_BLOCK, j * LANES:(j + 1) * LANES]
            rhs = jnp.concatenate([jnp.where(lo_mask, vp, zero),
                                   jnp.where(lo_mask, zero, vp)], axis=0)
            sv = jnp.dot(wcat[j], rhs, preferred_element_type=F32)
            sv = sv + sbias_ref[:, j * LANES:(j + 1) * LANES]
            ob_ref[r0:r0 + SGU_BLOCK, j * LANES:(j + 1) * LANES] = (
                gu[r0:r0 + SGU_BLOCK, j * LANES:(j + 1) * LANES] * sv).astype(BF16)

    c_g = c_u + 2 * SGU_WIDTH
    d_model = sga_ref.shape[1]
    for c0 in range(0, d_model, 512):
        sga_ref[:, c0:c0 + 512] = jax.nn.sigmoid(proj(c_g + c0, 512)).astype(BF16)
        sgb_ref[:, c0:c0 + 512] = jax.nn.sigmoid(proj(c_g + d_model + c0, 512)).astype(BF16)


def _inproj(x2, g_mix, w_in, sgu_w, sgu_bias, sgu_norm):
    t, d = x2.shape
    ncol = w_in.shape[1]
    tm = TM_PROJ
    const = lambda *shape: pl.BlockSpec(shape, lambda i: (0,) * len(shape))
    return pl.pallas_call(
        _inproj_kernel,
        grid=(t // tm,),
        in_specs=[
            pl.BlockSpec((tm, d), lambda i: (i, 0)),
            const(1, d),
            const(d, ncol),
            const(SGU_GROUPS, SGU_BLOCK, SGU_BLOCK),
            const(SGU_BLOCK, SGU_WIDTH),
            const(1, SGU_WIDTH),
        ],
        out_specs=[
            pl.BlockSpec((tm, 3 * SB_WIDTH), lambda i: (i, 0)),
            pl.BlockSpec((tm, SGU_WIDTH), lambda i: (i, 0)),
            pl.BlockSpec((tm, d), lambda i: (i, 0)),
            pl.BlockSpec((tm, d), lambda i: (i, 0)),
        ],
        out_shape=[
            jax.ShapeDtypeStruct((t, 3 * SB_WIDTH), BF16),
            jax.ShapeDtypeStruct((t, SGU_WIDTH), BF16),
            jax.ShapeDtypeStruct((t, d), BF16),
            jax.ShapeDtypeStruct((t, d), BF16),
        ],
        compiler_params=_cparams(("arbitrary",)),
        name="inproj_sgu",
    )(x2, g_mix, w_in, sgu_w, sgu_bias, sgu_norm)


def _attn_kernel(q_ref, k_ref, v_ref, tri_ref, o_ref):
    tq = q_ref.shape[1]
    kb = KB_ATTN
    qi = pl.program_id(2)
    lane = lax.broadcasted_iota(jnp.int32, (kb, LANES), 1)
    head0 = lane < SB_HEAD_DIM
    row = lax.broadcasted_iota(jnp.int32, (kb, kb), 0)
    col = lax.broadcasted_iota(jnp.int32, (kb, kb), 1)
    causal = col < row
    zero = jnp.zeros((), BF16)

    def head_tile(qh, kj, vj, carry, acc, masked):
        z = lax.dot_general(qh, kj, (((1,), (1,)), ((), ())), preferred_element_type=F32)
        lb = -(jnp.maximum(z, 0.0) + jnp.log(1.0 + jnp.exp(-jnp.abs(z))))
        if masked:
            lb = jnp.where(causal, lb, 0.0)
        ct = jnp.dot(lb.astype(BF16), tri_ref[...], preferred_element_type=F32)
        w = jnp.exp(z + ct[:, :kb] + carry)
        if masked:
            w = jnp.where(causal, w, 0.0)
        acc = acc + jnp.dot(w.astype(BF16), vj, preferred_element_type=F32)
        return carry + ct[:, kb:], acc

    def qblock(s, _):
        r0 = pl.multiple_of(s * kb, kb)
        q = q_ref[0, pl.ds(r0, kb), :]
        q0 = jnp.where(head0, q, zero)
        q1 = jnp.where(head0, zero, q)
        jd = qi * (tq // kb) + s

        def tile(j, st, masked):
            c0, a0, c1, a1 = st
            k0 = pl.multiple_of(j * kb, kb)
            kj = k_ref[0, pl.ds(k0, kb), :]
            vj = v_ref[0, pl.ds(k0, kb), :]
            c0, a0 = head_tile(q0, kj, vj, c0, a0, masked)
            c1, a1 = head_tile(q1, kj, vj, c1, a1, masked)
            return c0, a0, c1, a1

        zf = jnp.zeros((kb, LANES), F32)
        st = tile(jd, (zf, zf, zf, zf), True)
        st = lax.fori_loop(0, jd, lambda t, st: tile(jd - 1 - t, st, False), st)
        o_ref[0, pl.ds(r0, kb), :] = jnp.where(head0, st[1], st[3]).astype(o_ref.dtype)
        return 0

    lax.fori_loop(0, tq // kb, qblock, 0)


def _attention(qkv3, tri):
    b, s, _ = qkv3.shape
    tq = TQ_ATTN
    npair = SB_WIDTH // LANES
    return pl.pallas_call(
        _attn_kernel,
        grid=(b, npair, s // tq),
        in_specs=[
            pl.BlockSpec((1, tq, LANES), lambda bi, p, i: (bi, i, p)),
            pl.BlockSpec((1, s, LANES), lambda bi, p, i: (bi, 0, npair + p)),
            pl.BlockSpec((1, s, LANES), lambda bi, p, i: (bi, 0, 2 * npair + p)),
            pl.BlockSpec((KB_ATTN, 2 * KB_ATTN), lambda bi, p, i: (0, 0)),
        ],
        out_specs=pl.BlockSpec((1, tq, LANES), lambda bi, p, i: (bi, i, p)),
        out_shape=jax.ShapeDtypeStruct((b, s, SB_WIDTH), BF16),
        compiler_params=_cparams(("arbitrary", "arbitrary", "arbitrary")),
        name="stickbreak_attn",
    )(qkv3, qkv3, qkv3, tri)


def _outproj_kernel(oa_ref, ob_ref, sga_ref, sgb_ref, x_ref, woa_ref, wob_ref, wo_ref,
                    g_ref, wr_ref, br_ref, upper_ref,
                    h_ref, c_ref, idx_ref, gate_ref, rank_ref, cnt_ref, cnt_sc):
    i = pl.program_id(0)
    tm = x_ref.shape[0]

    @pl.when(i == 0)
    def _():
        cnt_sc[...] = jnp.zeros_like(cnt_sc)

    ma = jnp.dot(oa_ref[...], woa_ref[...], preferred_element_type=F32)
    mb = jnp.dot(ob_ref[...], wob_ref[...], preferred_element_type=F32)
    merged = sga_ref[...].astype(F32) * ma + sgb_ref[...].astype(F32) * mb
    h = x_ref[...] + jnp.dot(merged.astype(BF16), wo_ref[...], preferred_element_type=F32)
    h_ref[...] = h
    c = _rms(h, g_ref[...])
    c_ref[...] = c

    logits = lax.dot_general(wr_ref[...], c, (((1,), (1,)), ((), ())),
                             precision=lax.Precision.HIGHEST,
                             preferred_element_type=F32) + br_ref[...]
    eid = lax.broadcasted_iota(jnp.int32, (N_EXPERTS, tm), 0).astype(F32)
    work = logits
    vals, sels, ids = [], [], []
    for _ in range(TOP_K):
        m = jnp.max(work, axis=0, keepdims=True)
        ik = jnp.min(jnp.where(work == m, eid, float(N_EXPERTS)), axis=0, keepdims=True)
        sel = eid == ik
        vals.append(m)
        ids.append(ik)
        sels.append(sel)
        work = jnp.where(sel, -jnp.inf, work)
    es = [jnp.exp(v - vals[0]) for v in vals]
    inv = 1.0 / (es[0] + es[1] + es[2] + es[3])
    onehot = jnp.zeros((N_EXPERTS, tm), F32)
    for sel in sels:
        onehot = onehot + jnp.where(sel, 1.0, 0.0)
    prefix = jnp.dot(onehot.astype(BF16), upper_ref[...], preferred_element_type=F32)
    prefix = prefix + cnt_sc[...][:, 0:1]
    for k in range(TOP_K):
        idx_ref[k:k + 1, :] = ids[k].astype(jnp.int32)
        gate_ref[k:k + 1, :] = es[k] * inv
        rank_ref[k:k + 1, :] = jnp.sum(jnp.where(sels[k], prefix, 0.0), axis=0,
                                       keepdims=True).astype(jnp.int32)
    cnt_sc[...] = cnt_sc[...] + jnp.sum(onehot, axis=1, keepdims=True)
    cnt_ref[...] = cnt_sc[...]


def _outproj(o_a, o_b, sga, sgb, x2, w_out_a, w_out_b, w_out, g_moe, w_router_t, b_router, upper):
    t, d = x2.shape
    tm = TM_PROJ
    const = lambda *shape: pl.BlockSpec(shape, lambda i: (0,) * len(shape))
    row = lambda w: pl.BlockSpec((tm, w), lambda i: (i, 0))
    colk = pl.BlockSpec((TOP_K, tm), lambda i: (0, i))
    return pl.pallas_call(
        _outproj_kernel,
        grid=(t // tm,),
        in_specs=[row(SB_WIDTH), row(SGU_WIDTH), row(d), row(d), row(d),
                  const(SB_WIDTH, d), const(SGU_WIDTH, d), const(d, d),
                  const(1, d), const(N_EXPERTS, d), const(N_EXPERTS, 1), const(tm, tm)],
        out_specs=[row(d), row(d), colk, colk, colk, const(N_EXPERTS, LANES)],
        out_shape=[
            jax.ShapeDtypeStruct((t, d), F32),
            jax.ShapeDtypeStruct((t, d), F32),
            jax.ShapeDtypeStruct((TOP_K, t), jnp.int32),
            jax.ShapeDtypeStruct((TOP_K, t), F32),
            jax.ShapeDtypeStruct((TOP_K, t), jnp.int32),
            jax.ShapeDtypeStruct((N_EXPERTS, LANES), F32),
        ],
        scratch_shapes=[pltpu.VMEM((N_EXPERTS, LANES), F32)],
        compiler_params=_cparams(("arbitrary",)),
        name="outproj_router",
    )(o_a, o_b, sga, sgb, x2, w_out_a, w_out_b, w_out, g_moe, w_router_t, b_router, upper)


def _dispatch_kernel(dest_ref, c_ref, xs_in_ref, xs_ref, sem):
    del xs_in_ref
    tm = c_ref.shape[0]

    def row_copy(r, k):
        return pltpu.make_async_copy(c_ref.at[pl.ds(r, 1), :],
                                     xs_ref.at[pl.ds(dest_ref[k, r], 1), :], sem)

    def issue(r, _):
        for k in range(TOP_K):
            row_copy(r, k).start()
        return 0

    lax.fori_loop(0, tm, issue, 0)

    def drain(r, _):
        for k in range(TOP_K):
            row_copy(r, k).wait()
        return 0

    lax.fori_loop(0, tm, drain, 0)


def _dispatch(dest, c, xs_init):
    t, d = c.shape
    tm = TM_DISPATCH
    return pl.pallas_call(
        _dispatch_kernel,
        grid=(t // tm,),
        in_specs=[
            pl.BlockSpec((TOP_K, tm), lambda i: (0, i), memory_space=pltpu.SMEM),
            pl.BlockSpec((tm, d), lambda i: (i, 0)),
            pl.BlockSpec(memory_space=pl.ANY),
        ],
        out_specs=pl.BlockSpec(memory_space=pl.ANY),
        out_shape=jax.ShapeDtypeStruct(xs_init.shape, xs_init.dtype),
        scratch_shapes=[pltpu.SemaphoreType.DMA(())],
        input_output_aliases={2: 0},
        compiler_params=_cparams(("arbitrary",)),
        name="moe_dispatch",
    )(dest, c, xs_init)


def _expert_kernel(be_ref, nb_ref, xs_ref, w1g_ref, w1l_ref, w2_ref, b1g_ref, b1l_ref, b2_ref,
                   ys_ref):
    del be_ref

    @pl.when(pl.program_id(0) < nb_ref[0])
    def _():
        xb = xs_ref[...].astype(BF16)
        hg = jnp.dot(xb, w1g_ref[0], preferred_element_type=F32) + b1g_ref[0]
        hl = jnp.dot(xb, w1l_ref[0], preferred_element_type=F32) + b1l_ref[0]
        xg = jnp.minimum(hg, SWIGLU_LIMIT)
        xl = jnp.clip(hl, -SWIGLU_LIMIT, SWIGLU_LIMIT)
        act = xg * jax.nn.sigmoid(SWIGLU_ALPHA * xg) * (xl + 1.0)
        ys_ref[...] = jnp.dot(act.astype(BF16), w2_ref[0], preferred_element_type=F32) + b2_ref[0]


def _experts(block_e, n_used, xs, w1g, w1l, w2, b1g, b1l, b2):
    n_slots, d = xs.shape
    f = w1g.shape[2]
    tb = TB_EXPERT
    blk = lambda i, be, nb: (jnp.minimum(i, nb[0] - 1), 0)
    wsel = lambda i, be, nb: (be[jnp.minimum(i, nb[0] - 1)], 0, 0)
    grid_spec = pltpu.PrefetchScalarGridSpec(
        num_scalar_prefetch=2,
        grid=(n_slots // tb,),
        in_specs=[
            pl.BlockSpec((tb, d), blk),
            pl.BlockSpec((1, d, f), wsel),
            pl.BlockSpec((1, d, f), wsel),
            pl.BlockSpec((1, f, d), wsel),
            pl.BlockSpec((1, 1, f), wsel),
            pl.BlockSpec((1, 1, f), wsel),
            pl.BlockSpec((1, 1, d), wsel),
        ],
        out_specs=pl.BlockSpec((tb, d), blk),
    )
    return pl.pallas_call(
        _expert_kernel,
        grid_spec=grid_spec,
        out_shape=jax.ShapeDtypeStruct((n_slots, d), F32),
        compiler_params=_cparams(("arbitrary",)),
        name="moe_experts",
    )(block_e, n_used, xs, w1g, w1l, w2, b1g, b1l, b2)


def _combine_kernel(dest_ref, gate_ref, h_ref, p_ref, ys_ref, gp_ref, wg_ref, wp_ref, gf_ref,
                    out_ref, buf, sem, *, final):
    tm = h_ref.shape[0]

    def row_copy(r, k):
        return pltpu.make_async_copy(ys_ref.at[pl.ds(dest_ref[k, r], 1), :],
                                     buf.at[k, pl.ds(r, 1), :], sem)

    def issue(r, _):
        for k in range(TOP_K):
            row_copy(r, k).start()
        return 0

    lax.fori_loop(0, tm, issue, 0)

    def drain(r, _):
        for k in range(TOP_K):
            row_copy(r, k).wait()
        return 0

    lax.fori_loop(0, tm, drain, 0)

    h = h_ref[...]
    for k in range(TOP_K):
        h = h + gate_ref[:, k:k + 1] * buf[k]
    r = _rms(h, gp_ref[...]).astype(BF16)
    pg = jax.nn.sigmoid(jnp.dot(r, wg_ref[...], preferred_element_type=F32))
    pp = jnp.dot(p_ref[...].astype(BF16), wp_ref[...], preferred_element_type=F32)
    h = h + pg * pp
    out_ref[...] = _rms(h, gf_ref[...]) if final else h


def _combine(dest, gate_tk, h1, p2, ys, g_ple, w_ple_gate, w_ple_proj, g_final, final):
    t, d = h1.shape
    pd = p2.shape[1]
    tm = TM_COMBINE
    const = lambda *shape: pl.BlockSpec(shape, lambda i: (0,) * len(shape))
    return pl.pallas_call(
        functools.partial(_combine_kernel, final=final),
        grid=(t // tm,),
        in_specs=[
            pl.BlockSpec((TOP_K, tm), lambda i: (0, i), memory_space=pltpu.SMEM),
            pl.BlockSpec((tm, TOP_K), lambda i: (i, 0)),
            pl.BlockSpec((tm, d), lambda i: (i, 0)),
            pl.BlockSpec((tm, pd), lambda i: (i, 0)),
            pl.BlockSpec(memory_space=pl.ANY),
            const(1, d), const(d, d), const(pd, d), const(1, d),
        ],
        out_specs=pl.BlockSpec((tm, d), lambda i: (i, 0)),
        out_shape=jax.ShapeDtypeStruct((t, d), F32),
        scratch_shapes=[pltpu.VMEM((TOP_K, tm, d), F32), pltpu.SemaphoreType.DMA(())],
        compiler_params=_cparams(("arbitrary",)),
        name="combine_ple",
    )(dest, gate_tk, h1, p2, ys, g_ple, w_ple_gate, w_ple_proj, g_final)


def kernel(x, p, g_mix, w_in, w_out_a, w_out_b, w_out, sgu_norm, sgu_w, sgu_b, g_moe,
           w_router, b_router, w1, b1, w2, b2, g_ple, w_ple_gate, w_ple_proj, g_final):
    b, s, d = x.shape
    depth = w_in.shape[0]
    t = b * s
    assert s % TQ_ATTN == 0 and t % TM_PROJ == 0 and t % TM_COMBINE == 0

    kk = jnp.arange(KB_ATTN)
    tri = jnp.concatenate([(kk[:, None] >= kk[None, :]), jnp.ones((KB_ATTN, KB_ATTN), bool)],
                          axis=1).astype(BF16)
    tt = jnp.arange(TM_PROJ)
    upper = (tt[:, None] < tt[None, :]).astype(BF16)

    tb = TB_EXPERT
    n_blocks = -(-(t * TOP_K + N_EXPERTS * (tb - 1)) // tb)
    n_slots = n_blocks * tb

    h = x.reshape(t, d)
    for i in range(depth):
        sgu_bias = jnp.repeat(sgu_b[i].T, SGU_GROUP_DIM, axis=1)
        qkv, o_b, sga, sgb = _inproj(h, g_mix[i][None], w_in[i].astype(BF16), sgu_w[i],
                                     sgu_bias, sgu_norm[i][None])
        o_a = _attention(qkv.reshape(b, s, 3 * SB_WIDTH), tri).reshape(t, SB_WIDTH)
        h1, c, idx, gate, rank, cnt = _outproj(
            o_a, o_b, sga, sgb, h, w_out_a[i].astype(BF16), w_out_b[i].astype(BF16),
            w_out[i].astype(BF16), g_moe[i][None], w_router[i].T, b_router[i][:, None], upper)

        counts = cnt[:, 0].astype(jnp.int32)
        padded = ((counts + tb - 1) // tb) * tb
        pad_end = jnp.cumsum(padded)
        pad_off = pad_end - padded
        dest = jnp.take(pad_off, idx, axis=0) + rank
        starts = jnp.arange(n_blocks, dtype=jnp.int32) * tb
        block_e = jnp.minimum(jnp.sum(starts[:, None] >= pad_end[None, :], axis=1),
                              N_EXPERTS - 1).astype(jnp.int32)
        n_used = (pad_end[-1:] // tb).astype(jnp.int32)

        xs = _dispatch(dest, c, jnp.zeros((n_slots, d), F32))
        ys = _experts(block_e, n_used, xs,
                      w1[i][:, :, 0::2].astype(BF16), w1[i][:, :, 1::2].astype(BF16),
                      w2[i].astype(BF16),
                      b1[i][:, None, 0::2], b1[i][:, None, 1::2], b2[i][:, None, :])
        h = _combine(dest, gate.T, h1, p[i].reshape(t, -1), ys, g_ple[i][None],
                     w_ple_gate[i].astype(BF16), w_ple_proj[i].astype(BF16), g_final[None],
                     final=(i == depth - 1))
    return h.reshape(b, s, d)
```

```python
import functools
import math

import jax
import jax.numpy as jnp
from jax import lax
from jax.experimental import pallas as pl
from jax.experimental.pallas import tpu as pltpu

F32 = jnp.float32
BF16 = jnp.bfloat16

EPS = 1e-6
CHUNK = 64
SB_HEADS = 8
SB_HEAD_DIM = 64
SB_WIDTH = SB_HEADS * SB_HEAD_DIM
SGU_GROUPS = 8
SGU_WIDTH = 512
SGU_GROUP_DIM = SGU_WIDTH // SGU_GROUPS
SGU_BLOCK = 128
N_EXPERTS = 32
TOP_K = 4
SWIGLU_ALPHA = 1.702
SWIGLU_LIMIT = 7.0
LOG2E = 1.4426950408889634

LANES = 128
MXU_DIM = 256
VMEM_LIMIT_BYTES = 56 * 1024 * 1024

TM_PROJ = 512
TQ_ATTN = 512
KC_ATTN = MXU_DIM
TM_DISPATCH = 512
TB_EXPERT = 256
TM_COMBINE = 256


def _cparams(sem):
    return pltpu.CompilerParams(dimension_semantics=sem,
                                vmem_limit_bytes=VMEM_LIMIT_BYTES)


def _rms(x, g):
    ms = jnp.mean(x * x, axis=-1, keepdims=True)
    return x * lax.rsqrt(ms + EPS) * g


def _inproj_kernel(x_ref, g_ref, w_ref, sw_ref, sbias_ref, sn_ref,
                   qkv_ref, ob_ref, sga_ref, sgb_ref):
    tm = x_ref.shape[0]
    a = _rms(x_ref[...], g_ref[...]).astype(BF16)

    def proj(c0, width):
        return jnp.dot(a, w_ref[:, c0:c0 + width], preferred_element_type=F32)

    qkv_ref[:, 0:SB_WIDTH] = (proj(0, SB_WIDTH) * (LOG2E / math.sqrt(SB_HEAD_DIM))).astype(BF16)
    qkv_ref[:, SB_WIDTH:2 * SB_WIDTH] = proj(SB_WIDTH, SB_WIDTH).astype(BF16)
    qkv_ref[:, 2 * SB_WIDTH:3 * SB_WIDTH] = proj(2 * SB_WIDTH, SB_WIDTH).astype(BF16)

    c_u = 3 * SB_WIDTH
    gu = jax.nn.gelu(proj(c_u, SGU_WIDTH))
    gv = jax.nn.gelu(proj(c_u + SGU_WIDTH, SGU_WIDTH))
    mu = jnp.mean(gv, axis=-1, keepdims=True)
    d = gv - mu
    var = jnp.mean(d * d, axis=-1, keepdims=True)
    vn = (d * lax.rsqrt(var + EPS) * sn_ref[...]).astype(BF16)

    rr = lax.broadcasted_iota(jnp.int32, (SGU_BLOCK, SGU_BLOCK), 0) // CHUNK
    cc = lax.broadcasted_iota(jnp.int32, (SGU_BLOCK, SGU_BLOCK), 1) // CHUNK
    cmask = cc <= rr
    lane = lax.broadcasted_iota(jnp.int32, (SGU_BLOCK, LANES), 1)
    lo_mask = lane < SGU_GROUP_DIM
    wcat = []
    for j in range(SGU_GROUPS // 2):
        w0 = jnp.where(cmask, sw_ref[2 * j], 0.0).astype(BF16)
        w1 = jnp.where(cmask, sw_ref[2 * j + 1], 0.0).astype(BF16)
        wcat.append(jnp.concatenate([w0, w1], axis=1))

    zero = jnp.zeros((), BF16)
    for blk in range(tm // SGU_BLOCK):
        r0 = blk * SGU_BLOCK
        for j in range(SGU_GROUPS // 2):
            vp = vn[r0:r0 + SGU_BLOCK, j * LANES:(j + 1) * LANES]
            rhs = jnp.concatenate([jnp.where(lo_mask, vp, zero),
                                   jnp.where(lo_mask, zero, vp)], axis=0)
            sv = jnp.dot(wcat[j], rhs, preferred_element_type=F32)
            sv = sv + sbias_ref[:, j * LANES:(j + 1) * LANES]
            ob_ref[r0:r0 + SGU_BLOCK, j * LANES:(j + 1) * LANES] = (
                gu[r0:r0 + SGU_BLOCK, j * LANES:(j + 1) * LANES] * sv).astype(BF16)

    c_g = c_u + 2 * SGU_WIDTH
    d_model = sga_ref.shape[1]
    for c0 in range(0, d_model, 512):
        sga_ref[:, c0:c0 + 512] = jax.nn.sigmoid(proj(c_g + c0, 512)).astype(BF16)
        sgb_ref[:, c0:c0 + 512] = jax.nn.sigmoid(proj(c_g + d_model + c0, 512)).astype(BF16)


def _inproj(x2, g_mix, w_in, sgu_w, sgu_bias, sgu_norm):
    t, d = x2.shape
    ncol = w_in.shape[1]
    tm = TM_PROJ
    const = lambda *shape: pl.BlockSpec(shape, lambda i: (0,) * len(shape))
    return pl.pallas_call(
        _inproj_kernel,
        grid=(t // tm,),
        in_specs=[
            pl.BlockSpec((tm, d), lambda i: (i, 0)),
            const(1, d),
            const(d, ncol),
            const(SGU_GROUPS, SGU_BLOCK, SGU_BLOCK),
            const(SGU_BLOCK, SGU_WIDTH),
            const(1, SGU_WIDTH),
        ],
        out_specs=[
            pl.BlockSpec((tm, 3 * SB_WIDTH), lambda i: (i, 0)),
            pl.BlockSpec((tm, SGU_WIDTH), lambda i: (i, 0)),
            pl.BlockSpec((tm, d), lambda i: (i, 0)),
            pl.BlockSpec((tm, d), lambda i: (i, 0)),
        ],
        out_shape=[
            jax.ShapeDtypeStruct((t, 3 * SB_WIDTH), BF16),
            jax.ShapeDtypeStruct((t, SGU_WIDTH), BF16),
            jax.ShapeDtypeStruct((t, d), BF16),
            jax.ShapeDtypeStruct((t, d), BF16),
        ],
        compiler_params=_cparams(("arbitrary",)),
        name="inproj_sgu",
    )(x2, g_mix, w_in, sgu_w, sgu_bias, sgu_norm)


def _attn_kernel(q_ref, k_ref, v_ref, ntri_ref, o_ref, carry_sc, acc_sc):
    tq = q_ref.shape[1]
    kc = KC_ATTN
    qi = pl.program_id(2)
    lane = lax.broadcasted_iota(jnp.int32, (tq, LANES), 1)
    head0 = lane < SB_HEAD_DIM
    zero = jnp.zeros((), BF16)
    q = q_ref[0]
    qh = (jnp.where(head0, q, zero), jnp.where(head0, zero, q))
    row = lax.broadcasted_iota(jnp.int32, (tq, kc), 0)
    col = lax.broadcasted_iota(jnp.int32, (tq, kc), 1)
    carry_sc[...] = jnp.zeros_like(carry_sc)
    acc_sc[...] = jnp.zeros_like(acc_sc)

    def step(j, key_off):
        k0 = pl.multiple_of(j * kc, kc)
        kj = k_ref[0, pl.ds(k0, kc), :]
        vj = v_ref[0, pl.ds(k0, kc), :]
        mask = None if key_off is None else (col + key_off) < row
        for h in range(2):
            z = lax.dot_general(qh[h], kj, (((1,), (1,)), ((), ())), preferred_element_type=F32)
            sp = jnp.maximum(z, 0.0) + jnp.log2(1.0 + jnp.exp2(-jnp.abs(z)))
            if mask is not None:
                sp = jnp.where(mask, sp, 0.0)
            cum = jnp.dot(sp.astype(BF16), ntri_ref[...], preferred_element_type=F32)
            carry = carry_sc[h]
            w = jnp.exp2(z + cum + jnp.concatenate([carry] * (kc // LANES), axis=1))
            carry_sc[h] = carry + jnp.broadcast_to(cum[:, 0:1], carry.shape)
            if mask is not None:
                w = jnp.where(mask, w, 0.0)
            acc_sc[h] += jnp.dot(w.astype(BF16), vj, preferred_element_type=F32)

    ndiag = tq // kc
    for c in reversed(range(ndiag)):
        step(qi * ndiag + c, c * kc)

    def past(t, _):
        for c in range(ndiag):
            step((qi - t) * ndiag - 1 - c, None)
        return 0

    lax.fori_loop(0, qi, past, 0)
    o_ref[0] = jnp.where(head0, acc_sc[0], acc_sc[1]).astype(o_ref.dtype)


def _attention(qkv3, ntri):
    b, s, _ = qkv3.shape
    tq = TQ_ATTN
    npair = SB_WIDTH // LANES
    return pl.pallas_call(
        _attn_kernel,
        grid=(b, npair, s // tq),
        in_specs=[
            pl.BlockSpec((1, tq, LANES), lambda bi, p, i: (bi, i, p)),
            pl.BlockSpec((1, s, LANES), lambda bi, p, i: (bi, 0, npair + p)),
            pl.BlockSpec((1, s, LANES), lambda bi, p, i: (bi, 0, 2 * npair + p)),
            pl.BlockSpec((KC_ATTN, KC_ATTN), lambda bi, p, i: (0, 0)),
        ],
        out_specs=pl.BlockSpec((1, tq, LANES), lambda bi, p, i: (bi, i, p)),
        out_shape=jax.ShapeDtypeStruct((b, s, SB_WIDTH), BF16),
        scratch_shapes=[pltpu.VMEM((2, tq, LANES), F32), pltpu.VMEM((2, tq, LANES), F32)],
        compiler_params=_cparams(("arbitrary", "arbitrary", "arbitrary")),
        name="stickbreak_attn",
    )(qkv3, qkv3, qkv3, ntri)


def _outproj_kernel(oa_ref, ob_ref, sga_ref, sgb_ref, x_ref, woa_ref, wob_ref, wo_ref,
                    g_ref, wr_ref, br_ref, upper_ref,
                    h_ref, c_ref, idx_ref, gate_ref, rank_ref, cnt_ref, cnt_sc):
    i = pl.program_id(0)
    tm = x_ref.shape[0]

    @pl.when(i == 0)
    def _():
        cnt_sc[...] = jnp.zeros_like(cnt_sc)

    ma = jnp.dot(oa_ref[...], woa_ref[...], preferred_element_type=F32)
    mb = jnp.dot(ob_ref[...], wob_ref[...], preferred_element_type=F32)
    merged = sga_ref[...].astype(F32) * ma + sgb_ref[...].astype(F32) * mb
    h = x_ref[...] + jnp.dot(merged.astype(BF16), wo_ref[...], preferred_element_type=F32)
    h_ref[...] = h
    c = _rms(h, g_ref[...])
    c_ref[...] = c

    logits = lax.dot_general(wr_ref[...], c, (((1,), (1,)), ((), ())),
                             precision=lax.Precision.HIGHEST,
                             preferred_element_type=F32) + br_ref[...]
    eid = lax.broadcasted_iota(jnp.int32, (N_EXPERTS, tm), 0).astype(F32)
    work = logits
    vals, sels, ids = [], [], []
    for _ in range(TOP_K):
        m = jnp.max(work, axis=0, keepdims=True)
        ik = jnp.min(jnp.where(work == m, eid, float(N_EXPERTS)), axis=0, keepdims=True)
        sel = eid == ik
        vals.append(m)
        ids.append(ik)
        sels.append(sel)
        work = jnp.where(sel, -jnp.inf, work)
    es = [jnp.exp(v - vals[0]) for v in vals]
    inv = 1.0 / (es[0] + es[1] + es[2] + es[3])
    onehot = jnp.zeros((N_EXPERTS, tm), F32)
    for sel in sels:
        onehot = onehot + jnp.where(sel, 1.0, 0.0)
    prefix = jnp.dot(onehot.astype(BF16), upper_ref[...], preferred_element_type=F32)
    prefix = prefix + cnt_sc[...][:, 0:1]
    for k in range(TOP_K):
        idx_ref[k:k + 1, :] = ids[k].astype(jnp.int32)
        gate_ref[k:k + 1, :] = es[k] * inv
        rank_ref[k:k + 1, :] = jnp.sum(jnp.where(sels[k], prefix, 0.0), axis=0,
                                       keepdims=True).astype(jnp.int32)
    cnt_sc[...] = cnt_sc[...] + jnp.sum(onehot, axis=1, keepdims=True)
    cnt_ref[...] = cnt_sc[...]


def _outproj(o_a, o_b, sga, sgb, x2, w_out_a, w_out_b, w_out, g_moe, w_router_t, b_router, upper):
    t, d = x2.shape
    tm = TM_PROJ
    const = lambda *shape: pl.BlockSpec(shape, lambda i: (0,) * len(shape))
    row = lambda w: pl.BlockSpec((tm, w), lambda i: (i, 0))
    colk = pl.BlockSpec((TOP_K, tm), lambda i: (0, i))
    return pl.pallas_call(
        _outproj_kernel,
        grid=(t // tm,),
        in_specs=[row(SB_WIDTH), row(SGU_WIDTH), row(d), row(d), row(d),
                  const(SB_WIDTH, d), const(SGU_WIDTH, d), const(d, d),
                  const(1, d), const(N_EXPERTS, d), const(N_EXPERTS, 1), const(tm, tm)],
        out_specs=[row(d), row(d), colk, colk, colk, const(N_EXPERTS, LANES)],
        out_shape=[
            jax.ShapeDtypeStruct((t, d), F32),
            jax.ShapeDtypeStruct((t, d), F32),
            jax.ShapeDtypeStruct((TOP_K, t), jnp.int32),
            jax.ShapeDtypeStruct((TOP_K, t), F32),
            jax.ShapeDtypeStruct((TOP_K, t), jnp.int32),
            jax.ShapeDtypeStruct((N_EXPERTS, LANES), F32),
        ],
        scratch_shapes=[pltpu.VMEM((N_EXPERTS, LANES), F32)],
        compiler_params=_cparams(("arbitrary",)),
        name="outproj_router",
    )(o_a, o_b, sga, sgb, x2, w_out_a, w_out_b, w_out, g_moe, w_router_t, b_router, upper)


def _dispatch_kernel(pad_end_ref, dest_ref, c_ref, xs_ref, zero_sc, sem, zsem):
    tm = c_ref.shape[0]
    tb = zero_sc.shape[0]

    @pl.when(pl.program_id(0) == 0)
    def _():
        zero_sc[...] = jnp.zeros_like(zero_sc)

        def pad_copy(e):
            return pltpu.make_async_copy(
                zero_sc, xs_ref.at[pl.ds(pl.multiple_of(pad_end_ref[e] - tb, tb), tb), :], zsem)

        def nonempty(e):
            prev = jnp.where(e > 0, pad_end_ref[jnp.maximum(e - 1, 0)], 0)
            return pad_end_ref[e] > prev

        def zstart(e, _):
            @pl.when(nonempty(e))
            def _():
                pad_copy(e).start()
            return 0

        def zwait(e, _):
            @pl.when(nonempty(e))
            def _():
                pad_copy(e).wait()
            return 0

        lax.fori_loop(0, N_EXPERTS, zstart, 0)
        lax.fori_loop(0, N_EXPERTS, zwait, 0)

    def row_copy(r, k):
        return pltpu.make_async_copy(c_ref.at[pl.ds(r, 1), :],
                                     xs_ref.at[pl.ds(dest_ref[k, r], 1), :], sem)

    def issue(r, _):
        for k in range(TOP_K):
            row_copy(r, k).start()
        return 0

    lax.fori_loop(0, tm, issue, 0)

    def drain(r, _):
        for k in range(TOP_K):
            row_copy(r, k).wait()
        return 0

    lax.fori_loop(0, tm, drain, 0)


def _dispatch(pad_end, dest, c, n_slots):
    t, d = c.shape
    tm = TM_DISPATCH
    grid_spec = pltpu.PrefetchScalarGridSpec(
        num_scalar_prefetch=1,
        grid=(t // tm,),
        in_specs=[
            pl.BlockSpec((TOP_K, tm), lambda i, pe: (0, i), memory_space=pltpu.SMEM),
            pl.BlockSpec((tm, d), lambda i, pe: (i, 0)),
        ],
        out_specs=pl.BlockSpec(memory_space=pl.ANY),
        scratch_shapes=[pltpu.VMEM((TB_EXPERT, d), F32),
                        pltpu.SemaphoreType.DMA(()), pltpu.SemaphoreType.DMA(())],
    )
    return pl.pallas_call(
        _dispatch_kernel,
        grid_spec=grid_spec,
        out_shape=jax.ShapeDtypeStruct((n_slots, d), F32),
        compiler_params=_cparams(("arbitrary",)),
        name="moe_dispatch",
    )(pad_end, dest, c)


def _expert_kernel(be_ref, nb_ref, xs_ref, w1_ref, w2_ref, b1g_ref, b1l_ref, b2_ref, perm_ref,
                   ys_ref, w1g_sc, w1l_sc, w2_sc):
    i = pl.program_id(0)
    active = i < nb_ref[0]
    fresh = jnp.logical_or(i == 0, be_ref[i] != be_ref[jnp.maximum(i - 1, 0)])

    @pl.when(jnp.logical_and(active, fresh))
    def _():
        half = MXU_DIM // 2
        for cb in range(w1_ref.shape[2] // MXU_DIM):
            blk = w1_ref[0, :, cb * MXU_DIM:(cb + 1) * MXU_DIM].astype(BF16)
            de = jnp.dot(blk, perm_ref[...], preferred_element_type=F32)
            w1g_sc[:, cb * half:(cb + 1) * half] = de[:, :half].astype(BF16)
            w1l_sc[:, cb * half:(cb + 1) * half] = de[:, half:].astype(BF16)
        w2_sc[...] = w2_ref[0].astype(BF16)

    @pl.when(active)
    def _():
        xb = xs_ref[...].astype(BF16)
        hg = jnp.dot(xb, w1g_sc[...], preferred_element_type=F32) + b1g_ref[0]
        hl = jnp.dot(xb, w1l_sc[...], preferred_element_type=F32) + b1l_ref[0]
        xg = jnp.minimum(hg, SWIGLU_LIMIT)
        xl = jnp.clip(hl, -SWIGLU_LIMIT, SWIGLU_LIMIT)
        act = xg * jax.nn.sigmoid(SWIGLU_ALPHA * xg) * (xl + 1.0)
        ys_ref[...] = jnp.dot(act.astype(BF16), w2_sc[...], preferred_element_type=F32) + b2_ref[0]


def _experts(block_e, n_used, xs, w1, w2, b1g, b1l, b2, perm):
    n_slots, d = xs.shape
    f2 = w1.shape[2]
    f = f2 // 2
    tb = TB_EXPERT
    blk = lambda i, be, nb: (jnp.minimum(i, nb[0] - 1), 0)
    wsel = lambda i, be, nb: (be[i], 0, 0)
    grid_spec = pltpu.PrefetchScalarGridSpec(
        num_scalar_prefetch=2,
        grid=(n_slots // tb,),
        in_specs=[
            pl.BlockSpec((tb, d), blk),
            pl.BlockSpec((1, d, f2), wsel),
            pl.BlockSpec((1, f, d), wsel),
            pl.BlockSpec((1, 1, f), wsel),
            pl.BlockSpec((1, 1, f), wsel),
            pl.BlockSpec((1, 1, d), wsel),
            pl.BlockSpec((MXU_DIM, MXU_DIM), lambda i, be, nb: (0, 0)),
        ],
        out_specs=pl.BlockSpec((tb, d), blk),
        scratch_shapes=[pltpu.VMEM((d, f), BF16), pltpu.VMEM((d, f), BF16),
                        pltpu.VMEM((f, d), BF16)],
    )
    return pl.pallas_call(
        _expert_kernel,
        grid_spec=grid_spec,
        out_shape=jax.ShapeDtypeStruct((n_slots, d), F32),
        compiler_params=_cparams(("arbitrary",)),
        name="moe_experts",
    )(block_e, n_used, xs, w1, w2, b1g, b1l, b2, perm)


def _combine_kernel(dest_ref, gate_ref, h_ref, p_ref, ys_ref, gp_ref, wg_ref, wp_ref, gf_ref,
                    out_ref, buf, sem, *, final):
    tm = h_ref.shape[0]

    def row_copy(r, k):
        return pltpu.make_async_copy(ys_ref.at[pl.ds(dest_ref[k, r], 1), :],
                                     buf.at[k, pl.ds(r, 1), :], sem)

    def issue(r, _):
        for k in range(TOP_K):
            row_copy(r, k).start()
        return 0

    lax.fori_loop(0, tm, issue, 0)

    def drain(r, _):
        for k in range(TOP_K):
            row_copy(r, k).wait()
        return 0

    lax.fori_loop(0, tm, drain, 0)

    h = h_ref[...]
    for k in range(TOP_K):
        h = h + gate_ref[:, k:k + 1] * buf[k]
    r = _rms(h, gp_ref[...]).astype(BF16)
    pg = jax.nn.sigmoid(jnp.dot(r, wg_ref[...], preferred_element_type=F32))
    pp = jnp.dot(p_ref[...].astype(BF16), wp_ref[...], preferred_element_type=F32)
    h = h + pg * pp
    out_ref[...] = _rms(h, gf_ref[...]) if final else h


def _combine(dest, gate_tk, h1, p2, ys, g_ple, w_ple_gate, w_ple_proj, g_final, final):
    t, d = h1.shape
    pd = p2.shape[1]
    tm = TM_COMBINE
    const = lambda *shape: pl.BlockSpec(shape, lambda i: (0,) * len(shape))
    return pl.pallas_call(
        functools.partial(_combine_kernel, final=final),
        grid=(t // tm,),
        in_specs=[
            pl.BlockSpec((TOP_K, tm), lambda i: (0, i), memory_space=pltpu.SMEM),
            pl.BlockSpec((tm, TOP_K), lambda i: (i, 0)),
            pl.BlockSpec((tm, d), lambda i: (i, 0)),
            pl.BlockSpec((tm, pd), lambda i: (i, 0)),
            pl.BlockSpec(memory_space=pl.ANY),
            const(1, d), const(d, d), const(pd, d), const(1, d),
        ],
        out_specs=pl.BlockSpec((tm, d), lambda i: (i, 0)),
        out_shape=jax.ShapeDtypeStruct((t, d), F32),
        scratch_shapes=[pltpu.VMEM((TOP_K, tm, d), F32), pltpu.SemaphoreType.DMA(())],
        compiler_params=_cparams(("arbitrary",)),
        name="combine_ple",
    )(dest, gate_tk, h1, p2, ys, g_ple, w_ple_gate, w_ple_proj, g_final)


def kernel(x, p, g_mix, w_in, w_out_a, w_out_b, w_out, sgu_norm, sgu_w, sgu_b, g_moe,
           w_router, b_router, w1, b1, w2, b2, g_ple, w_ple_gate, w_ple_proj, g_final):
    b, s, d = x.shape
    depth = w_in.shape[0]
    t = b * s
    assert s % TQ_ATTN == 0 and t % TM_PROJ == 0 and t % TM_COMBINE == 0

    kk = jnp.arange(KC_ATTN)
    ntri = -(kk[:, None] >= kk[None, :]).astype(BF16)
    tt = jnp.arange(TM_PROJ)
    upper = (tt[:, None] < tt[None, :]).astype(BF16)
    rr = jnp.arange(MXU_DIM)[:, None]
    cc = jnp.arange(MXU_DIM)[None, :]
    half = MXU_DIM // 2
    perm = (rr == jnp.where(cc < half, 2 * cc, 2 * (cc - half) + 1)).astype(BF16)

    tb = TB_EXPERT
    n_blocks = -(-(t * TOP_K + N_EXPERTS * (tb - 1)) // tb)
    n_slots = n_blocks * tb
    experts = jnp.arange(N_EXPERTS, dtype=jnp.int32)

    h = x.reshape(t, d)
    for i in range(depth):
        sgu_bias = jnp.repeat(sgu_b[i].T, SGU_GROUP_DIM, axis=1)
        qkv, o_b, sga, sgb = _inproj(h, g_mix[i][None], w_in[i].astype(BF16), sgu_w[i],
                                     sgu_bias, sgu_norm[i][None])
        o_a = _attention(qkv.reshape(b, s, 3 * SB_WIDTH), ntri).reshape(t, SB_WIDTH)
        h1, c, idx, gate, rank, cnt = _outproj(
            o_a, o_b, sga, sgb, h, w_out_a[i].astype(BF16), w_out_b[i].astype(BF16),
            w_out[i].astype(BF16), g_moe[i][None], w_router[i].T, b_router[i][:, None], upper)

        counts = cnt[:, 0].astype(jnp.int32)
        padded = ((counts + tb - 1) // tb) * tb
        pad_end = jnp.cumsum(padded).astype(jnp.int32)
        pad_off = pad_end - padded
        dest = rank + jnp.sum(jnp.where(idx[:, :, None] == experts, pad_off, 0), axis=-1)
        starts = jnp.arange(n_blocks, dtype=jnp.int32) * tb
        block_e = jnp.minimum(jnp.sum(starts[:, None] >= pad_end[None, :], axis=1),
                              N_EXPERTS - 1).astype(jnp.int32)
        n_used = (pad_end[-1:] // tb).astype(jnp.int32)

        xs = _dispatch(pad_end, dest, c, n_slots)
        ys = _experts(block_e, n_used, xs, w1[i], w2[i],
                      b1[i][:, None, 0::2], b1[i][:, None, 1::2], b2[i][:, None, :], perm)
        h = _combine(dest, gate.T, h1, p[i].reshape(t, -1), ys, g_ple[i][None],
                     w_ple_gate[i].astype(BF16), w_ple_proj[i].astype(BF16), g_final[None],
                     final=(i == depth - 1))
    return h.reshape(b, s, d)
```

```python
import functools
import math

import jax
import jax.numpy as jnp
from jax import lax
from jax.experimental import pallas as pl
from jax.experimental.pallas import tpu as pltpu

F32 = jnp.float32
BF16 = jnp.bfloat16

EPS = 1e-6
CHUNK = 64
SB_HEADS = 8
SB_HEAD_DIM = 64
SB_WIDTH = SB_HEADS * SB_HEAD_DIM
SGU_GROUPS = 8
SGU_WIDTH = 512
SGU_GROUP_DIM = SGU_WIDTH // SGU_GROUPS
SGU_BLOCK = 128
N_EXPERTS = 32
TOP_K = 4
SWIGLU_ALPHA = 1.702
SWIGLU_LIMIT = 7.0
LOG2E = 1.4426950408889634
UNDERFLOW_LOG2 = -160.0

LANES = 128
MXU_DIM = 256
VMEM_LIMIT_BYTES = 56 * 1024 * 1024

TM_PROJ = 512
TQ_ATTN = 512
KC_ATTN = MXU_DIM
TM_DISPATCH = 512
TB_EXPERT = 256
TM_COMBINE = 256


def _cparams(sem):
    return pltpu.CompilerParams(dimension_semantics=sem,
                                vmem_limit_bytes=VMEM_LIMIT_BYTES)


def _rms(x, g):
    ms = jnp.mean(x * x, axis=-1, keepdims=True)
    return x * lax.rsqrt(ms + EPS) * g


def _inproj_kernel(x_ref, g_ref, w_ref, sw_ref, sbias_ref, sn_ref,
                   qkv_ref, ob_ref, sga_ref, sgb_ref):
    tm = x_ref.shape[0]
    a = _rms(x_ref[...], g_ref[...]).astype(BF16)

    def proj(c0, width):
        return jnp.dot(a, w_ref[:, c0:c0 + width], preferred_element_type=F32)

    qkv_ref[:, 0:SB_WIDTH] = (proj(0, SB_WIDTH) * (LOG2E / math.sqrt(SB_HEAD_DIM))).astype(BF16)
    qkv_ref[:, SB_WIDTH:2 * SB_WIDTH] = proj(SB_WIDTH, SB_WIDTH).astype(BF16)
    qkv_ref[:, 2 * SB_WIDTH:3 * SB_WIDTH] = proj(2 * SB_WIDTH, SB_WIDTH).astype(BF16)

    c_u = 3 * SB_WIDTH
    gu = jax.nn.gelu(proj(c_u, SGU_WIDTH))
    gv = jax.nn.gelu(proj(c_u + SGU_WIDTH, SGU_WIDTH))
    mu = jnp.mean(gv, axis=-1, keepdims=True)
    d = gv - mu
    var = jnp.mean(d * d, axis=-1, keepdims=True)
    vn = (d * lax.rsqrt(var + EPS) * sn_ref[...]).astype(BF16)

    rr = lax.broadcasted_iota(jnp.int32, (SGU_BLOCK, SGU_BLOCK), 0) // CHUNK
    cc = lax.broadcasted_iota(jnp.int32, (SGU_BLOCK, SGU_BLOCK), 1) // CHUNK
    cmask = cc <= rr
    lane = lax.broadcasted_iota(jnp.int32, (SGU_BLOCK, LANES), 1)
    lo_mask = lane < SGU_GROUP_DIM
    wcat = []
    for j in range(SGU_GROUPS // 2):
        w0 = jnp.where(cmask, sw_ref[2 * j], 0.0).astype(BF16)
        w1 = jnp.where(cmask, sw_ref[2 * j + 1], 0.0).astype(BF16)
        wcat.append(jnp.concatenate([w0, w1], axis=1))

    zero = jnp.zeros((), BF16)
    for blk in range(tm // SGU_BLOCK):
        r0 = blk * SGU_BLOCK
        for j in range(SGU_GROUPS // 2):
            vp = vn[r0:r0 + SGU_BLOCK, j * LANES:(j + 1) * LANES]
            rhs = jnp.concatenate([jnp.where(lo_mask, vp, zero),
                                   jnp.where(lo_mask, zero, vp)], axis=0)
            sv = jnp.dot(wcat[j], rhs, preferred_element_type=F32)
            sv = sv + sbias_ref[:, j * LANES:(j + 1) * LANES]
            ob_ref[r0:r0 + SGU_BLOCK, j * LANES:(j + 1) * LANES] = (
                gu[r0:r0 + SGU_BLOCK, j * LANES:(j + 1) * LANES] * sv).astype(BF16)

    c_g = c_u + 2 * SGU_WIDTH
    d_model = sga_ref.shape[1]
    for c0 in range(0, d_model, 512):
        sga_ref[:, c0:c0 + 512] = jax.nn.sigmoid(proj(c_g + c0, 512)).astype(BF16)
        sgb_ref[:, c0:c0 + 512] = jax.nn.sigmoid(proj(c_g + d_model + c0, 512)).astype(BF16)


def _inproj(x2, g_mix, w_in, sgu_w, sgu_bias, sgu_norm):
    t, d = x2.shape
    ncol = w_in.shape[1]
    tm = TM_PROJ
    const = lambda *shape: pl.BlockSpec(shape, lambda i: (0,) * len(shape))
    return pl.pallas_call(
        _inproj_kernel,
        grid=(t // tm,),
        in_specs=[
            pl.BlockSpec((tm, d), lambda i: (i, 0)),
            const(1, d),
            const(d, ncol),
            const(SGU_GROUPS, SGU_BLOCK, SGU_BLOCK),
            const(SGU_BLOCK, SGU_WIDTH),
            const(1, SGU_WIDTH),
        ],
        out_specs=[
            pl.BlockSpec((tm, 3 * SB_WIDTH), lambda i: (i, 0)),
            pl.BlockSpec((tm, SGU_WIDTH), lambda i: (i, 0)),
            pl.BlockSpec((tm, d), lambda i: (i, 0)),
            pl.BlockSpec((tm, d), lambda i: (i, 0)),
        ],
        out_shape=[
            jax.ShapeDtypeStruct((t, 3 * SB_WIDTH), BF16),
            jax.ShapeDtypeStruct((t, SGU_WIDTH), BF16),
            jax.ShapeDtypeStruct((t, d), BF16),
            jax.ShapeDtypeStruct((t, d), BF16),
        ],
        compiler_params=_cparams(("arbitrary",)),
        name="inproj_sgu",
    )(x2, g_mix, w_in, sgu_w, sgu_bias, sgu_norm)


def _attn_kernel(q_ref, k_ref, v_ref, ntri_ref, o_ref, carry_sc, acc_sc):
    tq = q_ref.shape[1]
    kc = KC_ATTN
    qi = pl.program_id(2)
    lane = lax.broadcasted_iota(jnp.int32, (tq, LANES), 1)
    head0 = lane < SB_HEAD_DIM
    zero = jnp.zeros((), BF16)
    q = q_ref[0]
    qh = (jnp.where(head0, q, zero), jnp.where(head0, zero, q))
    row = lax.broadcasted_iota(jnp.int32, (tq, kc), 0)
    col = lax.broadcasted_iota(jnp.int32, (tq, kc), 1)
    carry_sc[...] = jnp.zeros_like(carry_sc)
    acc_sc[...] = jnp.zeros_like(acc_sc)

    def step(j, key_off):
        k0 = pl.multiple_of(j * kc, kc)
        kj = k_ref[0, pl.ds(k0, kc), :]
        vj = v_ref[0, pl.ds(k0, kc), :]
        mask = None if key_off is None else (col + key_off) < row
        for h in range(2):
            z = lax.dot_general(qh[h], kj, (((1,), (1,)), ((), ())), preferred_element_type=F32)
            sp = jnp.maximum(z, 0.0) + jnp.log2(1.0 + jnp.exp2(-jnp.abs(z)))
            if mask is not None:
                sp = jnp.where(mask, sp, 0.0)
            cum = jnp.dot(sp.astype(BF16), ntri_ref[...], preferred_element_type=F32)
            carry = carry_sc[h]
            w = jnp.exp2(z + cum + jnp.concatenate([carry] * (kc // LANES), axis=1))
            carry_sc[h] = carry + jnp.broadcast_to(cum[:, 0:1], carry.shape)
            if mask is not None:
                w = jnp.where(mask, w, 0.0)
            acc_sc[h] += jnp.dot(w.astype(BF16), vj, preferred_element_type=F32)

    ndiag = tq // kc
    for c in reversed(range(ndiag)):
        step(qi * ndiag + c, c * kc)

    def live(t):
        return jnp.logical_and(t < qi * ndiag, jnp.max(carry_sc[...]) > UNDERFLOW_LOG2)

    def past(t):
        step(qi * ndiag - 1 - t, None)
        return t + 1

    lax.while_loop(live, past, 0)
    o_ref[0] = jnp.where(head0, acc_sc[0], acc_sc[1]).astype(o_ref.dtype)


def _attention(qkv3, ntri):
    b, s, _ = qkv3.shape
    tq = TQ_ATTN
    npair = SB_WIDTH // LANES
    return pl.pallas_call(
        _attn_kernel,
        grid=(b, npair, s // tq),
        in_specs=[
            pl.BlockSpec((1, tq, LANES), lambda bi, p, i: (bi, i, p)),
            pl.BlockSpec((1, s, LANES), lambda bi, p, i: (bi, 0, npair + p)),
            pl.BlockSpec((1, s, LANES), lambda bi, p, i: (bi, 0, 2 * npair + p)),
            pl.BlockSpec((KC_ATTN, KC_ATTN), lambda bi, p, i: (0, 0)),
        ],
        out_specs=pl.BlockSpec((1, tq, LANES), lambda bi, p, i: (bi, i, p)),
        out_shape=jax.ShapeDtypeStruct((b, s, SB_WIDTH), BF16),
        scratch_shapes=[pltpu.VMEM((2, tq, LANES), F32), pltpu.VMEM((2, tq, LANES), F32)],
        compiler_params=_cparams(("arbitrary", "arbitrary", "arbitrary")),
        name="stickbreak_attn",
    )(qkv3, qkv3, qkv3, ntri)


def _outproj_kernel(oa_ref, ob_ref, sga_ref, sgb_ref, x_ref, woa_ref, wob_ref, wo_ref,
                    g_ref, wr_ref, br_ref, upper_ref,
                    h_ref, c_ref, idx_ref, gate_ref, rank_ref, cnt_ref, cnt_sc):
    i = pl.program_id(0)
    tm = x_ref.shape[0]

    @pl.when(i == 0)
    def _():
        cnt_sc[...] = jnp.zeros_like(cnt_sc)

    ma = jnp.dot(oa_ref[...], woa_ref[...], preferred_element_type=F32)
    mb = jnp.dot(ob_ref[...], wob_ref[...], preferred_element_type=F32)
    merged = sga_ref[...].astype(F32) * ma + sgb_ref[...].astype(F32) * mb
    h = x_ref[...] + jnp.dot(merged.astype(BF16), wo_ref[...], preferred_element_type=F32)
    h_ref[...] = h
    c = _rms(h, g_ref[...])
    c_ref[...] = c

    logits = lax.dot_general(wr_ref[...], c, (((1,), (1,)), ((), ())),
                             precision=lax.Precision.HIGHEST,
                             preferred_element_type=F32) + br_ref[...]
    eid = lax.broadcasted_iota(jnp.int32, (N_EXPERTS, tm), 0).astype(F32)
    work = logits
    vals, sels, ids = [], [], []
    for _ in range(TOP_K):
        m = jnp.max(work, axis=0, keepdims=True)
        ik = jnp.min(jnp.where(work == m, eid, float(N_EXPERTS)), axis=0, keepdims=True)
        sel = eid == ik
        vals.append(m)
        ids.append(ik)
        sels.append(sel)
        work = jnp.where(sel, -jnp.inf, work)
    es = [jnp.exp(v - vals[0]) for v in vals]
    inv = 1.0 / (es[0] + es[1] + es[2] + es[3])
    onehot = jnp.zeros((N_EXPERTS, tm), F32)
    for sel in sels:
        onehot = onehot + jnp.where(sel, 1.0, 0.0)
    prefix = jnp.dot(onehot.astype(BF16), upper_ref[...], preferred_element_type=F32)
    prefix = prefix + cnt_sc[...][:, 0:1]
    for k in range(TOP_K):
        idx_ref[k:k + 1, :] = ids[k].astype(jnp.int32)
        gate_ref[k:k + 1, :] = es[k] * inv
        rank_ref[k:k + 1, :] = jnp.sum(jnp.where(sels[k], prefix, 0.0), axis=0,
                                       keepdims=True).astype(jnp.int32)
    cnt_sc[...] = cnt_sc[...] + jnp.sum(onehot, axis=1, keepdims=True)
    cnt_ref[...] = cnt_sc[...]


def _outproj(o_a, o_b, sga, sgb, x2, w_out_a, w_out_b, w_out, g_moe, w_router_t, b_router, upper):
    t, d = x2.shape
    tm = TM_PROJ
    const = lambda *shape: pl.BlockSpec(shape, lambda i: (0,) * len(shape))
    row = lambda w: pl.BlockSpec((tm, w), lambda i: (i, 0))
    colk = pl.BlockSpec((TOP_K, tm), lambda i: (0, i))
    return pl.pallas_call(
        _outproj_kernel,
        grid=(t // tm,),
        in_specs=[row(SB_WIDTH), row(SGU_WIDTH), row(d), row(d), row(d),
                  const(SB_WIDTH, d), const(SGU_WIDTH, d), const(d, d),
                  const(1, d), const(N_EXPERTS, d), const(N_EXPERTS, 1), const(tm, tm)],
        out_specs=[row(d), row(d), colk, colk, colk, const(N_EXPERTS, LANES)],
        out_shape=[
            jax.ShapeDtypeStruct((t, d), F32),
            jax.ShapeDtypeStruct((t, d), F32),
            jax.ShapeDtypeStruct((TOP_K, t), jnp.int32),
            jax.ShapeDtypeStruct((TOP_K, t), F32),
            jax.ShapeDtypeStruct((TOP_K, t), jnp.int32),
            jax.ShapeDtypeStruct((N_EXPERTS, LANES), F32),
        ],
        scratch_shapes=[pltpu.VMEM((N_EXPERTS, LANES), F32)],
        compiler_params=_cparams(("arbitrary",)),
        name="outproj_router",
    )(o_a, o_b, sga, sgb, x2, w_out_a, w_out_b, w_out, g_moe, w_router_t, b_router, upper)


def _dispatch_kernel(pad_end_ref, dest_ref, c_ref, xs_ref, zero_sc, sem, zsem):
    tm = c_ref.shape[0]
    tb = zero_sc.shape[0]

    @pl.when(pl.program_id(0) == 0)
    def _():
        zero_sc[...] = jnp.zeros_like(zero_sc)

        def pad_copy(e):
            return pltpu.make_async_copy(
                zero_sc, xs_ref.at[pl.ds(pl.multiple_of(pad_end_ref[e] - tb, tb), tb), :], zsem)

        def nonempty(e):
            prev = jnp.where(e > 0, pad_end_ref[jnp.maximum(e - 1, 0)], 0)
            return pad_end_ref[e] > prev

        def zstart(e, _):
            @pl.when(nonempty(e))
            def _():
                pad_copy(e).start()
            return 0

        def zwait(e, _):
            @pl.when(nonempty(e))
            def _():
                pad_copy(e).wait()
            return 0

        lax.fori_loop(0, N_EXPERTS, zstart, 0)
        lax.fori_loop(0, N_EXPERTS, zwait, 0)

    def row_copy(r, k):
        return pltpu.make_async_copy(c_ref.at[pl.ds(r, 1), :],
                                     xs_ref.at[pl.ds(dest_ref[k, r], 1), :], sem)

    def issue(r, _):
        for k in range(TOP_K):
            row_copy(r, k).start()
        return 0

    lax.fori_loop(0, tm, issue, 0)

    def drain(r, _):
        for k in range(TOP_K):
            row_copy(r, k).wait()
        return 0

    lax.fori_loop(0, tm, drain, 0)


def _dispatch(pad_end, dest, c, n_slots):
    t, d = c.shape
    tm = TM_DISPATCH
    grid_spec = pltpu.PrefetchScalarGridSpec(
        num_scalar_prefetch=1,
        grid=(t // tm,),
        in_specs=[
            pl.BlockSpec((TOP_K, tm), lambda i, pe: (0, i), memory_space=pltpu.SMEM),
            pl.BlockSpec((tm, d), lambda i, pe: (i, 0)),
        ],
        out_specs=pl.BlockSpec(memory_space=pl.ANY),
        scratch_shapes=[pltpu.VMEM((TB_EXPERT, d), F32),
                        pltpu.SemaphoreType.DMA(()), pltpu.SemaphoreType.DMA(())],
    )
    return pl.pallas_call(
        _dispatch_kernel,
        grid_spec=grid_spec,
        out_shape=jax.ShapeDtypeStruct((n_slots, d), F32),
        compiler_params=_cparams(("arbitrary",)),
        name="moe_dispatch",
    )(pad_end, dest, c)


def _expert_kernel(be_ref, nb_ref, xs_ref, w1_ref, w2_ref, b1g_ref, b1l_ref, b2_ref, perm_ref,
                   ys_ref, w1g_sc, w1l_sc, w2_sc):
    i = pl.program_id(0)
    active = i < nb_ref[0]
    fresh = jnp.logical_or(i == 0, be_ref[i] != be_ref[jnp.maximum(i - 1, 0)])

    @pl.when(jnp.logical_and(active, fresh))
    def _():
        half = MXU_DIM // 2
        for cb in range(w1_ref.shape[2] // MXU_DIM):
            blk = w1_ref[0, :, cb * MXU_DIM:(cb + 1) * MXU_DIM].astype(BF16)
            de = jnp.dot(blk, perm_ref[...], preferred_element_type=F32)
            w1g_sc[:, cb * half:(cb + 1) * half] = de[:, :half].astype(BF16)
            w1l_sc[:, cb * half:(cb + 1) * half] = de[:, half:].astype(BF16)
        w2_sc[...] = w2_ref[0].astype(BF16)

    @pl.when(active)
    def _():
        xb = xs_ref[...].astype(BF16)
        hg = jnp.dot(xb, w1g_sc[...], preferred_element_type=F32) + b1g_ref[0]
        hl = jnp.dot(xb, w1l_sc[...], preferred_element_type=F32) + b1l_ref[0]
        xg = jnp.minimum(hg, SWIGLU_LIMIT)
        xl = jnp.clip(hl, -SWIGLU_LIMIT, SWIGLU_LIMIT)
        act = xg * jax.nn.sigmoid(SWIGLU_ALPHA * xg) * (xl + 1.0)
        ys_ref[...] = jnp.dot(act.astype(BF16), w2_sc[...], preferred_element_type=F32) + b2_ref[0]


def _experts(block_e, n_used, xs, w1, w2, b1g, b1l, b2, perm):
    n_slots, d = xs.shape
    f2 = w1.shape[2]
    f = f2 // 2
    tb = TB_EXPERT
    blk = lambda i, be, nb: (jnp.minimum(i, nb[0] - 1), 0)
    wsel = lambda i, be, nb: (be[i], 0, 0)
    grid_spec = pltpu.PrefetchScalarGridSpec(
        num_scalar_prefetch=2,
        grid=(n_slots // tb,),
        in_specs=[
            pl.BlockSpec((tb, d), blk),
            pl.BlockSpec((1, d, f2), wsel),
            pl.BlockSpec((1, f, d), wsel),
            pl.BlockSpec((1, 1, f), wsel),
            pl.BlockSpec((1, 1, f), wsel),
            pl.BlockSpec((1, 1, d), wsel),
            pl.BlockSpec((MXU_DIM, MXU_DIM), lambda i, be, nb: (0, 0)),
        ],
        out_specs=pl.BlockSpec((tb, d), blk),
        scratch_shapes=[pltpu.VMEM((d, f), BF16), pltpu.VMEM((d, f), BF16),
                        pltpu.VMEM((f, d), BF16)],
    )
    return pl.pallas_call(
        _expert_kernel,
        grid_spec=grid_spec,
        out_shape=jax.ShapeDtypeStruct((n_slots, d), F32),
        compiler_params=_cparams(("arbitrary",)),
        name="moe_experts",
    )(block_e, n_used, xs, w1, w2, b1g, b1l, b2, perm)


def _combine_kernel(dest_ref, gate_ref, h_ref, p_ref, ys_ref, gp_ref, wg_ref, wp_ref, gf_ref,
                    out_ref, buf, sem, *, final):
    tm = h_ref.shape[0]

    def row_copy(r, k):
        return pltpu.make_async_copy(ys_ref.at[pl.ds(dest_ref[k, r], 1), :],
                                     buf.at[k, pl.ds(r, 1), :], sem)

    def issue(r, _):
        for k in range(TOP_K):
            row_copy(r, k).start()
        return 0

    lax.fori_loop(0, tm, issue, 0)

    def drain(r, _):
        for k in range(TOP_K):
            row_copy(r, k).wait()
        return 0

    lax.fori_loop(0, tm, drain, 0)

    h = h_ref[...]
    for k in range(TOP_K):
        h = h + gate_ref[:, k:k + 1] * buf[k]
    r = _rms(h, gp_ref[...]).astype(BF16)
    pg = jax.nn.sigmoid(jnp.dot(r, wg_ref[...], preferred_element_type=F32))
    pp = jnp.dot(p_ref[...].astype(BF16), wp_ref[...], preferred_element_type=F32)
    h = h + pg * pp
    out_ref[...] = _rms(h, gf_ref[...]) if final else h


def _combine(dest, gate_tk, h1, p2, ys, g_ple, w_ple_gate, w_ple_proj, g_final, final):
    t, d = h1.shape
    pd = p2.shape[1]
    tm = TM_COMBINE
    const = lambda *shape: pl.BlockSpec(shape, lambda i: (0,) * len(shape))
    return pl.pallas_call(
        functools.partial(_combine_kernel, final=final),
        grid=(t // tm,),
        in_specs=[
            pl.BlockSpec((TOP_K, tm), lambda i: (0, i), memory_space=pltpu.SMEM),
            pl.BlockSpec((tm, TOP_K), lambda i: (i, 0)),
            pl.BlockSpec((tm, d), lambda i: (i, 0)),
            pl.BlockSpec((tm, pd), lambda i: (i, 0)),
            pl.BlockSpec(memory_space=pl.ANY),
            const(1, d), const(d, d), const(pd, d), const(1, d),
        ],
        out_specs=pl.BlockSpec((tm, d), lambda i: (i, 0)),
        out_shape=jax.ShapeDtypeStruct((t, d), F32),
        scratch_shapes=[pltpu.VMEM((TOP_K, tm, d), F32), pltpu.SemaphoreType.DMA(())],
        compiler_params=_cparams(("arbitrary",)),
        name="combine_ple",
    )(dest, gate_tk, h1, p2, ys, g_ple, w_ple_gate, w_ple_proj, g_final)


def kernel(x, p, g_mix, w_in, w_out_a, w_out_b, w_out, sgu_norm, sgu_w, sgu_b, g_moe,
           w_router, b_router, w1, b1, w2, b2, g_ple, w_ple_gate, w_ple_proj, g_final):
    b, s, d = x.shape
    depth = w_in.shape[0]
    t = b * s
    assert s % TQ_ATTN == 0 and t % TM_PROJ == 0 and t % TM_COMBINE == 0

    kk = jnp.arange(KC_ATTN)
    ntri = -(kk[:, None] >= kk[None, :]).astype(BF16)
    tt = jnp.arange(TM_PROJ)
    upper = (tt[:, None] < tt[None, :]).astype(BF16)
    rr = jnp.arange(MXU_DIM)[:, None]
    cc = jnp.arange(MXU_DIM)[None, :]
    half = MXU_DIM // 2
    perm = (rr == jnp.where(cc < half, 2 * cc, 2 * (cc - half) + 1)).astype(BF16)

    tb = TB_EXPERT
    n_blocks = -(-(t * TOP_K + N_EXPERTS * (tb - 1)) // tb)
    n_slots = n_blocks * tb
    experts = jnp.arange(N_EXPERTS, dtype=jnp.int32)

    h = x.reshape(t, d)
    for i in range(depth):
        sgu_bias = jnp.repeat(sgu_b[i].T, SGU_GROUP_DIM, axis=1)
        qkv, o_b, sga, sgb = _inproj(h, g_mix[i][None], w_in[i].astype(BF16), sgu_w[i],
                                     sgu_bias, sgu_norm[i][None])
        o_a = _attention(qkv.reshape(b, s, 3 * SB_WIDTH), ntri).reshape(t, SB_WIDTH)
        h1, c, idx, gate, rank, cnt = _outproj(
            o_a, o_b, sga, sgb, h, w_out_a[i].astype(BF16), w_out_b[i].astype(BF16),
            w_out[i].astype(BF16), g_moe[i][None], w_router[i].T, b_router[i][:, None], upper)

        counts = cnt[:, 0].astype(jnp.int32)
        padded = ((counts + tb - 1) // tb) * tb
        pad_end = jnp.cumsum(padded).astype(jnp.int32)
        pad_off = pad_end - padded
        dest = rank + jnp.sum(jnp.where(idx[:, :, None] == experts, pad_off, 0), axis=-1)
        starts = jnp.arange(n_blocks, dtype=jnp.int32) * tb
        block_e = jnp.minimum(jnp.sum(starts[:, None] >= pad_end[None, :], axis=1),
                              N_EXPERTS - 1).astype(jnp.int32)
        n_used = (pad_end[-1:] // tb).astype(jnp.int32)

        xs = _dispatch(pad_end, dest, c, n_slots)
        ys = _experts(block_e, n_used, xs, w1[i], w2[i],
                      b1[i][:, None, 0::2], b1[i][:, None, 1::2], b2[i][:, None, :], perm)
        h = _combine(dest, gate.T, h1, p[i].reshape(t, -1), ys, g_ple[i][None],
                     w_ple_gate[i].astype(BF16), w_ple_proj[i].astype(BF16), g_final[None],
                     final=(i == depth - 1))
    return h.reshape(b, s, d)
```

```python
import functools
import math

import jax
import jax.numpy as jnp
from jax import lax
from jax.experimental import pallas as pl
from jax.experimental.pallas import tpu as pltpu

F32 = jnp.float32
BF16 = jnp.bfloat16

EPS = 1e-6
CHUNK = 64
SB_HEADS = 8
SB_HEAD_DIM = 64
SB_WIDTH = SB_HEADS * SB_HEAD_DIM
SGU_GROUPS = 8
SGU_WIDTH = 512
SGU_GROUP_DIM = SGU_WIDTH // SGU_GROUPS
SGU_BLOCK = 128
N_EXPERTS = 32
TOP_K = 4
SWIGLU_ALPHA = 1.702
SWIGLU_LIMIT = 7.0
LOG2E = 1.4426950408889634
UNDERFLOW_LOG2 = -160.0

LANES = 128
MXU_DIM = 256
VMEM_LIMIT_BYTES = 56 * 1024 * 1024

TM_PROJ = 512
TQ_ATTN = 512
KC_ATTN = MXU_DIM
TB_EXPERT = 256
ROW_ALIGN = 8
SLOT_CHUNK = 256
LOCAL_SLOTS = -(-(TM_PROJ * TOP_K + N_EXPERTS * (ROW_ALIGN - 1)) // SLOT_CHUNK) * SLOT_CHUNK


def _cparams(sem):
    return pltpu.CompilerParams(dimension_semantics=sem,
                                vmem_limit_bytes=VMEM_LIMIT_BYTES)


def _rms(x, g):
    ms = jnp.mean(x * x, axis=-1, keepdims=True)
    return x * lax.rsqrt(ms + EPS) * g


def _inproj_kernel(x_ref, g_ref, w_ref, sw_ref, sbias_ref, sn_ref,
                   qkv_ref, ob_ref, sga_ref, sgb_ref):
    tm = x_ref.shape[0]
    a = _rms(x_ref[...], g_ref[...]).astype(BF16)

    def proj(c0, width):
        return jnp.dot(a, w_ref[:, c0:c0 + width], preferred_element_type=F32)

    qkv_ref[:, 0:SB_WIDTH] = (proj(0, SB_WIDTH) * (LOG2E / math.sqrt(SB_HEAD_DIM))).astype(BF16)
    qkv_ref[:, SB_WIDTH:2 * SB_WIDTH] = proj(SB_WIDTH, SB_WIDTH).astype(BF16)
    qkv_ref[:, 2 * SB_WIDTH:3 * SB_WIDTH] = proj(2 * SB_WIDTH, SB_WIDTH).astype(BF16)

    c_u = 3 * SB_WIDTH
    gu = jax.nn.gelu(proj(c_u, SGU_WIDTH))
    gv = jax.nn.gelu(proj(c_u + SGU_WIDTH, SGU_WIDTH))
    mu = jnp.mean(gv, axis=-1, keepdims=True)
    d = gv - mu
    var = jnp.mean(d * d, axis=-1, keepdims=True)
    vn = (d * lax.rsqrt(var + EPS) * sn_ref[...]).astype(BF16)

    rr = lax.broadcasted_iota(jnp.int32, (SGU_BLOCK, SGU_BLOCK), 0) // CHUNK
    cc = lax.broadcasted_iota(jnp.int32, (SGU_BLOCK, SGU_BLOCK), 1) // CHUNK
    cmask = cc <= rr
    lane = lax.broadcasted_iota(jnp.int32, (SGU_BLOCK, LANES), 1)
    lo_mask = lane < SGU_GROUP_DIM
    wcat = []
    for j in range(SGU_GROUPS // 2):
        w0 = jnp.where(cmask, sw_ref[2 * j], 0.0).astype(BF16)
        w1 = jnp.where(cmask, sw_ref[2 * j + 1], 0.0).astype(BF16)
        wcat.append(jnp.concatenate([w0, w1], axis=1))

    zero = jnp.zeros((), BF16)
    for blk in range(tm // SGU_BLOCK):
        r0 = blk * SGU_BLOCK
        for j in range(SGU_GROUPS // 2):
            vp = vn[r0:r0 + SGU_BLOCK, j * LANES:(j + 1) * LANES]
            rhs = jnp.concatenate([jnp.where(lo_mask, vp, zero),
                                   jnp.where(lo_mask, zero, vp)], axis=0)
            sv = jnp.dot(wcat[j], rhs, preferred_element_type=F32)
            sv = sv + sbias_ref[:, j * LANES:(j + 1) * LANES]
            ob_ref[r0:r0 + SGU_BLOCK, j * LANES:(j + 1) * LANES] = (
                gu[r0:r0 + SGU_BLOCK, j * LANES:(j + 1) * LANES] * sv).astype(BF16)

    c_g = c_u + 2 * SGU_WIDTH
    d_model = sga_ref.shape[1]
    for c0 in range(0, d_model, 512):
        sga_ref[:, c0:c0 + 512] = jax.nn.sigmoid(proj(c_g + c0, 512)).astype(BF16)
        sgb_ref[:, c0:c0 + 512] = jax.nn.sigmoid(proj(c_g + d_model + c0, 512)).astype(BF16)


def _inproj(x2, g_mix, w_in, sgu_w, sgu_bias, sgu_norm):
    t, d = x2.shape
    ncol = w_in.shape[1]
    tm = TM_PROJ
    const = lambda *shape: pl.BlockSpec(shape, lambda i: (0,) * len(shape))
    return pl.pallas_call(
        _inproj_kernel,
        grid=(t // tm,),
        in_specs=[
            pl.BlockSpec((tm, d), lambda i: (i, 0)),
            const(1, d),
            const(d, ncol),
            const(SGU_GROUPS, SGU_BLOCK, SGU_BLOCK),
            const(SGU_BLOCK, SGU_WIDTH),
            const(1, SGU_WIDTH),
        ],
        out_specs=[
            pl.BlockSpec((tm, 3 * SB_WIDTH), lambda i: (i, 0)),
            pl.BlockSpec((tm, SGU_WIDTH), lambda i: (i, 0)),
            pl.BlockSpec((tm, d), lambda i: (i, 0)),
            pl.BlockSpec((tm, d), lambda i: (i, 0)),
        ],
        out_shape=[
            jax.ShapeDtypeStruct((t, 3 * SB_WIDTH), BF16),
            jax.ShapeDtypeStruct((t, SGU_WIDTH), BF16),
            jax.ShapeDtypeStruct((t, d), BF16),
            jax.ShapeDtypeStruct((t, d), BF16),
        ],
        compiler_params=_cparams(("arbitrary",)),
        name="inproj_sgu",
    )(x2, g_mix, w_in, sgu_w, sgu_bias, sgu_norm)


def _attn_kernel(q_ref, k_ref, v_ref, ntri_ref, o_ref, carry_sc, acc_sc):
    tq = q_ref.shape[1]
    kc = KC_ATTN
    qi = pl.program_id(2)
    lane = lax.broadcasted_iota(jnp.int32, (tq, LANES), 1)
    head0 = lane < SB_HEAD_DIM
    zero = jnp.zeros((), BF16)
    q = q_ref[0]
    qh = (jnp.where(head0, q, zero), jnp.where(head0, zero, q))
    row = lax.broadcasted_iota(jnp.int32, (tq, kc), 0)
    col = lax.broadcasted_iota(jnp.int32, (tq, kc), 1)
    carry_sc[...] = jnp.zeros_like(carry_sc)
    acc_sc[...] = jnp.zeros_like(acc_sc)

    def step(j, key_off):
        k0 = pl.multiple_of(j * kc, kc)
        kj = k_ref[0, pl.ds(k0, kc), :]
        vj = v_ref[0, pl.ds(k0, kc), :]
        mask = None if key_off is None else (col + key_off) < row
        for h in range(2):
            z = lax.dot_general(qh[h], kj, (((1,), (1,)), ((), ())), preferred_element_type=F32)
            sp = jnp.maximum(z, 0.0) + jnp.log2(1.0 + jnp.exp2(-jnp.abs(z)))
            if mask is not None:
                sp = jnp.where(mask, sp, 0.0)
            cum = jnp.dot(sp.astype(BF16), ntri_ref[...], preferred_element_type=F32)
            carry = carry_sc[h]
            w = jnp.exp2(z + cum + jnp.concatenate([carry] * (kc // LANES), axis=1))
            carry_sc[h] = carry + jnp.broadcast_to(cum[:, 0:1], carry.shape)
            if mask is not None:
                w = jnp.where(mask, w, 0.0)
            acc_sc[h] += jnp.dot(w.astype(BF16), vj, preferred_element_type=F32)

    ndiag = tq // kc
    for c in reversed(range(ndiag)):
        step(qi * ndiag + c, c * kc)

    def live(t):
        return jnp.logical_and(t < qi * ndiag, jnp.max(carry_sc[...]) > UNDERFLOW_LOG2)

    def past(t):
        step(qi * ndiag - 1 - t, None)
        return t + 1

    lax.while_loop(live, past, 0)
    o_ref[0] = jnp.where(head0, acc_sc[0], acc_sc[1]).astype(o_ref.dtype)


def _attention(qkv3, ntri):
    b, s, _ = qkv3.shape
    tq = TQ_ATTN
    npair = SB_WIDTH // LANES
    return pl.pallas_call(
        _attn_kernel,
        grid=(b, npair, s // tq),
        in_specs=[
            pl.BlockSpec((1, tq, LANES), lambda bi, p, i: (bi, i, p)),
            pl.BlockSpec((1, s, LANES), lambda bi, p, i: (bi, 0, npair + p)),
            pl.BlockSpec((1, s, LANES), lambda bi, p, i: (bi, 0, 2 * npair + p)),
            pl.BlockSpec((KC_ATTN, KC_ATTN), lambda bi, p, i: (0, 0)),
        ],
        out_specs=pl.BlockSpec((1, tq, LANES), lambda bi, p, i: (bi, i, p)),
        out_shape=jax.ShapeDtypeStruct((b, s, SB_WIDTH), BF16),
        scratch_shapes=[pltpu.VMEM((2, tq, LANES), F32), pltpu.VMEM((2, tq, LANES), F32)],
        compiler_params=_cparams(("arbitrary", "arbitrary", "arbitrary")),
        name="stickbreak_attn",
    )(qkv3, qkv3, qkv3, ntri)


def _outproj_kernel(oa_ref, ob_ref, sga_ref, sgb_ref, x_ref, woa_ref, wob_ref, wo_ref,
                    g_ref, wr_ref, br_ref, upper_ref,
                    h_ref, c_ref, lslot_ref, gate_ref, cnt_ref):
    tm = x_ref.shape[0]
    ma = jnp.dot(oa_ref[...], woa_ref[...], preferred_element_type=F32)
    mb = jnp.dot(ob_ref[...], wob_ref[...], preferred_element_type=F32)
    merged = sga_ref[...].astype(F32) * ma + sgb_ref[...].astype(F32) * mb
    h = x_ref[...] + jnp.dot(merged.astype(BF16), wo_ref[...], preferred_element_type=F32)
    h_ref[...] = h
    c = _rms(h, g_ref[...])
    c_ref[...] = c.astype(BF16)

    logits = lax.dot_general(wr_ref[...], c, (((1,), (1,)), ((), ())),
                             precision=lax.Precision.HIGHEST,
                             preferred_element_type=F32) + br_ref[...]
    eid = lax.broadcasted_iota(jnp.int32, (N_EXPERTS, tm), 0).astype(F32)
    work = logits
    vals, sels = [], []
    for _ in range(TOP_K):
        m = jnp.max(work, axis=0, keepdims=True)
        ik = jnp.min(jnp.where(work == m, eid, float(N_EXPERTS)), axis=0, keepdims=True)
        sel = eid == ik
        vals.append(m)
        sels.append(sel)
        work = jnp.where(sel, -jnp.inf, work)
    es = [jnp.exp(v - vals[0]) for v in vals]
    inv = 1.0 / (es[0] + es[1] + es[2] + es[3])
    onehot = jnp.zeros((N_EXPERTS, tm), F32)
    for sel in sels:
        onehot = onehot + jnp.where(sel, 1.0, 0.0)

    prefix = jnp.dot(onehot.astype(BF16), upper_ref[...], preferred_element_type=F32)
    n = jnp.sum(onehot, axis=1, keepdims=True)
    n_al = jnp.ceil(n * (1.0 / ROW_ALIGN)) * ROW_ALIGN
    er = lax.broadcasted_iota(jnp.int32, (N_EXPERTS, N_EXPERTS), 0)
    ec = lax.broadcasted_iota(jnp.int32, (N_EXPERTS, N_EXPERTS), 1)
    run_off = jnp.dot(jnp.where(ec < er, 1.0, 0.0), jnp.broadcast_to(n_al, (N_EXPERTS, LANES)),
                      precision=lax.Precision.HIGHEST, preferred_element_type=F32)
    slot = prefix + run_off[:, 0:1]
    for k in range(TOP_K):
        gate_ref[k:k + 1, :] = es[k] * inv
        lslot_ref[k:k + 1, :] = jnp.sum(jnp.where(sels[k], slot, 0.0), axis=0,
                                        keepdims=True).astype(jnp.int32)
    cnt_ref[0] = jnp.broadcast_to(n, (N_EXPERTS, LANES))


def _outproj(o_a, o_b, sga, sgb, x2, w_out_a, w_out_b, w_out, g_moe, w_router_t, b_router, upper):
    t, d = x2.shape
    tm = TM_PROJ
    const = lambda *shape: pl.BlockSpec(shape, lambda i: (0,) * len(shape))
    row = lambda w: pl.BlockSpec((tm, w), lambda i: (i, 0))
    colk = pl.BlockSpec((TOP_K, tm), lambda i: (0, i))
    return pl.pallas_call(
        _outproj_kernel,
        grid=(t // tm,),
        in_specs=[row(SB_WIDTH), row(SGU_WIDTH), row(d), row(d), row(d),
                  const(SB_WIDTH, d), const(SGU_WIDTH, d), const(d, d),
                  const(1, d), const(N_EXPERTS, d), const(N_EXPERTS, 1), const(tm, tm)],
        out_specs=[row(d), row(d), colk, colk,
                   pl.BlockSpec((1, N_EXPERTS, LANES), lambda i: (i, 0, 0))],
        out_shape=[
            jax.ShapeDtypeStruct((t, d), F32),
            jax.ShapeDtypeStruct((t, d), BF16),
            jax.ShapeDtypeStruct((TOP_K, t), jnp.int32),
            jax.ShapeDtypeStruct((TOP_K, t), F32),
            jax.ShapeDtypeStruct((t // tm, N_EXPERTS, LANES), F32),
        ],
        compiler_params=_cparams(("arbitrary",)),
        name="outproj_router",
    )(o_a, o_b, sga, sgb, x2, w_out_a, w_out_b, w_out, g_moe, w_router_t, b_router, upper)


def _run_loops(nch_ref, w, fn):
    def per_expert(e, _):
        def per_piece(j, _):
            fn(e, j)
            return 0
        lax.fori_loop(0, nch_ref[w * N_EXPERTS + e], per_piece, 0)
        return 0
    lax.fori_loop(0, N_EXPERTS, per_expert, 0)


def _dispatch_kernel(base_ref, nch_ref, loff_ref, pad_end_ref, lslot_ref, c_ref, xs_ref,
                     perm_sc, loc_sc, zero_sc, sem, zsem):
    w = pl.program_id(0)
    tm = c_ref.shape[0]
    tb = zero_sc.shape[0]
    nloc = loc_sc.shape[0]

    @pl.when(w == 0)
    def _():
        zero_sc[...] = jnp.zeros_like(zero_sc)

        def pad_copy(e):
            return pltpu.make_async_copy(
                zero_sc, xs_ref.at[pl.ds(pl.multiple_of(pad_end_ref[e] - tb, tb), tb), :], zsem)

        def nonempty(e):
            prev = jnp.where(e > 0, pad_end_ref[jnp.maximum(e - 1, 0)], 0)
            return pad_end_ref[e] > prev

        def zstart(e, _):
            @pl.when(nonempty(e))
            def _():
                pad_copy(e).start()
            return 0

        def zwait(e, _):
            @pl.when(nonempty(e))
            def _():
                pad_copy(e).wait()
            return 0

        lax.fori_loop(0, N_EXPERTS, zstart, 0)
        lax.fori_loop(0, N_EXPERTS, zwait, 0)

    ls = [lslot_ref[k:k + 1, :] for k in range(TOP_K)]
    for r0 in range(0, nloc, SLOT_CHUNK):
        srow = lax.broadcasted_iota(jnp.int32, (SLOT_CHUNK, tm), 0) + r0
        sel = jnp.zeros((SLOT_CHUNK, tm), F32)
        for k in range(TOP_K):
            sel = jnp.where(srow == ls[k], 1.0, sel)
        perm_sc[r0:r0 + SLOT_CHUNK, :] = sel.astype(BF16)
    loc_sc[...] = jnp.dot(perm_sc[...], c_ref[...], preferred_element_type=F32)

    def piece_copy(e, j):
        s0 = pl.multiple_of(loff_ref[w * N_EXPERTS + e] + j * ROW_ALIGN, ROW_ALIGN)
        d0 = pl.multiple_of(base_ref[w * N_EXPERTS + e] + j * ROW_ALIGN, ROW_ALIGN)
        return pltpu.make_async_copy(loc_sc.at[pl.ds(s0, ROW_ALIGN), :],
                                     xs_ref.at[pl.ds(d0, ROW_ALIGN), :], sem)

    _run_loops(nch_ref, w, lambda e, j: piece_copy(e, j).start())
    _run_loops(nch_ref, w, lambda e, j: piece_copy(e, j).wait())


def _dispatch(base, nch, loff, pad_end, lslot, c, n_slots):
    t, d = c.shape
    tm = TM_PROJ
    grid_spec = pltpu.PrefetchScalarGridSpec(
        num_scalar_prefetch=4,
        grid=(t // tm,),
        in_specs=[
            pl.BlockSpec((TOP_K, tm), lambda i, *_: (0, i)),
            pl.BlockSpec((tm, d), lambda i, *_: (i, 0)),
        ],
        out_specs=pl.BlockSpec(memory_space=pl.ANY),
        scratch_shapes=[pltpu.VMEM((LOCAL_SLOTS, tm), BF16),
                        pltpu.VMEM((LOCAL_SLOTS, d), F32),
                        pltpu.VMEM((TB_EXPERT, d), F32),
                        pltpu.SemaphoreType.DMA(()), pltpu.SemaphoreType.DMA(())],
    )
    return pl.pallas_call(
        _dispatch_kernel,
        grid_spec=grid_spec,
        out_shape=jax.ShapeDtypeStruct((n_slots, d), F32),
        compiler_params=_cparams(("arbitrary",)),
        name="moe_dispatch",
    )(base, nch, loff, pad_end, lslot, c)


def _expert_kernel(be_ref, nb_ref, xs_ref, w1_ref, w2_ref, b1g_ref, b1l_ref, b2_ref, perm_ref,
                   ys_ref, w1g_sc, w1l_sc, w2_sc):
    i = pl.program_id(0)
    active = i < nb_ref[0]
    fresh = jnp.logical_or(i == 0, be_ref[i] != be_ref[jnp.maximum(i - 1, 0)])

    @pl.when(jnp.logical_and(active, fresh))
    def _():
        half = MXU_DIM // 2
        for cb in range(w1_ref.shape[2] // MXU_DIM):
            blk = w1_ref[0, :, cb * MXU_DIM:(cb + 1) * MXU_DIM].astype(BF16)
            de = jnp.dot(blk, perm_ref[...], preferred_element_type=F32)
            w1g_sc[:, cb * half:(cb + 1) * half] = de[:, :half].astype(BF16)
            w1l_sc[:, cb * half:(cb + 1) * half] = de[:, half:].astype(BF16)
        w2_sc[...] = w2_ref[0].astype(BF16)

    @pl.when(active)
    def _():
        xb = xs_ref[...].astype(BF16)
        hg = jnp.dot(xb, w1g_sc[...], preferred_element_type=F32) + b1g_ref[0]
        hl = jnp.dot(xb, w1l_sc[...], preferred_element_type=F32) + b1l_ref[0]
        xg = jnp.minimum(hg, SWIGLU_LIMIT)
        xl = jnp.clip(hl, -SWIGLU_LIMIT, SWIGLU_LIMIT)
        act = xg * jax.nn.sigmoid(SWIGLU_ALPHA * xg) * (xl + 1.0)
        ys_ref[...] = jnp.dot(act.astype(BF16), w2_sc[...], preferred_element_type=F32) + b2_ref[0]


def _experts(block_e, n_used, xs, w1, w2, b1g, b1l, b2, perm):
    n_slots, d = xs.shape
    f2 = w1.shape[2]
    f = f2 // 2
    tb = TB_EXPERT
    blk = lambda i, be, nb: (jnp.minimum(i, nb[0] - 1), 0)
    wsel = lambda i, be, nb: (be[i], 0, 0)
    grid_spec = pltpu.PrefetchScalarGridSpec(
        num_scalar_prefetch=2,
        grid=(n_slots // tb,),
        in_specs=[
            pl.BlockSpec((tb, d), blk),
            pl.BlockSpec((1, d, f2), wsel),
            pl.BlockSpec((1, f, d), wsel),
            pl.BlockSpec((1, 1, f), wsel),
            pl.BlockSpec((1, 1, f), wsel),
            pl.BlockSpec((1, 1, d), wsel),
            pl.BlockSpec((MXU_DIM, MXU_DIM), lambda i, be, nb: (0, 0)),
        ],
        out_specs=pl.BlockSpec((tb, d), blk),
        scratch_shapes=[pltpu.VMEM((d, f), BF16), pltpu.VMEM((d, f), BF16),
                        pltpu.VMEM((f, d), BF16)],
    )
    return pl.pallas_call(
        _expert_kernel,
        grid_spec=grid_spec,
        out_shape=jax.ShapeDtypeStruct((n_slots, d), F32),
        compiler_params=_cparams(("arbitrary",)),
        name="moe_experts",
    )(block_e, n_used, xs, w1, w2, b1g, b1l, b2, perm)


def _combine_kernel(base_ref, nch_ref, loff_ref, lslot_ref, gate_ref, h_ref, p_ref, ys_ref,
                    gp_ref, wg_ref, wp_ref, gf_ref, out_ref, loc_sc, locb_sc, comb_sc, sem,
                    *, final):
    w = pl.program_id(0)
    tm = h_ref.shape[0]
    nloc = loc_sc.shape[0]

    @pl.when(w == 0)
    def _():
        loc_sc[...] = jnp.zeros_like(loc_sc)

    def piece_copy(e, j):
        s0 = pl.multiple_of(loff_ref[w * N_EXPERTS + e] + j * ROW_ALIGN, ROW_ALIGN)
        d0 = pl.multiple_of(base_ref[w * N_EXPERTS + e] + j * ROW_ALIGN, ROW_ALIGN)
        return pltpu.make_async_copy(ys_ref.at[pl.ds(d0, ROW_ALIGN), :],
                                     loc_sc.at[pl.ds(s0, ROW_ALIGN), :], sem)

    _run_loops(nch_ref, w, lambda e, j: piece_copy(e, j).start())
    _run_loops(nch_ref, w, lambda e, j: piece_copy(e, j).wait())

    lcol = [lslot_ref[:, k:k + 1] for k in range(TOP_K)]
    gcol = [gate_ref[:, k:k + 1] for k in range(TOP_K)]
    for c0 in range(0, nloc, SLOT_CHUNK):
        scol = lax.broadcasted_iota(jnp.int32, (tm, SLOT_CHUNK), 1) + c0
        g = jnp.zeros((tm, SLOT_CHUNK), F32)
        for k in range(TOP_K):
            g = jnp.where(scol == lcol[k], gcol[k], g)
        comb_sc[:, c0:c0 + SLOT_CHUNK] = g.astype(BF16)
        locb_sc[c0:c0 + SLOT_CHUNK, :] = loc_sc[c0:c0 + SLOT_CHUNK, :].astype(BF16)
    h = h_ref[...] + jnp.dot(comb_sc[...], locb_sc[...], preferred_element_type=F32)

    r = _rms(h, gp_ref[...]).astype(BF16)
    pg = jax.nn.sigmoid(jnp.dot(r, wg_ref[...], preferred_element_type=F32))
    pp = jnp.dot(p_ref[...].astype(BF16), wp_ref[...], preferred_element_type=F32)
    h = h + pg * pp
    out_ref[...] = _rms(h, gf_ref[...]) if final else h


def _combine(base, nch, loff, lslot_tk, gate_tk, h1, p2, ys, g_ple, w_ple_gate, w_ple_proj,
             g_final, final):
    t, d = h1.shape
    pd = p2.shape[1]
    tm = TM_PROJ
    const = lambda *shape: pl.BlockSpec(shape, lambda i, *_: (0,) * len(shape))
    row = lambda width: pl.BlockSpec((tm, width), lambda i, *_: (i, 0))
    grid_spec = pltpu.PrefetchScalarGridSpec(
        num_scalar_prefetch=3,
        grid=(t // tm,),
        in_specs=[row(TOP_K), row(TOP_K), row(d), row(pd),
                  pl.BlockSpec(memory_space=pl.ANY),
                  const(1, d), const(d, d), const(pd, d), const(1, d)],
        out_specs=row(d),
        scratch_shapes=[pltpu.VMEM((LOCAL_SLOTS, d), F32), pltpu.VMEM((LOCAL_SLOTS, d), BF16),
                        pltpu.VMEM((tm, LOCAL_SLOTS), BF16), pltpu.SemaphoreType.DMA(())],
    )
    return pl.pallas_call(
        functools.partial(_combine_kernel, final=final),
        grid_spec=grid_spec,
        out_shape=jax.ShapeDtypeStruct((t, d), F32),
        compiler_params=_cparams(("arbitrary",)),
        name="combine_ple",
    )(base, nch, loff, lslot_tk, gate_tk, h1, p2, ys, g_ple, w_ple_gate, w_ple_proj, g_final)


def kernel(x, p, g_mix, w_in, w_out_a, w_out_b, w_out, sgu_norm, sgu_w, sgu_b, g_moe,
           w_router, b_router, w1, b1, w2, b2, g_ple, w_ple_gate, w_ple_proj, g_final):
    b, s, d = x.shape
    depth = w_in.shape[0]
    t = b * s
    assert s % TQ_ATTN == 0 and t % TM_PROJ == 0

    kk = jnp.arange(KC_ATTN)
    ntri = -(kk[:, None] >= kk[None, :]).astype(BF16)
    tt = jnp.arange(TM_PROJ)
    upper = (tt[:, None] < tt[None, :]).astype(BF16)
    rr = jnp.arange(MXU_DIM)[:, None]
    cc = jnp.arange(MXU_DIM)[None, :]
    half = MXU_DIM // 2
    perm = (rr == jnp.where(cc < half, 2 * cc, 2 * (cc - half) + 1)).astype(BF16)

    tb = TB_EXPERT
    n_win = t // TM_PROJ
    n_blocks = -(-(t * TOP_K + n_win * N_EXPERTS * (ROW_ALIGN - 1) + N_EXPERTS * (tb - 1)) // tb)
    n_slots = n_blocks * tb

    h = x.reshape(t, d)
    for i in range(depth):
        sgu_bias = jnp.repeat(sgu_b[i].T, SGU_GROUP_DIM, axis=1)
        qkv, o_b, sga, sgb = _inproj(h, g_mix[i][None], w_in[i].astype(BF16), sgu_w[i],
                                     sgu_bias, sgu_norm[i][None])
        o_a = _attention(qkv.reshape(b, s, 3 * SB_WIDTH), ntri).reshape(t, SB_WIDTH)
        h1, c, lslot, gate, cnt = _outproj(
            o_a, o_b, sga, sgb, h, w_out_a[i].astype(BF16), w_out_b[i].astype(BF16),
            w_out[i].astype(BF16), g_moe[i][None], w_router[i].T, b_router[i][:, None], upper)

        counts = cnt[:, :, 0].astype(jnp.int32)
        run = ((counts + ROW_ALIGN - 1) // ROW_ALIGN) * ROW_ALIGN
        loff = jnp.cumsum(run, axis=1) - run
        region = ((jnp.sum(run, axis=0) + tb - 1) // tb) * tb
        pad_end = jnp.cumsum(region).astype(jnp.int32)
        base = (pad_end - region)[None, :] + jnp.cumsum(run, axis=0) - run
        nch = run // ROW_ALIGN
        starts = jnp.arange(n_blocks, dtype=jnp.int32) * tb
        block_e = jnp.minimum(jnp.sum(starts[:, None] >= pad_end[None, :], axis=1),
                              N_EXPERTS - 1).astype(jnp.int32)
        n_used = (pad_end[-1:] // tb).astype(jnp.int32)
        flat = lambda a: a.reshape(-1).astype(jnp.int32)

        xs = _dispatch(flat(base), flat(nch), flat(loff), pad_end, lslot, c, n_slots)
        ys = _experts(block_e, n_used, xs, w1[i], w2[i],
                      b1[i][:, None, 0::2], b1[i][:, None, 1::2], b2[i][:, None, :], perm)
        h = _combine(flat(base), flat(nch), flat(loff), lslot.T, gate.T, h1, p[i].reshape(t, -1),
                     ys, g_ple[i][None], w_ple_gate[i].astype(BF16), w_ple_proj[i].astype(BF16),
                     g_final[None], final=(i == depth - 1))
    return h.reshape(b, s, d)
```

```python
import functools
import math

import jax
import jax.numpy as jnp
from jax import lax
from jax.experimental import pallas as pl
from jax.experimental.pallas import tpu as pltpu

F32 = jnp.float32
BF16 = jnp.bfloat16

EPS = 1e-6
CHUNK = 64
SB_HEADS = 8
SB_HEAD_DIM = 64
SB_WIDTH = SB_HEADS * SB_HEAD_DIM
SGU_GROUPS = 8
SGU_WIDTH = 512
SGU_GROUP_DIM = SGU_WIDTH // SGU_GROUPS
SGU_BLOCK = 128
N_EXPERTS = 32
TOP_K = 4
SWIGLU_ALPHA = 1.702
SWIGLU_LIMIT = 7.0
LOG2E = 1.4426950408889634
UNDERFLOW_LOG2 = -160.0

LANES = 128
MXU_DIM = 256
VMEM_LIMIT_BYTES = 56 * 1024 * 1024

TM_PROJ = 512
TQ_ATTN = 512
KC_ATTN = MXU_DIM
TB_EXPERT = 512
ROW_ALIGN = 8
SLOT_CHUNK = 256
LOCAL_SLOTS = -(-(TM_PROJ * TOP_K + N_EXPERTS * (ROW_ALIGN - 1)) // SLOT_CHUNK) * SLOT_CHUNK


def _cparams(sem):
    return pltpu.CompilerParams(dimension_semantics=sem,
                                vmem_limit_bytes=VMEM_LIMIT_BYTES)


def _rms(x, g):
    ms = jnp.mean(x * x, axis=-1, keepdims=True)
    return x * lax.rsqrt(ms + EPS) * g


def _inproj_kernel(x_ref, g_ref, w_ref, sw_ref, sbias_ref, sn_ref,
                   qkv_ref, ob_ref, sga_ref, sgb_ref):
    tm = x_ref.shape[0]
    a = _rms(x_ref[...], g_ref[...]).astype(BF16)

    def proj(c0, width):
        return jnp.dot(a, w_ref[:, c0:c0 + width], preferred_element_type=F32)

    qkv_ref[:, 0:SB_WIDTH] = (proj(0, SB_WIDTH) * (LOG2E / math.sqrt(SB_HEAD_DIM))).astype(BF16)
    qkv_ref[:, SB_WIDTH:2 * SB_WIDTH] = proj(SB_WIDTH, SB_WIDTH).astype(BF16)
    qkv_ref[:, 2 * SB_WIDTH:3 * SB_WIDTH] = proj(2 * SB_WIDTH, SB_WIDTH).astype(BF16)

    c_u = 3 * SB_WIDTH
    gu = jax.nn.gelu(proj(c_u, SGU_WIDTH))
    gv = jax.nn.gelu(proj(c_u + SGU_WIDTH, SGU_WIDTH))
    mu = jnp.mean(gv, axis=-1, keepdims=True)
    d = gv - mu
    var = jnp.mean(d * d, axis=-1, keepdims=True)
    vn = (d * lax.rsqrt(var + EPS) * sn_ref[...]).astype(BF16)

    rr = lax.broadcasted_iota(jnp.int32, (SGU_BLOCK, SGU_BLOCK), 0) // CHUNK
    cc = lax.broadcasted_iota(jnp.int32, (SGU_BLOCK, SGU_BLOCK), 1) // CHUNK
    cmask = cc <= rr
    lane = lax.broadcasted_iota(jnp.int32, (SGU_BLOCK, LANES), 1)
    lo_mask = lane < SGU_GROUP_DIM
    wcat = []
    for j in range(SGU_GROUPS // 2):
        w0 = jnp.where(cmask, sw_ref[2 * j], 0.0).astype(BF16)
        w1 = jnp.where(cmask, sw_ref[2 * j + 1], 0.0).astype(BF16)
        wcat.append(jnp.concatenate([w0, w1], axis=1))

    zero = jnp.zeros((), BF16)
    for blk in range(tm // SGU_BLOCK):
        r0 = blk * SGU_BLOCK
        for j in range(SGU_GROUPS // 2):
            vp = vn[r0:r0 + SGU_BLOCK, j * LANES:(j + 1) * LANES]
            rhs = jnp.concatenate([jnp.where(lo_mask, vp, zero),
                                   jnp.where(lo_mask, zero, vp)], axis=0)
            sv = jnp.dot(wcat[j], rhs, preferred_element_type=F32)
            sv = sv + sbias_ref[:, j * LANES:(j + 1) * LANES]
            ob_ref[r0:r0 + SGU_BLOCK, j * LANES:(j + 1) * LANES] = (
                gu[r0:r0 + SGU_BLOCK, j * LANES:(j + 1) * LANES] * sv).astype(BF16)

    c_g = c_u + 2 * SGU_WIDTH
    d_model = sga_ref.shape[1]
    for c0 in range(0, d_model, 512):
        sga_ref[:, c0:c0 + 512] = jax.nn.sigmoid(proj(c_g + c0, 512)).astype(BF16)
        sgb_ref[:, c0:c0 + 512] = jax.nn.sigmoid(proj(c_g + d_model + c0, 512)).astype(BF16)


def _inproj(x2, g_mix, w_in, sgu_w, sgu_bias, sgu_norm):
    t, d = x2.shape
    ncol = w_in.shape[1]
    tm = TM_PROJ
    const = lambda *shape: pl.BlockSpec(shape, lambda i: (0,) * len(shape))
    return pl.pallas_call(
        _inproj_kernel,
        grid=(t // tm,),
        in_specs=[
            pl.BlockSpec((tm, d), lambda i: (i, 0)),
            const(1, d),
            const(d, ncol),
            const(SGU_GROUPS, SGU_BLOCK, SGU_BLOCK),
            const(SGU_BLOCK, SGU_WIDTH),
            const(1, SGU_WIDTH),
        ],
        out_specs=[
            pl.BlockSpec((tm, 3 * SB_WIDTH), lambda i: (i, 0)),
            pl.BlockSpec((tm, SGU_WIDTH), lambda i: (i, 0)),
            pl.BlockSpec((tm, d), lambda i: (i, 0)),
            pl.BlockSpec((tm, d), lambda i: (i, 0)),
        ],
        out_shape=[
            jax.ShapeDtypeStruct((t, 3 * SB_WIDTH), BF16),
            jax.ShapeDtypeStruct((t, SGU_WIDTH), BF16),
            jax.ShapeDtypeStruct((t, d), BF16),
            jax.ShapeDtypeStruct((t, d), BF16),
        ],
        compiler_params=_cparams(("arbitrary",)),
        name="inproj_sgu",
    )(x2, g_mix, w_in, sgu_w, sgu_bias, sgu_norm)


def _attn_kernel(q_ref, k_ref, v_ref, ntri_ref, o_ref, carry_sc, acc_sc):
    tq = q_ref.shape[1]
    kc = KC_ATTN
    qi = pl.program_id(2)
    lane = lax.broadcasted_iota(jnp.int32, (tq, LANES), 1)
    head0 = lane < SB_HEAD_DIM
    zero = jnp.zeros((), BF16)
    q = q_ref[0]
    qh = (jnp.where(head0, q, zero), jnp.where(head0, zero, q))
    row = lax.broadcasted_iota(jnp.int32, (tq, kc), 0)
    col = lax.broadcasted_iota(jnp.int32, (tq, kc), 1)
    carry_sc[...] = jnp.zeros_like(carry_sc)
    acc_sc[...] = jnp.zeros_like(acc_sc)

    def step(j, key_off):
        k0 = pl.multiple_of(j * kc, kc)
        kj = k_ref[0, pl.ds(k0, kc), :]
        vj = v_ref[0, pl.ds(k0, kc), :]
        mask = None if key_off is None else (col + key_off) < row
        for h in range(2):
            z = lax.dot_general(qh[h], kj, (((1,), (1,)), ((), ())), preferred_element_type=F32)
            sp = jnp.maximum(z, 0.0) + jnp.log2(1.0 + jnp.exp2(-jnp.abs(z)))
            if mask is not None:
                sp = jnp.where(mask, sp, 0.0)
            cum = jnp.dot(sp.astype(BF16), ntri_ref[...], preferred_element_type=F32)
            carry = carry_sc[h]
            w = jnp.exp2(z + cum + jnp.concatenate([carry] * (kc // LANES), axis=1))
            carry_sc[h] = carry + jnp.broadcast_to(cum[:, 0:1], carry.shape)
            if mask is not None:
                w = jnp.where(mask, w, 0.0)
            acc_sc[h] += jnp.dot(w.astype(BF16), vj, preferred_element_type=F32)

    ndiag = tq // kc
    for c in reversed(range(ndiag)):
        step(qi * ndiag + c, c * kc)

    def live(t):
        return jnp.logical_and(t < qi * ndiag, jnp.max(carry_sc[...]) > UNDERFLOW_LOG2)

    def past(t):
        step(qi * ndiag - 1 - t, None)
        return t + 1

    lax.while_loop(live, past, 0)
    o_ref[0] = jnp.where(head0, acc_sc[0], acc_sc[1]).astype(o_ref.dtype)


def _attention(qkv3, ntri):
    b, s, _ = qkv3.shape
    tq = TQ_ATTN
    npair = SB_WIDTH // LANES
    return pl.pallas_call(
        _attn_kernel,
        grid=(b, npair, s // tq),
        in_specs=[
            pl.BlockSpec((1, tq, LANES), lambda bi, p, i: (bi, i, p)),
            pl.BlockSpec((1, s, LANES), lambda bi, p, i: (bi, 0, npair + p)),
            pl.BlockSpec((1, s, LANES), lambda bi, p, i: (bi, 0, 2 * npair + p)),
            pl.BlockSpec((KC_ATTN, KC_ATTN), lambda bi, p, i: (0, 0)),
        ],
        out_specs=pl.BlockSpec((1, tq, LANES), lambda bi, p, i: (bi, i, p)),
        out_shape=jax.ShapeDtypeStruct((b, s, SB_WIDTH), BF16),
        scratch_shapes=[pltpu.VMEM((2, tq, LANES), F32), pltpu.VMEM((2, tq, LANES), F32)],
        compiler_params=_cparams(("arbitrary", "arbitrary", "arbitrary")),
        name="stickbreak_attn",
    )(qkv3, qkv3, qkv3, ntri)


def _outproj_kernel(oa_ref, ob_ref, sga_ref, sgb_ref, x_ref, woa_ref, wob_ref, wo_ref,
                    g_ref, wr_ref, br_ref, upper_ref,
                    h_ref, c_ref, lslot_ref, gate_ref, cnt_ref):
    tm = x_ref.shape[0]
    ma = jnp.dot(oa_ref[...], woa_ref[...], preferred_element_type=F32)
    mb = jnp.dot(ob_ref[...], wob_ref[...], preferred_element_type=F32)
    merged = sga_ref[...].astype(F32) * ma + sgb_ref[...].astype(F32) * mb
    h = x_ref[...] + jnp.dot(merged.astype(BF16), wo_ref[...], preferred_element_type=F32)
    h_ref[...] = h
    c = _rms(h, g_ref[...])
    c_ref[...] = c.astype(BF16)

    logits = lax.dot_general(wr_ref[...], c, (((1,), (1,)), ((), ())),
                             precision=lax.Precision.HIGHEST,
                             preferred_element_type=F32) + br_ref[...]
    eid = lax.broadcasted_iota(jnp.int32, (N_EXPERTS, tm), 0).astype(F32)
    work = logits
    vals, sels = [], []
    for _ in range(TOP_K):
        m = jnp.max(work, axis=0, keepdims=True)
        ik = jnp.min(jnp.where(work == m, eid, float(N_EXPERTS)), axis=0, keepdims=True)
        sel = eid == ik
        vals.append(m)
        sels.append(sel)
        work = jnp.where(sel, -jnp.inf, work)
    es = [jnp.exp(v - vals[0]) for v in vals]
    inv = 1.0 / (es[0] + es[1] + es[2] + es[3])
    onehot = jnp.zeros((N_EXPERTS, tm), F32)
    for sel in sels:
        onehot = onehot + jnp.where(sel, 1.0, 0.0)

    prefix = jnp.dot(onehot.astype(BF16), upper_ref[...], preferred_element_type=F32)
    n = jnp.sum(onehot, axis=1, keepdims=True)
    n_al = jnp.ceil(n * (1.0 / ROW_ALIGN)) * ROW_ALIGN
    er = lax.broadcasted_iota(jnp.int32, (N_EXPERTS, N_EXPERTS), 0)
    ec = lax.broadcasted_iota(jnp.int32, (N_EXPERTS, N_EXPERTS), 1)
    run_off = jnp.dot(jnp.where(ec < er, 1.0, 0.0), jnp.broadcast_to(n_al, (N_EXPERTS, LANES)),
                      precision=lax.Precision.HIGHEST, preferred_element_type=F32)
    slot = prefix + run_off[:, 0:1]
    for k in range(TOP_K):
        gate_ref[k:k + 1, :] = es[k] * inv
        lslot_ref[k:k + 1, :] = jnp.sum(jnp.where(sels[k], slot, 0.0), axis=0,
                                        keepdims=True).astype(jnp.int32)
    cnt_ref[0] = jnp.broadcast_to(n, (N_EXPERTS, LANES))


def _outproj(o_a, o_b, sga, sgb, x2, w_out_a, w_out_b, w_out, g_moe, w_router_t, b_router, upper):
    t, d = x2.shape
    tm = TM_PROJ
    const = lambda *shape: pl.BlockSpec(shape, lambda i: (0,) * len(shape))
    row = lambda w: pl.BlockSpec((tm, w), lambda i: (i, 0))
    colk = pl.BlockSpec((TOP_K, tm), lambda i: (0, i))
    return pl.pallas_call(
        _outproj_kernel,
        grid=(t // tm,),
        in_specs=[row(SB_WIDTH), row(SGU_WIDTH), row(d), row(d), row(d),
                  const(SB_WIDTH, d), const(SGU_WIDTH, d), const(d, d),
                  const(1, d), const(N_EXPERTS, d), const(N_EXPERTS, 1), const(tm, tm)],
        out_specs=[row(d), row(d), colk, colk,
                   pl.BlockSpec((1, N_EXPERTS, LANES), lambda i: (i, 0, 0))],
        out_shape=[
            jax.ShapeDtypeStruct((t, d), F32),
            jax.ShapeDtypeStruct((t, d), BF16),
            jax.ShapeDtypeStruct((TOP_K, t), jnp.int32),
            jax.ShapeDtypeStruct((TOP_K, t), F32),
            jax.ShapeDtypeStruct((t // tm, N_EXPERTS, LANES), F32),
        ],
        compiler_params=_cparams(("arbitrary",)),
        name="outproj_router",
    )(o_a, o_b, sga, sgb, x2, w_out_a, w_out_b, w_out, g_moe, w_router_t, b_router, upper)


def _run_loops(nch_ref, w, fn):
    def per_expert(e, _):
        def per_piece(j, _):
            fn(e, j)
            return 0
        lax.fori_loop(0, nch_ref[w * N_EXPERTS + e], per_piece, 0)
        return 0
    lax.fori_loop(0, N_EXPERTS, per_expert, 0)


def _dispatch_kernel(base_ref, nch_ref, loff_ref, pad_end_ref, lslot_ref, c_ref, xs_ref,
                     perm_sc, loc_sc, zero_sc, sem, zsem):
    w = pl.program_id(0)
    tm = c_ref.shape[0]
    tb = zero_sc.shape[0]
    nloc = loc_sc.shape[1]

    @pl.when(w == 0)
    def _():
        zero_sc[...] = jnp.zeros_like(zero_sc)

        def pad_copy(e):
            return pltpu.make_async_copy(
                zero_sc, xs_ref.at[pl.ds(pl.multiple_of(pad_end_ref[e] - tb, tb), tb), :], zsem)

        def nonempty(e):
            prev = jnp.where(e > 0, pad_end_ref[jnp.maximum(e - 1, 0)], 0)
            return pad_end_ref[e] > prev

        def zstart(e, _):
            @pl.when(nonempty(e))
            def _():
                pad_copy(e).start()
            return 0

        def zwait(e, _):
            @pl.when(nonempty(e))
            def _():
                pad_copy(e).wait()
            return 0

        lax.fori_loop(0, N_EXPERTS, zstart, 0)
        lax.fori_loop(0, N_EXPERTS, zwait, 0)

    ls = [lslot_ref[k:k + 1, :] for k in range(TOP_K)]
    for r0 in range(0, nloc, SLOT_CHUNK):
        srow = lax.broadcasted_iota(jnp.int32, (SLOT_CHUNK, tm), 0) + r0
        sel = jnp.zeros((SLOT_CHUNK, tm), F32)
        for k in range(TOP_K):
            sel = jnp.where(srow == ls[k], 1.0, sel)
        perm_sc[r0:r0 + SLOT_CHUNK, :] = sel.astype(BF16)
    buf = w % 2
    loc_sc[buf] = jnp.dot(perm_sc[...], c_ref[...], preferred_element_type=F32)

    def piece_copy(win, e, j):
        s0 = pl.multiple_of(loff_ref[win * N_EXPERTS + e] + j * ROW_ALIGN, ROW_ALIGN)
        d0 = pl.multiple_of(base_ref[win * N_EXPERTS + e] + j * ROW_ALIGN, ROW_ALIGN)
        return pltpu.make_async_copy(loc_sc.at[win % 2, pl.ds(s0, ROW_ALIGN), :],
                                     xs_ref.at[pl.ds(d0, ROW_ALIGN), :], sem.at[win % 2])

    _run_loops(nch_ref, w, lambda e, j: piece_copy(w, e, j).start())

    @pl.when(w > 0)
    def _():
        _run_loops(nch_ref, w - 1, lambda e, j: piece_copy(w - 1, e, j).wait())

    @pl.when(w == pl.num_programs(0) - 1)
    def _():
        _run_loops(nch_ref, w, lambda e, j: piece_copy(w, e, j).wait())


def _dispatch(base, nch, loff, pad_end, lslot, c, n_slots):
    t, d = c.shape
    tm = TM_PROJ
    grid_spec = pltpu.PrefetchScalarGridSpec(
        num_scalar_prefetch=4,
        grid=(t // tm,),
        in_specs=[
            pl.BlockSpec((TOP_K, tm), lambda i, *_: (0, i)),
            pl.BlockSpec((tm, d), lambda i, *_: (i, 0)),
        ],
        out_specs=pl.BlockSpec(memory_space=pl.ANY),
        scratch_shapes=[pltpu.VMEM((LOCAL_SLOTS, tm), BF16),
                        pltpu.VMEM((2, LOCAL_SLOTS, d), F32),
                        pltpu.VMEM((TB_EXPERT, d), F32),
                        pltpu.SemaphoreType.DMA((2,)), pltpu.SemaphoreType.DMA(())],
    )
    return pl.pallas_call(
        _dispatch_kernel,
        grid_spec=grid_spec,
        out_shape=jax.ShapeDtypeStruct((n_slots, d), F32),
        compiler_params=_cparams(("arbitrary",)),
        name="moe_dispatch",
    )(base, nch, loff, pad_end, lslot, c)


def _expert_kernel(be_ref, nb_ref, xs_ref, w1_ref, w2_ref, b1g_ref, b1l_ref, b2_ref, perm_ref,
                   ys_ref, w1g_sc, w1l_sc, w2_sc):
    i = pl.program_id(0)
    active = i < nb_ref[0]
    fresh = jnp.logical_or(i == 0, be_ref[i] != be_ref[jnp.maximum(i - 1, 0)])

    @pl.when(jnp.logical_and(active, fresh))
    def _():
        half = MXU_DIM // 2
        for cb in range(w1_ref.shape[2] // MXU_DIM):
            blk = w1_ref[0, :, cb * MXU_DIM:(cb + 1) * MXU_DIM].astype(BF16)
            de = jnp.dot(blk, perm_ref[...], preferred_element_type=F32)
            w1g_sc[:, cb * half:(cb + 1) * half] = de[:, :half].astype(BF16)
            w1l_sc[:, cb * half:(cb + 1) * half] = de[:, half:].astype(BF16)
        w2_sc[...] = w2_ref[0].astype(BF16)

    @pl.when(active)
    def _():
        xb = xs_ref[...].astype(BF16)
        hg = jnp.dot(xb, w1g_sc[...], preferred_element_type=F32) + b1g_ref[0]
        hl = jnp.dot(xb, w1l_sc[...], preferred_element_type=F32) + b1l_ref[0]
        xg = jnp.minimum(hg, SWIGLU_LIMIT)
        xl = jnp.clip(hl, -SWIGLU_LIMIT, SWIGLU_LIMIT)
        act = xg * jax.nn.sigmoid(SWIGLU_ALPHA * xg) * (xl + 1.0)
        ys_ref[...] = jnp.dot(act.astype(BF16), w2_sc[...], preferred_element_type=F32) + b2_ref[0]


def _experts(block_e, n_used, xs, w1, w2, b1g, b1l, b2, perm):
    n_slots, d = xs.shape
    f2 = w1.shape[2]
    f = f2 // 2
    tb = TB_EXPERT
    blk = lambda i, be, nb: (jnp.minimum(i, nb[0] - 1), 0)
    wsel = lambda i, be, nb: (be[i], 0, 0)
    grid_spec = pltpu.PrefetchScalarGridSpec(
        num_scalar_prefetch=2,
        grid=(n_slots // tb,),
        in_specs=[
            pl.BlockSpec((tb, d), blk),
            pl.BlockSpec((1, d, f2), wsel),
            pl.BlockSpec((1, f, d), wsel),
            pl.BlockSpec((1, 1, f), wsel),
            pl.BlockSpec((1, 1, f), wsel),
            pl.BlockSpec((1, 1, d), wsel),
            pl.BlockSpec((MXU_DIM, MXU_DIM), lambda i, be, nb: (0, 0)),
        ],
        out_specs=pl.BlockSpec((tb, d), blk),
        scratch_shapes=[pltpu.VMEM((d, f), BF16), pltpu.VMEM((d, f), BF16),
                        pltpu.VMEM((f, d), BF16)],
    )
    return pl.pallas_call(
        _expert_kernel,
        grid_spec=grid_spec,
        out_shape=jax.ShapeDtypeStruct((n_slots, d), F32),
        compiler_params=_cparams(("arbitrary",)),
        name="moe_experts",
    )(block_e, n_used, xs, w1, w2, b1g, b1l, b2, perm)


def _combine_kernel(base_ref, nch_ref, loff_ref, lslot_ref, gate_ref, h_ref, p_ref, ys_ref,
                    gp_ref, wg_ref, wp_ref, gf_ref, out_ref, loc_sc, locb_sc, comb_sc, sem,
                    *, final):
    w = pl.program_id(0)
    tm = h_ref.shape[0]
    nloc = loc_sc.shape[0]

    def piece_copy(win, e, j):
        s0 = pl.multiple_of(loff_ref[win * N_EXPERTS + e] + j * ROW_ALIGN, ROW_ALIGN)
        d0 = pl.multiple_of(base_ref[win * N_EXPERTS + e] + j * ROW_ALIGN, ROW_ALIGN)
        return pltpu.make_async_copy(ys_ref.at[pl.ds(d0, ROW_ALIGN), :],
                                     loc_sc.at[pl.ds(s0, ROW_ALIGN), :], sem)

    @pl.when(w == 0)
    def _():
        loc_sc[...] = jnp.zeros_like(loc_sc)
        _run_loops(nch_ref, w, lambda e, j: piece_copy(w, e, j).start())

    _run_loops(nch_ref, w, lambda e, j: piece_copy(w, e, j).wait())
    for c0 in range(0, nloc, SLOT_CHUNK):
        locb_sc[c0:c0 + SLOT_CHUNK, :] = loc_sc[c0:c0 + SLOT_CHUNK, :].astype(BF16)

    @pl.when(w + 1 < pl.num_programs(0))
    def _():
        _run_loops(nch_ref, w + 1, lambda e, j: piece_copy(w + 1, e, j).start())

    lcol = [lslot_ref[:, k:k + 1] for k in range(TOP_K)]
    gcol = [gate_ref[:, k:k + 1] for k in range(TOP_K)]
    for c0 in range(0, nloc, SLOT_CHUNK):
        scol = lax.broadcasted_iota(jnp.int32, (tm, SLOT_CHUNK), 1) + c0
        g = jnp.zeros((tm, SLOT_CHUNK), F32)
        for k in range(TOP_K):
            g = jnp.where(scol == lcol[k], gcol[k], g)
        comb_sc[:, c0:c0 + SLOT_CHUNK] = g.astype(BF16)
    h = h_ref[...] + jnp.dot(comb_sc[...], locb_sc[...], preferred_element_type=F32)

    r = _rms(h, gp_ref[...]).astype(BF16)
    pg = jax.nn.sigmoid(jnp.dot(r, wg_ref[...], preferred_element_type=F32))
    pp = jnp.dot(p_ref[...].astype(BF16), wp_ref[...], preferred_element_type=F32)
    h = h + pg * pp
    out_ref[...] = _rms(h, gf_ref[...]) if final else h


def _combine(base, nch, loff, lslot_tk, gate_tk, h1, p2, ys, g_ple, w_ple_gate, w_ple_proj,
             g_final, final):
    t, d = h1.shape
    pd = p2.shape[1]
    tm = TM_PROJ
    const = lambda *shape: pl.BlockSpec(shape, lambda i, *_: (0,) * len(shape))
    row = lambda width: pl.BlockSpec((tm, width), lambda i, *_: (i, 0))
    grid_spec = pltpu.PrefetchScalarGridSpec(
        num_scalar_prefetch=3,
        grid=(t // tm,),
        in_specs=[row(TOP_K), row(TOP_K), row(d), row(pd),
                  pl.BlockSpec(memory_space=pl.ANY),
                  const(1, d), const(d, d), const(pd, d), const(1, d)],
        out_specs=row(d),
        scratch_shapes=[pltpu.VMEM((LOCAL_SLOTS, d), F32), pltpu.VMEM((LOCAL_SLOTS, d), BF16),
                        pltpu.VMEM((tm, LOCAL_SLOTS), BF16), pltpu.SemaphoreType.DMA(())],
    )
    return pl.pallas_call(
        functools.partial(_combine_kernel, final=final),
        grid_spec=grid_spec,
        out_shape=jax.ShapeDtypeStruct((t, d), F32),
        compiler_params=_cparams(("arbitrary",)),
        name="combine_ple",
    )(base, nch, loff, lslot_tk, gate_tk, h1, p2, ys, g_ple, w_ple_gate, w_ple_proj, g_final)


def kernel(x, p, g_mix, w_in, w_out_a, w_out_b, w_out, sgu_norm, sgu_w, sgu_b, g_moe,
           w_router, b_router, w1, b1, w2, b2, g_ple, w_ple_gate, w_ple_proj, g_final):
    b, s, d = x.shape
    depth = w_in.shape[0]
    t = b * s
    assert s % TQ_ATTN == 0 and t % TM_PROJ == 0

    kk = jnp.arange(KC_ATTN)
    ntri = -(kk[:, None] >= kk[None, :]).astype(BF16)
    tt = jnp.arange(TM_PROJ)
    upper = (tt[:, None] < tt[None, :]).astype(BF16)
    rr = jnp.arange(MXU_DIM)[:, None]
    cc = jnp.arange(MXU_DIM)[None, :]
    half = MXU_DIM // 2
    perm = (rr == jnp.where(cc < half, 2 * cc, 2 * (cc - half) + 1)).astype(BF16)

    tb = TB_EXPERT
    n_win = t // TM_PROJ
    n_blocks = -(-(t * TOP_K + n_win * N_EXPERTS * (ROW_ALIGN - 1) + N_EXPERTS * (tb - 1)) // tb)
    n_slots = n_blocks * tb

    h = x.reshape(t, d)
    for i in range(depth):
        sgu_bias = jnp.repeat(sgu_b[i].T, SGU_GROUP_DIM, axis=1)
        qkv, o_b, sga, sgb = _inproj(h, g_mix[i][None], w_in[i].astype(BF16), sgu_w[i],
                                     sgu_bias, sgu_norm[i][None])
        o_a = _attention(qkv.reshape(b, s, 3 * SB_WIDTH), ntri).reshape(t, SB_WIDTH)
        h1, c, lslot, gate, cnt = _outproj(
            o_a, o_b, sga, sgb, h, w_out_a[i].astype(BF16), w_out_b[i].astype(BF16),
            w_out[i].astype(BF16), g_moe[i][None], w_router[i].T, b_router[i][:, None], upper)

        counts = cnt[:, :, 0].astype(jnp.int32)
        run = ((counts + ROW_ALIGN - 1) // ROW_ALIGN) * ROW_ALIGN
        loff = jnp.cumsum(run, axis=1) - run
        region = ((jnp.sum(run, axis=0) + tb - 1) // tb) * tb
        pad_end = jnp.cumsum(region).astype(jnp.int32)
        base = (pad_end - region)[None, :] + jnp.cumsum(run, axis=0) - run
        nch = run // ROW_ALIGN
        starts = jnp.arange(n_blocks, dtype=jnp.int32) * tb
        block_e = jnp.minimum(jnp.sum(starts[:, None] >= pad_end[None, :], axis=1),
                              N_EXPERTS - 1).astype(jnp.int32)
        n_used = (pad_end[-1:] // tb).astype(jnp.int32)
        flat = lambda a: a.reshape(-1).astype(jnp.int32)

        xs = _dispatch(flat(base), flat(nch), flat(loff), pad_end, lslot, c, n_slots)
        ys = _experts(block_e, n_used, xs, w1[i], w2[i],
                      b1[i][:, None, 0::2], b1[i][:, None, 1::2], b2[i][:, None, :], perm)
        h = _combine(flat(base), flat(nch), flat(loff), lslot.T, gate.T, h1, p[i].reshape(t, -1),
                     ys, g_ple[i][None], w_ple_gate[i].astype(BF16), w_ple_proj[i].astype(BF16),
                     g_final[None], final=(i == depth - 1))
    return h.reshape(b, s, d)
```

```python
import functools
import math

import jax
import jax.numpy as jnp
from jax import lax
from jax.experimental import pallas as pl
from jax.experimental.pallas import tpu as pltpu

F32 = jnp.float32
BF16 = jnp.bfloat16

EPS = 1e-6
CHUNK = 64
SB_HEADS = 8
SB_HEAD_DIM = 64
SB_WIDTH = SB_HEADS * SB_HEAD_DIM
SGU_GROUPS = 8
SGU_WIDTH = 512
SGU_GROUP_DIM = SGU_WIDTH // SGU_GROUPS
SGU_BLOCK = 128
N_EXPERTS = 32
TOP_K = 4
SWIGLU_ALPHA = 1.702
SWIGLU_LIMIT = 7.0
LOG2E = 1.4426950408889634
UNDERFLOW_LOG2 = -160.0

LANES = 128
MXU_DIM = 256
VMEM_LIMIT_BYTES = 56 * 1024 * 1024

TM_PROJ = 512
TQ_ATTN = 512
KC_ATTN = MXU_DIM
TB_EXPERT = 512
ROW_ALIGN = 8
PIECE_ROWS = (32, ROW_ALIGN)
SLOT_CHUNK = 256
LOCAL_SLOTS = -(-(TM_PROJ * TOP_K + N_EXPERTS * (ROW_ALIGN - 1)) // SLOT_CHUNK) * SLOT_CHUNK
MAX_PIECES = max(LOCAL_SLOTS // PIECE_ROWS[0],
                 N_EXPERTS * (PIECE_ROWS[0] // PIECE_ROWS[1] - 1))


def _cparams(sem):
    return pltpu.CompilerParams(dimension_semantics=sem,
                                vmem_limit_bytes=VMEM_LIMIT_BYTES)


def _rms(x, g):
    ms = jnp.mean(x * x, axis=-1, keepdims=True)
    return x * lax.rsqrt(ms + EPS) * g


def _inproj_kernel(x_ref, g_ref, w_ref, sw_ref, sbias_ref, sn_ref,
                   qkv_ref, ob_ref, sga_ref, sgb_ref):
    tm = x_ref.shape[0]
    a = _rms(x_ref[...], g_ref[...]).astype(BF16)

    def proj(c0, width):
        return jnp.dot(a, w_ref[:, c0:c0 + width], preferred_element_type=F32)

    qkv_ref[:, 0:SB_WIDTH] = (proj(0, SB_WIDTH) * (LOG2E / math.sqrt(SB_HEAD_DIM))).astype(BF16)
    qkv_ref[:, SB_WIDTH:2 * SB_WIDTH] = proj(SB_WIDTH, SB_WIDTH).astype(BF16)
    qkv_ref[:, 2 * SB_WIDTH:3 * SB_WIDTH] = proj(2 * SB_WIDTH, SB_WIDTH).astype(BF16)

    c_u = 3 * SB_WIDTH
    gu = jax.nn.gelu(proj(c_u, SGU_WIDTH))
    gv = jax.nn.gelu(proj(c_u + SGU_WIDTH, SGU_WIDTH))
    mu = jnp.mean(gv, axis=-1, keepdims=True)
    d = gv - mu
    var = jnp.mean(d * d, axis=-1, keepdims=True)
    vn = (d * lax.rsqrt(var + EPS) * sn_ref[...]).astype(BF16)

    rr = lax.broadcasted_iota(jnp.int32, (SGU_BLOCK, SGU_BLOCK), 0) // CHUNK
    cc = lax.broadcasted_iota(jnp.int32, (SGU_BLOCK, SGU_BLOCK), 1) // CHUNK
    cmask = cc <= rr
    lane = lax.broadcasted_iota(jnp.int32, (SGU_BLOCK, LANES), 1)
    lo_mask = lane < SGU_GROUP_DIM
    wcat = []
    for j in range(SGU_GROUPS // 2):
        w0 = jnp.where(cmask, sw_ref[2 * j], 0.0).astype(BF16)
        w1 = jnp.where(cmask, sw_ref[2 * j + 1], 0.0).astype(BF16)
        wcat.append(jnp.concatenate([w0, w1], axis=1))

    zero = jnp.zeros((), BF16)
    for blk in range(tm // SGU_BLOCK):
        r0 = blk * SGU_BLOCK
        for j in range(SGU_GROUPS // 2):
            vp = vn[r0:r0 + SGU_BLOCK, j * LANES:(j + 1) * LANES]
            rhs = jnp.concatenate([jnp.where(lo_mask, vp, zero),
                                   jnp.where(lo_mask, zero, vp)], axis=0)
            sv = jnp.dot(wcat[j], rhs, preferred_element_type=F32)
            sv = sv + sbias_ref[:, j * LANES:(j + 1) * LANES]
            ob_ref[r0:r0 + SGU_BLOCK, j * LANES:(j + 1) * LANES] = (
                gu[r0:r0 + SGU_BLOCK, j * LANES:(j + 1) * LANES] * sv).astype(BF16)

    c_g = c_u + 2 * SGU_WIDTH
    d_model = sga_ref.shape[1]
    for c0 in range(0, d_model, 512):
        sga_ref[:, c0:c0 + 512] = jax.nn.sigmoid(proj(c_g + c0, 512)).astype(BF16)
        sgb_ref[:, c0:c0 + 512] = jax.nn.sigmoid(proj(c_g + d_model + c0, 512)).astype(BF16)


def _inproj(x2, g_mix, w_in, sgu_w, sgu_bias, sgu_norm):
    t, d = x2.shape
    ncol = w_in.shape[1]
    tm = TM_PROJ
    const = lambda *shape: pl.BlockSpec(shape, lambda i: (0,) * len(shape))
    return pl.pallas_call(
        _inproj_kernel,
        grid=(t // tm,),
        in_specs=[
            pl.BlockSpec((tm, d), lambda i: (i, 0)),
            const(1, d),
            const(d, ncol),
            const(SGU_GROUPS, SGU_BLOCK, SGU_BLOCK),
            const(SGU_BLOCK, SGU_WIDTH),
            const(1, SGU_WIDTH),
        ],
        out_specs=[
            pl.BlockSpec((tm, 3 * SB_WIDTH), lambda i: (i, 0)),
            pl.BlockSpec((tm, SGU_WIDTH), lambda i: (i, 0)),
            pl.BlockSpec((tm, d), lambda i: (i, 0)),
            pl.BlockSpec((tm, d), lambda i: (i, 0)),
        ],
        out_shape=[
            jax.ShapeDtypeStruct((t, 3 * SB_WIDTH), BF16),
            jax.ShapeDtypeStruct((t, SGU_WIDTH), BF16),
            jax.ShapeDtypeStruct((t, d), BF16),
            jax.ShapeDtypeStruct((t, d), BF16),
        ],
        compiler_params=_cparams(("arbitrary",)),
        name="inproj_sgu",
    )(x2, g_mix, w_in, sgu_w, sgu_bias, sgu_norm)


def _attn_kernel(q_ref, k_ref, v_ref, ntri_ref, o_ref, carry_sc, acc_sc):
    tq = q_ref.shape[1]
    kc = KC_ATTN
    qi = pl.program_id(2)
    lane = lax.broadcasted_iota(jnp.int32, (tq, LANES), 1)
    head0 = lane < SB_HEAD_DIM
    zero = jnp.zeros((), BF16)
    q = q_ref[0]
    qh = (jnp.where(head0, q, zero), jnp.where(head0, zero, q))
    row = lax.broadcasted_iota(jnp.int32, (tq, kc), 0)
    col = lax.broadcasted_iota(jnp.int32, (tq, kc), 1)
    carry_sc[...] = jnp.zeros_like(carry_sc)
    acc_sc[...] = jnp.zeros_like(acc_sc)

    def step(j, key_off):
        k0 = pl.multiple_of(j * kc, kc)
        kj = k_ref[0, pl.ds(k0, kc), :]
        vj = v_ref[0, pl.ds(k0, kc), :]
        mask = None if key_off is None else (col + key_off) < row
        for h in range(2):
            z = lax.dot_general(qh[h], kj, (((1,), (1,)), ((), ())), preferred_element_type=F32)
            sp = jnp.maximum(z, 0.0) + jnp.log2(1.0 + jnp.exp2(-jnp.abs(z)))
            if mask is not None:
                sp = jnp.where(mask, sp, 0.0)
            cum = jnp.dot(sp.astype(BF16), ntri_ref[...], preferred_element_type=F32)
            carry = carry_sc[h]
            w = jnp.exp2(z + cum + jnp.concatenate([carry] * (kc // LANES), axis=1))
            carry_sc[h] = carry + jnp.broadcast_to(cum[:, 0:1], carry.shape)
            if mask is not None:
                w = jnp.where(mask, w, 0.0)
            acc_sc[h] += jnp.dot(w.astype(BF16), vj, preferred_element_type=F32)

    ndiag = tq // kc
    for c in reversed(range(ndiag)):
        step(qi * ndiag + c, c * kc)

    def live(t):
        return jnp.logical_and(t < qi * ndiag, jnp.max(carry_sc[...]) > UNDERFLOW_LOG2)

    def past(t):
        step(qi * ndiag - 1 - t, None)
        return t + 1

    lax.while_loop(live, past, 0)
    o_ref[0] = jnp.where(head0, acc_sc[0], acc_sc[1]).astype(o_ref.dtype)


def _attention(qkv3, ntri):
    b, s, _ = qkv3.shape
    tq = TQ_ATTN
    npair = SB_WIDTH // LANES
    return pl.pallas_call(
        _attn_kernel,
        grid=(b, npair, s // tq),
        in_specs=[
            pl.BlockSpec((1, tq, LANES), lambda bi, p, i: (bi, i, p)),
            pl.BlockSpec((1, s, LANES), lambda bi, p, i: (bi, 0, npair + p)),
            pl.BlockSpec((1, s, LANES), lambda bi, p, i: (bi, 0, 2 * npair + p)),
            pl.BlockSpec((KC_ATTN, KC_ATTN), lambda bi, p, i: (0, 0)),
        ],
        out_specs=pl.BlockSpec((1, tq, LANES), lambda bi, p, i: (bi, i, p)),
        out_shape=jax.ShapeDtypeStruct((b, s, SB_WIDTH), BF16),
        scratch_shapes=[pltpu.VMEM((2, tq, LANES), F32), pltpu.VMEM((2, tq, LANES), F32)],
        compiler_params=_cparams(("arbitrary", "arbitrary", "arbitrary")),
        name="stickbreak_attn",
    )(qkv3, qkv3, qkv3, ntri)


def _outproj_kernel(oa_ref, ob_ref, sga_ref, sgb_ref, x_ref, woa_ref, wob_ref, wo_ref,
                    g_ref, wr_ref, br_ref, upper_ref,
                    h_ref, c_ref, lslot_ref, gate_ref, cnt_ref):
    tm = x_ref.shape[0]
    ma = jnp.dot(oa_ref[...], woa_ref[...], preferred_element_type=F32)
    mb = jnp.dot(ob_ref[...], wob_ref[...], preferred_element_type=F32)
    merged = sga_ref[...].astype(F32) * ma + sgb_ref[...].astype(F32) * mb
    h = x_ref[...] + jnp.dot(merged.astype(BF16), wo_ref[...], preferred_element_type=F32)
    h_ref[...] = h
    c = _rms(h, g_ref[...])
    c_ref[...] = c.astype(BF16)

    logits = lax.dot_general(wr_ref[...], c, (((1,), (1,)), ((), ())),
                             precision=lax.Precision.HIGHEST,
                             preferred_element_type=F32) + br_ref[...]
    eid = lax.broadcasted_iota(jnp.int32, (N_EXPERTS, tm), 0).astype(F32)
    work = logits
    vals, sels = [], []
    for _ in range(TOP_K):
        m = jnp.max(work, axis=0, keepdims=True)
        ik = jnp.min(jnp.where(work == m, eid, float(N_EXPERTS)), axis=0, keepdims=True)
        sel = eid == ik
        vals.append(m)
        sels.append(sel)
        work = jnp.where(sel, -jnp.inf, work)
    es = [jnp.exp(v - vals[0]) for v in vals]
    inv = 1.0 / (es[0] + es[1] + es[2] + es[3])
    onehot = jnp.zeros((N_EXPERTS, tm), F32)
    for sel in sels:
        onehot = onehot + jnp.where(sel, 1.0, 0.0)

    prefix = jnp.dot(onehot.astype(BF16), upper_ref[...], preferred_element_type=F32)
    n = jnp.sum(onehot, axis=1, keepdims=True)
    n_al = jnp.ceil(n * (1.0 / ROW_ALIGN)) * ROW_ALIGN
    er = lax.broadcasted_iota(jnp.int32, (N_EXPERTS, N_EXPERTS), 0)
    ec = lax.broadcasted_iota(jnp.int32, (N_EXPERTS, N_EXPERTS), 1)
    run_off = jnp.dot(jnp.where(ec < er, 1.0, 0.0), jnp.broadcast_to(n_al, (N_EXPERTS, LANES)),
                      precision=lax.Precision.HIGHEST, preferred_element_type=F32)
    slot = prefix + run_off[:, 0:1]
    for k in range(TOP_K):
        gate_ref[k:k + 1, :] = es[k] * inv
        lslot_ref[k:k + 1, :] = jnp.sum(jnp.where(sels[k], slot, 0.0), axis=0,
                                        keepdims=True).astype(jnp.int32)
    cnt_ref[0] = jnp.broadcast_to(n, (N_EXPERTS, LANES))


def _outproj(o_a, o_b, sga, sgb, x2, w_out_a, w_out_b, w_out, g_moe, w_router_t, b_router, upper):
    t, d = x2.shape
    tm = TM_PROJ
    const = lambda *shape: pl.BlockSpec(shape, lambda i: (0,) * len(shape))
    row = lambda w: pl.BlockSpec((tm, w), lambda i: (i, 0))
    colk = pl.BlockSpec((TOP_K, tm), lambda i: (0, i))
    return pl.pallas_call(
        _outproj_kernel,
        grid=(t // tm,),
        in_specs=[row(SB_WIDTH), row(SGU_WIDTH), row(d), row(d), row(d),
                  const(SB_WIDTH, d), const(SGU_WIDTH, d), const(d, d),
                  const(1, d), const(N_EXPERTS, d), const(N_EXPERTS, 1), const(tm, tm)],
        out_specs=[row(d), row(d), colk, colk,
                   pl.BlockSpec((1, N_EXPERTS, LANES), lambda i: (i, 0, 0))],
        out_shape=[
            jax.ShapeDtypeStruct((t, d), F32),
            jax.ShapeDtypeStruct((t, d), BF16),
            jax.ShapeDtypeStruct((TOP_K, t), jnp.int32),
            jax.ShapeDtypeStruct((TOP_K, t), F32),
            jax.ShapeDtypeStruct((t // tm, N_EXPERTS, LANES), F32),
        ],
        compiler_params=_cparams(("arbitrary",)),
        name="outproj_router",
    )(o_a, o_b, sga, sgb, x2, w_out_a, w_out_b, w_out, g_moe, w_router_t, b_router, upper)


def _piece_loops(tables, win, fn):
    n_ref, local_ref, global_ref = tables
    for cls, rows in enumerate(PIECE_ROWS):
        head = win * len(PIECE_ROWS) + cls

        def body(i, _, off=head * MAX_PIECES, rows=rows):
            fn(pl.multiple_of(local_ref[off + i], ROW_ALIGN),
               pl.multiple_of(global_ref[off + i], ROW_ALIGN), rows)
            return 0

        lax.fori_loop(0, n_ref[head], body, 0)


def _piece_tables(run, loff, base):
    big = PIECE_ROWS[0]
    nbig = run // big
    nsmall = (run - nbig * big) // ROW_ALIGN
    i = jnp.arange(MAX_PIECES, dtype=jnp.int32)
    experts = jnp.arange(N_EXPERTS, dtype=jnp.int32)

    def flat(cnt, rows, skip):
        cum = jnp.cumsum(cnt, axis=1)
        owner = jnp.minimum(jnp.sum(i[None, :, None] >= cum[:, None, :], axis=-1), N_EXPERTS - 1)
        onehot = owner[:, :, None] == experts
        pick = lambda tab: jnp.sum(jnp.where(onehot, tab[:, None, :], 0), axis=-1)
        j = i[None, :] - pick(cum - cnt)
        return cum[:, -1], pick(loff + skip) + j * rows, pick(base + skip) + j * rows

    nb, lb, gb = flat(nbig, big, 0)
    ns, lsm, gsm = flat(nsmall, ROW_ALIGN, nbig * big)
    pack = lambda a, b: jnp.stack([a, b], axis=1).reshape(-1).astype(jnp.int32)
    return pack(nb, ns), pack(lb, lsm), pack(gb, gsm)


def _dispatch_kernel(pn_ref, pl_ref, pg_ref, pad_end_ref, lslot_ref, c_ref, xs_ref,
                     perm_sc, loc_sc, zero_sc, sem, zsem):
    w = pl.program_id(0)
    tm = c_ref.shape[0]
    tb = zero_sc.shape[0]
    nloc = loc_sc.shape[1]

    @pl.when(w == 0)
    def _():
        zero_sc[...] = jnp.zeros_like(zero_sc)

        def pad_copy(e):
            return pltpu.make_async_copy(
                zero_sc, xs_ref.at[pl.ds(pl.multiple_of(pad_end_ref[e] - tb, tb), tb), :], zsem)

        def nonempty(e):
            prev = jnp.where(e > 0, pad_end_ref[jnp.maximum(e - 1, 0)], 0)
            return pad_end_ref[e] > prev

        def zstart(e, _):
            @pl.when(nonempty(e))
            def _():
                pad_copy(e).start()
            return 0

        def zwait(e, _):
            @pl.when(nonempty(e))
            def _():
                pad_copy(e).wait()
            return 0

        lax.fori_loop(0, N_EXPERTS, zstart, 0)
        lax.fori_loop(0, N_EXPERTS, zwait, 0)

    ls = [lslot_ref[k:k + 1, :] for k in range(TOP_K)]
    for r0 in range(0, nloc, SLOT_CHUNK):
        srow = lax.broadcasted_iota(jnp.int32, (SLOT_CHUNK, tm), 0) + r0
        sel = jnp.zeros((SLOT_CHUNK, tm), F32)
        for k in range(TOP_K):
            sel = jnp.where(srow == ls[k], 1.0, sel)
        perm_sc[r0:r0 + SLOT_CHUNK, :] = sel.astype(BF16)
    buf = lax.bitwise_and(w, 1)
    loc_sc[buf] = jnp.dot(perm_sc[...], c_ref[...], preferred_element_type=F32)

    tables = (pn_ref, pl_ref, pg_ref)

    def piece_copy(slot):
        def build(local_row, global_row, rows):
            return pltpu.make_async_copy(loc_sc.at[slot, pl.ds(local_row, rows), :],
                                         xs_ref.at[pl.ds(global_row, rows), :], sem.at[slot])
        return build

    _piece_loops(tables, w, lambda *a: piece_copy(buf)(*a).start())

    @pl.when(w > 0)
    def _():
        _piece_loops(tables, w - 1, lambda *a: piece_copy(1 - buf)(*a).wait())

    @pl.when(w == pl.num_programs(0) - 1)
    def _():
        _piece_loops(tables, w, lambda *a: piece_copy(buf)(*a).wait())


def _dispatch(pieces, pad_end, lslot, c, n_slots):
    t, d = c.shape
    tm = TM_PROJ
    grid_spec = pltpu.PrefetchScalarGridSpec(
        num_scalar_prefetch=4,
        grid=(t // tm,),
        in_specs=[
            pl.BlockSpec((TOP_K, tm), lambda i, *_: (0, i)),
            pl.BlockSpec((tm, d), lambda i, *_: (i, 0)),
        ],
        out_specs=pl.BlockSpec(memory_space=pl.ANY),
        scratch_shapes=[pltpu.VMEM((LOCAL_SLOTS, tm), BF16),
                        pltpu.VMEM((2, LOCAL_SLOTS, d), F32),
                        pltpu.VMEM((TB_EXPERT, d), F32),
                        pltpu.SemaphoreType.DMA((2,)), pltpu.SemaphoreType.DMA(())],
    )
    return pl.pallas_call(
        _dispatch_kernel,
        grid_spec=grid_spec,
        out_shape=jax.ShapeDtypeStruct((n_slots, d), F32),
        compiler_params=_cparams(("arbitrary",)),
        name="moe_dispatch",
    )(*pieces, pad_end, lslot, c)


def _expert_kernel(be_ref, nb_ref, xs_ref, w1_ref, w2_ref, b1g_ref, b1l_ref, b2_ref, perm_ref,
                   ys_ref, w1g_sc, w1l_sc, w2_sc):
    i = pl.program_id(0)
    active = i < nb_ref[0]
    fresh = jnp.logical_or(i == 0, be_ref[i] != be_ref[jnp.maximum(i - 1, 0)])

    @pl.when(jnp.logical_and(active, fresh))
    def _():
        half = MXU_DIM // 2
        for cb in range(w1_ref.shape[2] // MXU_DIM):
            blk = w1_ref[0, :, cb * MXU_DIM:(cb + 1) * MXU_DIM].astype(BF16)
            de = jnp.dot(blk, perm_ref[...], preferred_element_type=F32)
            w1g_sc[:, cb * half:(cb + 1) * half] = de[:, :half].astype(BF16)
            w1l_sc[:, cb * half:(cb + 1) * half] = de[:, half:].astype(BF16)
        w2_sc[...] = w2_ref[0].astype(BF16)

    @pl.when(active)
    def _():
        xb = xs_ref[...].astype(BF16)
        hg = jnp.dot(xb, w1g_sc[...], preferred_element_type=F32) + b1g_ref[0]
        hl = jnp.dot(xb, w1l_sc[...], preferred_element_type=F32) + b1l_ref[0]
        xg = jnp.minimum(hg, SWIGLU_LIMIT)
        xl = jnp.clip(hl, -SWIGLU_LIMIT, SWIGLU_LIMIT)
        act = xg * jax.nn.sigmoid(SWIGLU_ALPHA * xg) * (xl + 1.0)
        ys_ref[...] = jnp.dot(act.astype(BF16), w2_sc[...], preferred_element_type=F32) + b2_ref[0]


def _experts(block_e, n_used, xs, w1, w2, b1g, b1l, b2, perm):
    n_slots, d = xs.shape
    f2 = w1.shape[2]
    f = f2 // 2
    tb = TB_EXPERT
    blk = lambda i, be, nb: (jnp.minimum(i, nb[0] - 1), 0)
    wsel = lambda i, be, nb: (be[i], 0, 0)
    grid_spec = pltpu.PrefetchScalarGridSpec(
        num_scalar_prefetch=2,
        grid=(n_slots // tb,),
        in_specs=[
            pl.BlockSpec((tb, d), blk),
            pl.BlockSpec((1, d, f2), wsel),
            pl.BlockSpec((1, f, d), wsel),
            pl.BlockSpec((1, 1, f), wsel),
            pl.BlockSpec((1, 1, f), wsel),
            pl.BlockSpec((1, 1, d), wsel),
            pl.BlockSpec((MXU_DIM, MXU_DIM), lambda i, be, nb: (0, 0)),
        ],
        out_specs=pl.BlockSpec((tb, d), blk),
        scratch_shapes=[pltpu.VMEM((d, f), BF16), pltpu.VMEM((d, f), BF16),
                        pltpu.VMEM((f, d), BF16)],
    )
    return pl.pallas_call(
        _expert_kernel,
        grid_spec=grid_spec,
        out_shape=jax.ShapeDtypeStruct((n_slots, d), F32),
        compiler_params=_cparams(("arbitrary",)),
        name="moe_experts",
    )(block_e, n_used, xs, w1, w2, b1g, b1l, b2, perm)


def _combine_kernel(pn_ref, pl_ref, pg_ref, lslot_ref, gate_ref, h_ref, p_ref, ys_ref,
                    gp_ref, wg_ref, wp_ref, gf_ref, out_ref, loc_sc, locb_sc, comb_sc, sem,
                    *, final):
    w = pl.program_id(0)
    tm = h_ref.shape[0]
    nloc = loc_sc.shape[0]

    tables = (pn_ref, pl_ref, pg_ref)

    def piece_copy(local_row, global_row, rows):
        return pltpu.make_async_copy(ys_ref.at[pl.ds(global_row, rows), :],
                                     loc_sc.at[pl.ds(local_row, rows), :], sem)

    @pl.when(w == 0)
    def _():
        loc_sc[...] = jnp.zeros_like(loc_sc)
        _piece_loops(tables, w, lambda *a: piece_copy(*a).start())

    _piece_loops(tables, w, lambda *a: piece_copy(*a).wait())
    for c0 in range(0, nloc, SLOT_CHUNK):
        locb_sc[c0:c0 + SLOT_CHUNK, :] = loc_sc[c0:c0 + SLOT_CHUNK, :].astype(BF16)

    @pl.when(w + 1 < pl.num_programs(0))
    def _():
        _piece_loops(tables, w + 1, lambda *a: piece_copy(*a).start())

    lcol = [lslot_ref[:, k:k + 1] for k in range(TOP_K)]
    gcol = [gate_ref[:, k:k + 1] for k in range(TOP_K)]
    for c0 in range(0, nloc, SLOT_CHUNK):
        scol = lax.broadcasted_iota(jnp.int32, (tm, SLOT_CHUNK), 1) + c0
        g = jnp.zeros((tm, SLOT_CHUNK), F32)
        for k in range(TOP_K):
            g = jnp.where(scol == lcol[k], gcol[k], g)
        comb_sc[:, c0:c0 + SLOT_CHUNK] = g.astype(BF16)
    h = h_ref[...] + jnp.dot(comb_sc[...], locb_sc[...], preferred_element_type=F32)

    r = _rms(h, gp_ref[...]).astype(BF16)
    pg = jax.nn.sigmoid(jnp.dot(r, wg_ref[...], preferred_element_type=F32))
    pp = jnp.dot(p_ref[...].astype(BF16), wp_ref[...], preferred_element_type=F32)
    h = h + pg * pp
    out_ref[...] = _rms(h, gf_ref[...]) if final else h


def _combine(pieces, lslot_tk, gate_tk, h1, p2, ys, g_ple, w_ple_gate, w_ple_proj, g_final,
             final):
    t, d = h1.shape
    pd = p2.shape[1]
    tm = TM_PROJ
    const = lambda *shape: pl.BlockSpec(shape, lambda i, *_: (0,) * len(shape))
    row = lambda width: pl.BlockSpec((tm, width), lambda i, *_: (i, 0))
    grid_spec = pltpu.PrefetchScalarGridSpec(
        num_scalar_prefetch=3,
        grid=(t // tm,),
        in_specs=[row(TOP_K), row(TOP_K), row(d), row(pd),
                  pl.BlockSpec(memory_space=pl.ANY),
                  const(1, d), const(d, d), const(pd, d), const(1, d)],
        out_specs=row(d),
        scratch_shapes=[pltpu.VMEM((LOCAL_SLOTS, d), F32), pltpu.VMEM((LOCAL_SLOTS, d), BF16),
                        pltpu.VMEM((tm, LOCAL_SLOTS), BF16), pltpu.SemaphoreType.DMA(())],
    )
    return pl.pallas_call(
        functools.partial(_combine_kernel, final=final),
        grid_spec=grid_spec,
        out_shape=jax.ShapeDtypeStruct((t, d), F32),
        compiler_params=_cparams(("arbitrary",)),
        name="combine_ple",
    )(*pieces, lslot_tk, gate_tk, h1, p2, ys, g_ple, w_ple_gate, w_ple_proj, g_final)


def kernel(x, p, g_mix, w_in, w_out_a, w_out_b, w_out, sgu_norm, sgu_w, sgu_b, g_moe,
           w_router, b_router, w1, b1, w2, b2, g_ple, w_ple_gate, w_ple_proj, g_final):
    b, s, d = x.shape
    depth = w_in.shape[0]
    t = b * s
    assert s % TQ_ATTN == 0 and t % TM_PROJ == 0

    kk = jnp.arange(KC_ATTN)
    ntri = -(kk[:, None] >= kk[None, :]).astype(BF16)
    tt = jnp.arange(TM_PROJ)
    upper = (tt[:, None] < tt[None, :]).astype(BF16)
    rr = jnp.arange(MXU_DIM)[:, None]
    cc = jnp.arange(MXU_DIM)[None, :]
    half = MXU_DIM // 2
    perm = (rr == jnp.where(cc < half, 2 * cc, 2 * (cc - half) + 1)).astype(BF16)

    tb = TB_EXPERT
    n_win = t // TM_PROJ
    n_blocks = -(-(t * TOP_K + n_win * N_EXPERTS * (ROW_ALIGN - 1) + N_EXPERTS * (tb - 1)) // tb)
    n_slots = n_blocks * tb

    h = x.reshape(t, d)
    for i in range(depth):
        sgu_bias = jnp.repeat(sgu_b[i].T, SGU_GROUP_DIM, axis=1)
        qkv, o_b, sga, sgb = _inproj(h, g_mix[i][None], w_in[i].astype(BF16), sgu_w[i],
                                     sgu_bias, sgu_norm[i][None])
        o_a = _attention(qkv.reshape(b, s, 3 * SB_WIDTH), ntri).reshape(t, SB_WIDTH)
        h1, c, lslot, gate, cnt = _outproj(
            o_a, o_b, sga, sgb, h, w_out_a[i].astype(BF16), w_out_b[i].astype(BF16),
            w_out[i].astype(BF16), g_moe[i][None], w_router[i].T, b_router[i][:, None], upper)

        counts = cnt[:, :, 0].astype(jnp.int32)
        run = ((counts + ROW_ALIGN - 1) // ROW_ALIGN) * ROW_ALIGN
        loff = jnp.cumsum(run, axis=1) - run
        region = ((jnp.sum(run, axis=0) + tb - 1) // tb) * tb
        pad_end = jnp.cumsum(region).astype(jnp.int32)
        base = (pad_end - region)[None, :] + jnp.cumsum(run, axis=0) - run
        pieces = _piece_tables(run, loff, base)
        starts = jnp.arange(n_blocks, dtype=jnp.int32) * tb
        block_e = jnp.minimum(jnp.sum(starts[:, None] >= pad_end[None, :], axis=1),
                              N_EXPERTS - 1).astype(jnp.int32)
        n_used = (pad_end[-1:] // tb).astype(jnp.int32)

        xs = _dispatch(pieces, pad_end, lslot, c, n_slots)
        ys = _experts(block_e, n_used, xs, w1[i], w2[i],
                      b1[i][:, None, 0::2], b1[i][:, None, 1::2], b2[i][:, None, :], perm)
        h = _combine(pieces, lslot.T, gate.T, h1, p[i].reshape(t, -1), ys, g_ple[i][None],
                     w_ple_gate[i].astype(BF16), w_ple_proj[i].astype(BF16), g_final[None],
                     final=(i == depth - 1))
    return h.reshape(b, s, d)
```

```python
import functools
import math

import jax
import jax.numpy as jnp
from jax import lax
from jax.experimental import pallas as pl
from jax.experimental.pallas import tpu as pltpu

F32 = jnp.float32
BF16 = jnp.bfloat16

EPS = 1e-6
CHUNK = 64
SB_HEADS = 8
SB_HEAD_DIM = 64
SB_WIDTH = SB_HEADS * SB_HEAD_DIM
SGU_GROUPS = 8
SGU_WIDTH = 512
SGU_GROUP_DIM = SGU_WIDTH // SGU_GROUPS
SGU_BLOCK = 128
N_EXPERTS = 32
TOP_K = 4
SWIGLU_ALPHA = 1.702
SWIGLU_LIMIT = 7.0
LOG2E = 1.4426950408889634
UNDERFLOW_LOG2 = -160.0

LANES = 128
MXU_DIM = 256
VMEM_LIMIT_BYTES = 56 * 1024 * 1024

TM_PROJ = 512
TQ_ATTN = 512
KC_ATTN = MXU_DIM
TB_EXPERT = 512
ROW_ALIGN = 8
PIECE_ROWS = (32, ROW_ALIGN)
SLOT_CHUNK = 256
LOCAL_SLOTS = -(-(TM_PROJ * TOP_K + N_EXPERTS * (ROW_ALIGN - 1)) // SLOT_CHUNK) * SLOT_CHUNK
MAX_PIECES = max(LOCAL_SLOTS // PIECE_ROWS[0],
                 N_EXPERTS * (PIECE_ROWS[0] // PIECE_ROWS[1] - 1))


def _cparams(sem):
    return pltpu.CompilerParams(dimension_semantics=sem,
                                vmem_limit_bytes=VMEM_LIMIT_BYTES)


def _rms(x, g):
    ms = jnp.mean(x * x, axis=-1, keepdims=True)
    return x * lax.rsqrt(ms + EPS) * g


def _inproj_kernel(x_ref, g_ref, w_ref, sw_ref, sbias_ref, sn_ref,
                   qkv_ref, ob_ref, sga_ref, sgb_ref):
    tm = x_ref.shape[0]
    a = _rms(x_ref[...], g_ref[...]).astype(BF16)

    def proj(c0, width):
        return jnp.dot(a, w_ref[:, c0:c0 + width], preferred_element_type=F32)

    qkv_ref[:, 0:SB_WIDTH] = (proj(0, SB_WIDTH) * (LOG2E / math.sqrt(SB_HEAD_DIM))).astype(BF16)
    qkv_ref[:, SB_WIDTH:2 * SB_WIDTH] = proj(SB_WIDTH, SB_WIDTH).astype(BF16)
    qkv_ref[:, 2 * SB_WIDTH:3 * SB_WIDTH] = proj(2 * SB_WIDTH, SB_WIDTH).astype(BF16)

    c_u = 3 * SB_WIDTH
    gu = jax.nn.gelu(proj(c_u, SGU_WIDTH))
    gv = jax.nn.gelu(proj(c_u + SGU_WIDTH, SGU_WIDTH))
    mu = jnp.mean(gv, axis=-1, keepdims=True)
    d = gv - mu
    var = jnp.mean(d * d, axis=-1, keepdims=True)
    vn = (d * lax.rsqrt(var + EPS) * sn_ref[...]).astype(BF16)

    rr = lax.broadcasted_iota(jnp.int32, (SGU_BLOCK, SGU_BLOCK), 0) // CHUNK
    cc = lax.broadcasted_iota(jnp.int32, (SGU_BLOCK, SGU_BLOCK), 1) // CHUNK
    cmask = cc <= rr
    lane = lax.broadcasted_iota(jnp.int32, (SGU_BLOCK, LANES), 1)
    lo_mask = lane < SGU_GROUP_DIM
    wcat = []
    for j in range(SGU_GROUPS // 2):
        w0 = jnp.where(cmask, sw_ref[2 * j], 0.0).astype(BF16)
        w1 = jnp.where(cmask, sw_ref[2 * j + 1], 0.0).astype(BF16)
        wcat.append(jnp.concatenate([w0, w1], axis=1))

    zero = jnp.zeros((), BF16)
    for blk in range(tm // SGU_BLOCK):
        r0 = blk * SGU_BLOCK
        for j in range(SGU_GROUPS // 2):
            vp = vn[r0:r0 + SGU_BLOCK, j * LANES:(j + 1) * LANES]
            rhs = jnp.concatenate([jnp.where(lo_mask, vp, zero),
                                   jnp.where(lo_mask, zero, vp)], axis=0)
            sv = jnp.dot(wcat[j], rhs, preferred_element_type=F32)
            sv = sv + sbias_ref[:, j * LANES:(j + 1) * LANES]
            ob_ref[r0:r0 + SGU_BLOCK, j * LANES:(j + 1) * LANES] = (
                gu[r0:r0 + SGU_BLOCK, j * LANES:(j + 1) * LANES] * sv).astype(BF16)

    c_g = c_u + 2 * SGU_WIDTH
    d_model = sga_ref.shape[1]
    for c0 in range(0, d_model, 512):
        sga_ref[:, c0:c0 + 512] = jax.nn.sigmoid(proj(c_g + c0, 512)).astype(BF16)
        sgb_ref[:, c0:c0 + 512] = jax.nn.sigmoid(proj(c_g + d_model + c0, 512)).astype(BF16)


def _inproj(x2, g_mix, w_in, sgu_w, sgu_bias, sgu_norm):
    t, d = x2.shape
    ncol = w_in.shape[1]
    tm = TM_PROJ
    const = lambda *shape: pl.BlockSpec(shape, lambda i: (0,) * len(shape))
    return pl.pallas_call(
        _inproj_kernel,
        grid=(t // tm,),
        in_specs=[
            pl.BlockSpec((tm, d), lambda i: (i, 0)),
            const(1, d),
            const(d, ncol),
            const(SGU_GROUPS, SGU_BLOCK, SGU_BLOCK),
            const(SGU_BLOCK, SGU_WIDTH),
            const(1, SGU_WIDTH),
        ],
        out_specs=[
            pl.BlockSpec((tm, 3 * SB_WIDTH), lambda i: (i, 0)),
            pl.BlockSpec((tm, SGU_WIDTH), lambda i: (i, 0)),
            pl.BlockSpec((tm, d), lambda i: (i, 0)),
            pl.BlockSpec((tm, d), lambda i: (i, 0)),
        ],
        out_shape=[
            jax.ShapeDtypeStruct((t, 3 * SB_WIDTH), BF16),
            jax.ShapeDtypeStruct((t, SGU_WIDTH), BF16),
            jax.ShapeDtypeStruct((t, d), BF16),
            jax.ShapeDtypeStruct((t, d), BF16),
        ],
        compiler_params=_cparams(("arbitrary",)),
        name="inproj_sgu",
    )(x2, g_mix, w_in, sgu_w, sgu_bias, sgu_norm)


def _attn_kernel(q_ref, k_ref, v_ref, ntri_ref, o_ref, carry_sc, acc_sc):
    tq = q_ref.shape[1]
    kc = KC_ATTN
    qi = pl.program_id(2)
    lane = lax.broadcasted_iota(jnp.int32, (tq, LANES), 1)
    head0 = lane < SB_HEAD_DIM
    zero = jnp.zeros((), BF16)
    q = q_ref[0]
    qh = (jnp.where(head0, q, zero), jnp.where(head0, zero, q))
    row = lax.broadcasted_iota(jnp.int32, (tq, kc), 0)
    col = lax.broadcasted_iota(jnp.int32, (tq, kc), 1)
    carry_sc[...] = jnp.zeros_like(carry_sc)
    acc_sc[...] = jnp.zeros_like(acc_sc)

    def scores(j, key_off, keep):
        k0 = pl.multiple_of(j * kc, kc)
        kj = k_ref[0, pl.ds(k0, kc), :]
        mask = None if key_off is None else (col + key_off) < row
        out = []
        for h in range(2):
            z = lax.dot_general(qh[h], kj, (((1,), (1,)), ((), ())), preferred_element_type=F32)
            sp = jnp.maximum(z, 0.0) + jnp.log2(1.0 + jnp.exp2(-jnp.abs(z)))
            if mask is not None:
                sp = jnp.where(mask, sp, 0.0)
            if keep is not None:
                sp = jnp.where(keep, sp, 0.0)
            out.append((z, jnp.dot(sp.astype(BF16), ntri_ref[...], preferred_element_type=F32)))
        return k0, mask, keep, out

    def weigh(chunk):
        k0, mask, keep, out = chunk
        vj = v_ref[0, pl.ds(k0, kc), :]
        for h in range(2):
            z, cum = out[h]
            carry = carry_sc[h]
            w = jnp.exp2(z + cum + jnp.concatenate([carry] * (kc // LANES), axis=1))
            carry_sc[h] = carry + jnp.broadcast_to(cum[:, 0:1], carry.shape)
            if mask is not None:
                w = jnp.where(mask, w, 0.0)
            if keep is not None:
                w = jnp.where(keep, w, 0.0)
            acc_sc[h] += jnp.dot(w.astype(BF16), vj, preferred_element_type=F32)

    ndiag = tq // kc
    npast = qi * ndiag
    head = [scores(npast + c, c * kc, None) for c in reversed(range(ndiag))]
    head.append(scores(jnp.maximum(npast - 1, 0), None, npast > 0))
    for chunk in head:
        weigh(chunk)

    def live(t):
        return jnp.logical_and(t < npast, jnp.max(carry_sc[...]) > UNDERFLOW_LOG2)

    def past(t):
        weigh(scores(npast - 1 - t, None, None))
        return t + 1

    lax.while_loop(live, past, 1)
    o_ref[0] = jnp.where(head0, acc_sc[0], acc_sc[1]).astype(o_ref.dtype)


def _attention(qkv3, ntri):
    b, s, _ = qkv3.shape
    tq = TQ_ATTN
    npair = SB_WIDTH // LANES
    return pl.pallas_call(
        _attn_kernel,
        grid=(b, npair, s // tq),
        in_specs=[
            pl.BlockSpec((1, tq, LANES), lambda bi, p, i: (bi, i, p)),
            pl.BlockSpec((1, s, LANES), lambda bi, p, i: (bi, 0, npair + p)),
            pl.BlockSpec((1, s, LANES), lambda bi, p, i: (bi, 0, 2 * npair + p)),
            pl.BlockSpec((KC_ATTN, KC_ATTN), lambda bi, p, i: (0, 0)),
        ],
        out_specs=pl.BlockSpec((1, tq, LANES), lambda bi, p, i: (bi, i, p)),
        out_shape=jax.ShapeDtypeStruct((b, s, SB_WIDTH), BF16),
        scratch_shapes=[pltpu.VMEM((2, tq, LANES), F32), pltpu.VMEM((2, tq, LANES), F32)],
        compiler_params=_cparams(("arbitrary", "arbitrary", "arbitrary")),
        name="stickbreak_attn",
    )(qkv3, qkv3, qkv3, ntri)


def _outproj_kernel(oa_ref, ob_ref, sga_ref, sgb_ref, x_ref, woa_ref, wob_ref, wo_ref,
                    g_ref, wr_ref, br_ref, upper_ref,
                    h_ref, c_ref, lslot_ref, gate_ref, cnt_ref):
    tm = x_ref.shape[0]

    ma = jnp.dot(oa_ref[...], woa_ref[...], preferred_element_type=F32)
    mb = jnp.dot(ob_ref[...], wob_ref[...], preferred_element_type=F32)
    merged = sga_ref[...].astype(F32) * ma + sgb_ref[...].astype(F32) * mb
    h = x_ref[...] + jnp.dot(merged.astype(BF16), wo_ref[...], preferred_element_type=F32)
    h_ref[...] = h
    c = _rms(h, g_ref[...])
    c_ref[...] = c.astype(BF16)

    logits = lax.dot_general(wr_ref[...], c, (((1,), (1,)), ((), ())),
                             precision=lax.Precision.HIGHEST,
                             preferred_element_type=F32) + br_ref[...]
    eid = lax.broadcasted_iota(jnp.int32, (N_EXPERTS, tm), 0).astype(F32)
    work = logits
    vals, sels = [], []
    for _ in range(TOP_K):
        m = jnp.max(work, axis=0, keepdims=True)
        ik = jnp.min(jnp.where(work == m, eid, float(N_EXPERTS)), axis=0, keepdims=True)
        sel = eid == ik
        vals.append(m)
        sels.append(sel)
        work = jnp.where(sel, -jnp.inf, work)
    es = [jnp.exp(v - vals[0]) for v in vals]
    inv = 1.0 / (es[0] + es[1] + es[2] + es[3])
    onehot = jnp.zeros((N_EXPERTS, tm), F32)
    for sel in sels:
        onehot = onehot + jnp.where(sel, 1.0, 0.0)

    prefix = jnp.dot(onehot.astype(BF16), upper_ref[...], preferred_element_type=F32)
    n = jnp.sum(onehot, axis=1, keepdims=True)
    n_al = jnp.ceil(n * (1.0 / ROW_ALIGN)) * ROW_ALIGN
    er = lax.broadcasted_iota(jnp.int32, (N_EXPERTS, N_EXPERTS), 0)
    ec = lax.broadcasted_iota(jnp.int32, (N_EXPERTS, N_EXPERTS), 1)
    run_off = jnp.dot(jnp.where(ec < er, 1.0, 0.0), jnp.broadcast_to(n_al, (N_EXPERTS, LANES)),
                      precision=lax.Precision.HIGHEST, preferred_element_type=F32)
    slot = prefix + run_off[:, 0:1]
    for k in range(TOP_K):
        gate_ref[k:k + 1, :] = es[k] * inv
        lslot_ref[k:k + 1, :] = jnp.sum(jnp.where(sels[k], slot, 0.0), axis=0,
                                        keepdims=True).astype(jnp.int32)
    cnt_ref[0] = jnp.broadcast_to(n, (N_EXPERTS, LANES))


def _outproj(o_a, o_b, sga, sgb, x2, w_out_a, w_out_b, w_out, g_moe, w_router_t, b_router, upper):
    t, d = x2.shape
    tm = TM_PROJ
    const = lambda *shape: pl.BlockSpec(shape, lambda i: (0,) * len(shape))
    row = lambda w: pl.BlockSpec((tm, w), lambda i: (i, 0))
    colk = pl.BlockSpec((TOP_K, tm), lambda i: (0, i))
    return pl.pallas_call(
        _outproj_kernel,
        grid=(t // tm,),
        in_specs=[row(SB_WIDTH), row(SGU_WIDTH), row(d), row(d), row(d),
                  const(SB_WIDTH, d), const(SGU_WIDTH, d), const(d, d),
                  const(1, d), const(N_EXPERTS, d), const(N_EXPERTS, 1), const(tm, tm)],
        out_specs=[row(d), row(d), colk, colk,
                   pl.BlockSpec((1, N_EXPERTS, LANES), lambda i: (i, 0, 0))],
        out_shape=[
            jax.ShapeDtypeStruct((t, d), F32),
            jax.ShapeDtypeStruct((t, d), BF16),
            jax.ShapeDtypeStruct((TOP_K, t), jnp.int32),
            jax.ShapeDtypeStruct((TOP_K, t), F32),
            jax.ShapeDtypeStruct((t // tm, N_EXPERTS, LANES), F32),
        ],
        compiler_params=_cparams(("arbitrary",)),
        name="outproj_router",
    )(o_a, o_b, sga, sgb, x2, w_out_a, w_out_b, w_out, g_moe, w_router_t, b_router, upper)


def _piece_loops(tables, win, fn):
    n_ref, local_ref, global_ref = tables
    for cls, rows in enumerate(PIECE_ROWS):
        head = win * len(PIECE_ROWS) + cls

        def body(i, _, off=head * MAX_PIECES, rows=rows):
            fn(pl.multiple_of(local_ref[off + i], ROW_ALIGN),
               pl.multiple_of(global_ref[off + i], ROW_ALIGN), rows)
            return 0

        lax.fori_loop(0, n_ref[head], body, 0)


def _piece_tables(run, loff, base):
    big = PIECE_ROWS[0]
    nbig = run // big
    nsmall = (run - nbig * big) // ROW_ALIGN
    i = jnp.arange(MAX_PIECES, dtype=jnp.int32)
    experts = jnp.arange(N_EXPERTS, dtype=jnp.int32)

    def flat(cnt, rows, skip):
        cum = jnp.cumsum(cnt, axis=1)
        owner = jnp.minimum(jnp.sum(i[None, :, None] >= cum[:, None, :], axis=-1), N_EXPERTS - 1)
        onehot = owner[:, :, None] == experts
        pick = lambda tab: jnp.sum(jnp.where(onehot, tab[:, None, :], 0), axis=-1)
        j = i[None, :] - pick(cum - cnt)
        return cum[:, -1], pick(loff + skip) + j * rows, pick(base + skip) + j * rows

    nb, lb, gb = flat(nbig, big, 0)
    ns, lsm, gsm = flat(nsmall, ROW_ALIGN, nbig * big)
    pack = lambda a, b: jnp.stack([a, b], axis=1).reshape(-1).astype(jnp.int32)
    return pack(nb, ns), pack(lb, lsm), pack(gb, gsm)


def _dispatch_kernel(pn_ref, pl_ref, pg_ref, pad_end_ref, lslot_ref, c_ref, xs_ref,
                     perm_sc, loc_sc, zero_sc, sem, zsem):
    w = pl.program_id(0)
    tm = c_ref.shape[0]
    tb = zero_sc.shape[0]
    nloc = loc_sc.shape[1]

    @pl.when(w == 0)
    def _():
        zero_sc[...] = jnp.zeros_like(zero_sc)

        def pad_copy(e):
            return pltpu.make_async_copy(
                zero_sc, xs_ref.at[pl.ds(pl.multiple_of(pad_end_ref[e] - tb, tb), tb), :], zsem)

        def nonempty(e):
            prev = jnp.where(e > 0, pad_end_ref[jnp.maximum(e - 1, 0)], 0)
            return pad_end_ref[e] > prev

        def zstart(e, _):
            @pl.when(nonempty(e))
            def _():
                pad_copy(e).start()
            return 0

        def zwait(e, _):
            @pl.when(nonempty(e))
            def _():
                pad_copy(e).wait()
            return 0

        lax.fori_loop(0, N_EXPERTS, zstart, 0)
        lax.fori_loop(0, N_EXPERTS, zwait, 0)

    ls = [lslot_ref[k:k + 1, :] for k in range(TOP_K)]
    for r0 in range(0, nloc, SLOT_CHUNK):
        srow = lax.broadcasted_iota(jnp.int32, (SLOT_CHUNK, tm), 0) + r0
        sel = jnp.zeros((SLOT_CHUNK, tm), F32)
        for k in range(TOP_K):
            sel = jnp.where(srow == ls[k], 1.0, sel)
        perm_sc[r0:r0 + SLOT_CHUNK, :] = sel.astype(BF16)
    buf = lax.bitwise_and(w, 1)
    loc_sc[buf] = jnp.dot(perm_sc[...], c_ref[...], preferred_element_type=F32)

    tables = (pn_ref, pl_ref, pg_ref)

    def piece_copy(slot):
        def build(local_row, global_row, rows):
            return pltpu.make_async_copy(loc_sc.at[slot, pl.ds(local_row, rows), :],
                                         xs_ref.at[pl.ds(global_row, rows), :], sem.at[slot])
        return build

    _piece_loops(tables, w, lambda *a: piece_copy(buf)(*a).start())

    @pl.when(w > 0)
    def _():
        _piece_loops(tables, w - 1, lambda *a: piece_copy(1 - buf)(*a).wait())

    @pl.when(w == pl.num_programs(0) - 1)
    def _():
        _piece_loops(tables, w, lambda *a: piece_copy(buf)(*a).wait())


def _dispatch(pieces, pad_end, lslot, c, n_slots):
    t, d = c.shape
    tm = TM_PROJ
    grid_spec = pltpu.PrefetchScalarGridSpec(
        num_scalar_prefetch=4,
        grid=(t // tm,),
        in_specs=[
            pl.BlockSpec((TOP_K, tm), lambda i, *_: (0, i)),
            pl.BlockSpec((tm, d), lambda i, *_: (i, 0)),
        ],
        out_specs=pl.BlockSpec(memory_space=pl.ANY),
        scratch_shapes=[pltpu.VMEM((LOCAL_SLOTS, tm), BF16),
                        pltpu.VMEM((2, LOCAL_SLOTS, d), F32),
                        pltpu.VMEM((TB_EXPERT, d), F32),
                        pltpu.SemaphoreType.DMA((2,)), pltpu.SemaphoreType.DMA(())],
    )
    return pl.pallas_call(
        _dispatch_kernel,
        grid_spec=grid_spec,
        out_shape=jax.ShapeDtypeStruct((n_slots, d), F32),
        compiler_params=_cparams(("arbitrary",)),
        name="moe_dispatch",
    )(*pieces, pad_end, lslot, c)


def _expert_kernel(be_ref, nb_ref, xs_ref, w1_ref, w2_ref, b1g_ref, b1l_ref, b2_ref, perm_ref,
                   ys_ref, w1g_sc, w1l_sc, w2_sc):
    i = pl.program_id(0)
    active = i < nb_ref[0]
    fresh = jnp.logical_or(i == 0, be_ref[i] != be_ref[jnp.maximum(i - 1, 0)])

    @pl.when(jnp.logical_and(active, fresh))
    def _():
        half = MXU_DIM // 2
        for cb in range(w1_ref.shape[2] // MXU_DIM):
            blk = w1_ref[0, :, cb * MXU_DIM:(cb + 1) * MXU_DIM].astype(BF16)
            de = jnp.dot(blk, perm_ref[...], preferred_element_type=F32)
            w1g_sc[:, cb * half:(cb + 1) * half] = de[:, :half].astype(BF16)
            w1l_sc[:, cb * half:(cb + 1) * half] = de[:, half:].astype(BF16)
        w2_sc[...] = w2_ref[0].astype(BF16)

    @pl.when(active)
    def _():
        xb = xs_ref[...].astype(BF16)
        hg = jnp.dot(xb, w1g_sc[...], preferred_element_type=F32) + b1g_ref[0]
        hl = jnp.dot(xb, w1l_sc[...], preferred_element_type=F32) + b1l_ref[0]
        xg = jnp.minimum(hg, SWIGLU_LIMIT)
        xl = jnp.clip(hl, -SWIGLU_LIMIT, SWIGLU_LIMIT)
        act = xg * jax.nn.sigmoid(SWIGLU_ALPHA * xg) * (xl + 1.0)
        ys_ref[...] = jnp.dot(act.astype(BF16), w2_sc[...], preferred_element_type=F32) + b2_ref[0]


def _experts(block_e, n_used, xs, w1, w2, b1g, b1l, b2, perm):
    n_slots, d = xs.shape
    f2 = w1.shape[2]
    f = f2 // 2
    tb = TB_EXPERT
    blk = lambda i, be, nb: (jnp.minimum(i, nb[0] - 1), 0)
    wsel = lambda i, be, nb: (be[i], 0, 0)
    grid_spec = pltpu.PrefetchScalarGridSpec(
        num_scalar_prefetch=2,
        grid=(n_slots // tb,),
        in_specs=[
            pl.BlockSpec((tb, d), blk),
            pl.BlockSpec((1, d, f2), wsel),
            pl.BlockSpec((1, f, d), wsel),
            pl.BlockSpec((1, 1, f), wsel),
            pl.BlockSpec((1, 1, f), wsel),
            pl.BlockSpec((1, 1, d), wsel),
            pl.BlockSpec((MXU_DIM, MXU_DIM), lambda i, be, nb: (0, 0)),
        ],
        out_specs=pl.BlockSpec((tb, d), blk),
        scratch_shapes=[pltpu.VMEM((d, f), BF16), pltpu.VMEM((d, f), BF16),
                        pltpu.VMEM((f, d), BF16)],
    )
    return pl.pallas_call(
        _expert_kernel,
        grid_spec=grid_spec,
        out_shape=jax.ShapeDtypeStruct((n_slots, d), F32),
        compiler_params=_cparams(("arbitrary",)),
        name="moe_experts",
    )(block_e, n_used, xs, w1, w2, b1g, b1l, b2, perm)


def _combine_kernel(pn_ref, pl_ref, pg_ref, lslot_ref, gate_ref, h_ref, p_ref, ys_ref,
                    gp_ref, wg_ref, wp_ref, gf_ref, out_ref, loc_sc, locb_sc, comb_sc, sem,
                    *, final):
    w = pl.program_id(0)
    tm = h_ref.shape[0]
    nloc = loc_sc.shape[0]

    tables = (pn_ref, pl_ref, pg_ref)

    def piece_copy(local_row, global_row, rows):
        return pltpu.make_async_copy(ys_ref.at[pl.ds(global_row, rows), :],
                                     loc_sc.at[pl.ds(local_row, rows), :], sem)

    @pl.when(w == 0)
    def _():
        loc_sc[...] = jnp.zeros_like(loc_sc)
        _piece_loops(tables, w, lambda *a: piece_copy(*a).start())

    _piece_loops(tables, w, lambda *a: piece_copy(*a).wait())
    for c0 in range(0, nloc, SLOT_CHUNK):
        locb_sc[c0:c0 + SLOT_CHUNK, :] = loc_sc[c0:c0 + SLOT_CHUNK, :].astype(BF16)

    @pl.when(w + 1 < pl.num_programs(0))
    def _():
        _piece_loops(tables, w + 1, lambda *a: piece_copy(*a).start())

    lcol = [lslot_ref[:, k:k + 1] for k in range(TOP_K)]
    gcol = [gate_ref[:, k:k + 1] for k in range(TOP_K)]
    for c0 in range(0, nloc, SLOT_CHUNK):
        scol = lax.broadcasted_iota(jnp.int32, (tm, SLOT_CHUNK), 1) + c0
        g = jnp.zeros((tm, SLOT_CHUNK), F32)
        for k in range(TOP_K):
            g = jnp.where(scol == lcol[k], gcol[k], g)
        comb_sc[:, c0:c0 + SLOT_CHUNK] = g.astype(BF16)
    h = h_ref[...] + jnp.dot(comb_sc[...], locb_sc[...], preferred_element_type=F32)

    r = _rms(h, gp_ref[...]).astype(BF16)
    pg = jax.nn.sigmoid(jnp.dot(r, wg_ref[...], preferred_element_type=F32))
    pp = jnp.dot(p_ref[...].astype(BF16), wp_ref[...], preferred_element_type=F32)
    h = h + pg * pp
    out_ref[...] = _rms(h, gf_ref[...]) if final else h


def _combine(pieces, lslot_tk, gate_tk, h1, p2, ys, g_ple, w_ple_gate, w_ple_proj, g_final,
             final):
    t, d = h1.shape
    pd = p2.shape[1]
    tm = TM_PROJ
    const = lambda *shape: pl.BlockSpec(shape, lambda i, *_: (0,) * len(shape))
    row = lambda width: pl.BlockSpec((tm, width), lambda i, *_: (i, 0))
    grid_spec = pltpu.PrefetchScalarGridSpec(
        num_scalar_prefetch=3,
        grid=(t // tm,),
        in_specs=[row(TOP_K), row(TOP_K), row(d), row(pd),
                  pl.BlockSpec(memory_space=pl.ANY),
                  const(1, d), const(d, d), const(pd, d), const(1, d)],
        out_specs=row(d),
        scratch_shapes=[pltpu.VMEM((LOCAL_SLOTS, d), F32), pltpu.VMEM((LOCAL_SLOTS, d), BF16),
                        pltpu.VMEM((tm, LOCAL_SLOTS), BF16), pltpu.SemaphoreType.DMA(())],
    )
    return pl.pallas_call(
        functools.partial(_combine_kernel, final=final),
        grid_spec=grid_spec,
        out_shape=jax.ShapeDtypeStruct((t, d), F32),
        compiler_params=_cparams(("arbitrary",)),
        name="combine_ple",
    )(*pieces, lslot_tk, gate_tk, h1, p2, ys, g_ple, w_ple_gate, w_ple_proj, g_final)


def kernel(x, p, g_mix, w_in, w_out_a, w_out_b, w_out, sgu_norm, sgu_w, sgu_b, g_moe,
           w_router, b_router, w1, b1, w2, b2, g_ple, w_ple_gate, w_ple_proj, g_final):
    b, s, d = x.shape
    depth = w_in.shape[0]
    t = b * s
    assert s % TQ_ATTN == 0 and t % TM_PROJ == 0

    kk = jnp.arange(KC_ATTN)
    ntri = -(kk[:, None] >= kk[None, :]).astype(BF16)
    tt = jnp.arange(TM_PROJ)
    upper = (tt[:, None] < tt[None, :]).astype(BF16)
    rr = jnp.arange(MXU_DIM)[:, None]
    cc = jnp.arange(MXU_DIM)[None, :]
    half = MXU_DIM // 2
    perm = (rr == jnp.where(cc < half, 2 * cc, 2 * (cc - half) + 1)).astype(BF16)

    tb = TB_EXPERT
    n_win = t // TM_PROJ
    n_blocks = -(-(t * TOP_K + n_win * N_EXPERTS * (ROW_ALIGN - 1) + N_EXPERTS * (tb - 1)) // tb)
    n_slots = n_blocks * tb

    h = x.reshape(t, d)
    for i in range(depth):
        sgu_bias = jnp.repeat(sgu_b[i].T, SGU_GROUP_DIM, axis=1)
        qkv, o_b, sga, sgb = _inproj(h, g_mix[i][None], w_in[i].astype(BF16), sgu_w[i],
                                     sgu_bias, sgu_norm[i][None])
        o_a = _attention(qkv.reshape(b, s, 3 * SB_WIDTH), ntri).reshape(t, SB_WIDTH)
        h1, c, lslot, gate, cnt = _outproj(
            o_a, o_b, sga, sgb, h, w_out_a[i].astype(BF16), w_out_b[i].astype(BF16),
            w_out[i].astype(BF16), g_moe[i][None], w_router[i].T, b_router[i][:, None], upper)

        counts = cnt[:, :, 0].astype(jnp.int32)
        run = ((counts + ROW_ALIGN - 1) // ROW_ALIGN) * ROW_ALIGN
        loff = jnp.cumsum(run, axis=1) - run
        region = ((jnp.sum(run, axis=0) + tb - 1) // tb) * tb
        pad_end = jnp.cumsum(region).astype(jnp.int32)
        base = (pad_end - region)[None, :] + jnp.cumsum(run, axis=0) - run
        pieces = _piece_tables(run, loff, base)
        starts = jnp.arange(n_blocks, dtype=jnp.int32) * tb
        block_e = jnp.minimum(jnp.sum(starts[:, None] >= pad_end[None, :], axis=1),
                              N_EXPERTS - 1).astype(jnp.int32)
        n_used = (pad_end[-1:] // tb).astype(jnp.int32)

        xs = _dispatch(pieces, pad_end, lslot, c, n_slots)
        ys = _experts(block_e, n_used, xs, w1[i], w2[i],
                      b1[i][:, None, 0::2], b1[i][:, None, 1::2], b2[i][:, None, :], perm)
        h = _combine(pieces, lslot.T, gate.T, h1, p[i].reshape(t, -1), ys, g_ple[i][None],
                     w_ple_gate[i].astype(BF16), w_ple_proj[i].astype(BF16), g_final[None],
                     final=(i == depth - 1))
    return h.reshape(b, s, d)
```

```python
import functools
import math

import jax
import jax.numpy as jnp
from jax import lax
from jax.experimental import pallas as pl
from jax.experimental.pallas import tpu as pltpu

F32 = jnp.float32
BF16 = jnp.bfloat16

EPS = 1e-6
CHUNK = 64
SB_HEADS = 8
SB_HEAD_DIM = 64
SB_WIDTH = SB_HEADS * SB_HEAD_DIM
SGU_GROUPS = 8
SGU_WIDTH = 512
SGU_GROUP_DIM = SGU_WIDTH // SGU_GROUPS
SGU_BLOCK = 128
N_EXPERTS = 32
TOP_K = 4
SWIGLU_ALPHA = 1.702
SWIGLU_LIMIT = 7.0
LOG2E = 1.4426950408889634
UNDERFLOW_LOG2 = -160.0

LANES = 128
MXU_DIM = 256
VMEM_LIMIT_BYTES = 56 * 1024 * 1024

TM_PROJ = 512
TQ_ATTN = 512
ATTN_TILES_PER_STEP = 2
KC_ATTN = MXU_DIM
TB_EXPERT = 512
ROW_ALIGN = 8
PIECE_ROWS = (32, ROW_ALIGN)
SLOT_CHUNK = 256
LOCAL_SLOTS = -(-(TM_PROJ * TOP_K + N_EXPERTS * (ROW_ALIGN - 1)) // SLOT_CHUNK) * SLOT_CHUNK
MAX_PIECES = max(LOCAL_SLOTS // PIECE_ROWS[0],
                 N_EXPERTS * (PIECE_ROWS[0] // PIECE_ROWS[1] - 1))


def _cparams(sem):
    return pltpu.CompilerParams(dimension_semantics=sem,
                                vmem_limit_bytes=VMEM_LIMIT_BYTES)


def _rms(x, g):
    ms = jnp.mean(x * x, axis=-1, keepdims=True)
    return x * lax.rsqrt(ms + EPS) * g


def _inproj_kernel(x_ref, g_ref, w_ref, sw_ref, sbias_ref, sn_ref,
                   qkv_ref, ob_ref, sga_ref, sgb_ref):
    tm = x_ref.shape[0]
    a = _rms(x_ref[...], g_ref[...]).astype(BF16)

    def proj(c0, width):
        return jnp.dot(a, w_ref[:, c0:c0 + width], preferred_element_type=F32)

    qkv_ref[:, 0:SB_WIDTH] = (proj(0, SB_WIDTH) * (LOG2E / math.sqrt(SB_HEAD_DIM))).astype(BF16)
    qkv_ref[:, SB_WIDTH:2 * SB_WIDTH] = proj(SB_WIDTH, SB_WIDTH).astype(BF16)
    qkv_ref[:, 2 * SB_WIDTH:3 * SB_WIDTH] = proj(2 * SB_WIDTH, SB_WIDTH).astype(BF16)

    c_u = 3 * SB_WIDTH
    gu = jax.nn.gelu(proj(c_u, SGU_WIDTH))
    gv = jax.nn.gelu(proj(c_u + SGU_WIDTH, SGU_WIDTH))
    mu = jnp.mean(gv, axis=-1, keepdims=True)
    d = gv - mu
    var = jnp.mean(d * d, axis=-1, keepdims=True)
    vn = (d * lax.rsqrt(var + EPS) * sn_ref[...]).astype(BF16)

    rr = lax.broadcasted_iota(jnp.int32, (SGU_BLOCK, SGU_BLOCK), 0) // CHUNK
    cc = lax.broadcasted_iota(jnp.int32, (SGU_BLOCK, SGU_BLOCK), 1) // CHUNK
    cmask = cc <= rr
    lane = lax.broadcasted_iota(jnp.int32, (SGU_BLOCK, LANES), 1)
    lo_mask = lane < SGU_GROUP_DIM
    wcat = []
    for j in range(SGU_GROUPS // 2):
        w0 = jnp.where(cmask, sw_ref[2 * j], 0.0).astype(BF16)
        w1 = jnp.where(cmask, sw_ref[2 * j + 1], 0.0).astype(BF16)
        wcat.append(jnp.concatenate([w0, w1], axis=1))

    zero = jnp.zeros((), BF16)
    for blk in range(tm // SGU_BLOCK):
        r0 = blk * SGU_BLOCK
        for j in range(SGU_GROUPS // 2):
            vp = vn[r0:r0 + SGU_BLOCK, j * LANES:(j + 1) * LANES]
            rhs = jnp.concatenate([jnp.where(lo_mask, vp, zero),
                                   jnp.where(lo_mask, zero, vp)], axis=0)
            sv = jnp.dot(wcat[j], rhs, preferred_element_type=F32)
            sv = sv + sbias_ref[:, j * LANES:(j + 1) * LANES]
            ob_ref[r0:r0 + SGU_BLOCK, j * LANES:(j + 1) * LANES] = (
                gu[r0:r0 + SGU_BLOCK, j * LANES:(j + 1) * LANES] * sv).astype(BF16)

    c_g = c_u + 2 * SGU_WIDTH
    d_model = sga_ref.shape[1]
    for c0 in range(0, d_model, 512):
        sga_ref[:, c0:c0 + 512] = jax.nn.sigmoid(proj(c_g + c0, 512)).astype(BF16)
        sgb_ref[:, c0:c0 + 512] = jax.nn.sigmoid(proj(c_g + d_model + c0, 512)).astype(BF16)


def _inproj(x2, g_mix, w_in, sgu_w, sgu_bias, sgu_norm):
    t, d = x2.shape
    ncol = w_in.shape[1]
    tm = TM_PROJ
    const = lambda *shape: pl.BlockSpec(shape, lambda i: (0,) * len(shape))
    return pl.pallas_call(
        _inproj_kernel,
        grid=(t // tm,),
        in_specs=[
            pl.BlockSpec((tm, d), lambda i: (i, 0)),
            const(1, d),
            const(d, ncol),
            const(SGU_GROUPS, SGU_BLOCK, SGU_BLOCK),
            const(SGU_BLOCK, SGU_WIDTH),
            const(1, SGU_WIDTH),
        ],
        out_specs=[
            pl.BlockSpec((tm, 3 * SB_WIDTH), lambda i: (i, 0)),
            pl.BlockSpec((tm, SGU_WIDTH), lambda i: (i, 0)),
            pl.BlockSpec((tm, d), lambda i: (i, 0)),
            pl.BlockSpec((tm, d), lambda i: (i, 0)),
        ],
        out_shape=[
            jax.ShapeDtypeStruct((t, 3 * SB_WIDTH), BF16),
            jax.ShapeDtypeStruct((t, SGU_WIDTH), BF16),
            jax.ShapeDtypeStruct((t, d), BF16),
            jax.ShapeDtypeStruct((t, d), BF16),
        ],
        compiler_params=_cparams(("arbitrary",)),
        name="inproj_sgu",
    )(x2, g_mix, w_in, sgu_w, sgu_bias, sgu_norm)


def _attn_kernel(q_ref, k_ref, v_ref, ntri_ref, o_ref, carry_sc, acc_sc):
    tq = TQ_ATTN
    kc = KC_ATTN
    lane = lax.broadcasted_iota(jnp.int32, (tq, LANES), 1)
    head0 = lane < SB_HEAD_DIM
    zero = jnp.zeros((), BF16)
    row = lax.broadcasted_iota(jnp.int32, (tq, kc), 0)
    col = lax.broadcasted_iota(jnp.int32, (tq, kc), 1)

    def tile(sub, _):
        qi = pl.program_id(2) * (q_ref.shape[1] // tq) + sub
        r0 = pl.multiple_of(sub * tq, tq)
        q = q_ref[0, pl.ds(r0, tq), :]
        qh = (jnp.where(head0, q, zero), jnp.where(head0, zero, q))
        carry_sc[...] = jnp.zeros_like(carry_sc)
        acc_sc[...] = jnp.zeros_like(acc_sc)

        def scores(j, key_off, keep):
            k0 = pl.multiple_of(j * kc, kc)
            kj = k_ref[0, pl.ds(k0, kc), :]
            mask = None if key_off is None else (col + key_off) < row
            out = []
            for h in range(2):
                z = lax.dot_general(qh[h], kj, (((1,), (1,)), ((), ())), preferred_element_type=F32)
                sp = jnp.maximum(z, 0.0) + jnp.log2(1.0 + jnp.exp2(-jnp.abs(z)))
                if mask is not None:
                    sp = jnp.where(mask, sp, 0.0)
                if keep is not None:
                    sp = jnp.where(keep, sp, 0.0)
                out.append((z, jnp.dot(sp.astype(BF16), ntri_ref[...], preferred_element_type=F32)))
            return k0, mask, keep, out

        def weigh(chunk):
            k0, mask, keep, out = chunk
            vj = v_ref[0, pl.ds(k0, kc), :]
            for h in range(2):
                z, cum = out[h]
                carry = carry_sc[h]
                w = jnp.exp2(z + cum + jnp.concatenate([carry] * (kc // LANES), axis=1))
                carry_sc[h] = carry + jnp.broadcast_to(cum[:, 0:1], carry.shape)
                if mask is not None:
                    w = jnp.where(mask, w, 0.0)
                if keep is not None:
                    w = jnp.where(keep, w, 0.0)
                acc_sc[h] += jnp.dot(w.astype(BF16), vj, preferred_element_type=F32)

        ndiag = tq // kc
        npast = qi * ndiag
        head = [scores(npast + c, c * kc, None) for c in reversed(range(ndiag))]
        head.append(scores(jnp.maximum(npast - 1, 0), None, npast > 0))
        for chunk in head:
            weigh(chunk)

        def live(t):
            return jnp.logical_and(t < npast, jnp.max(carry_sc[...]) > UNDERFLOW_LOG2)

        def past(t):
            weigh(scores(npast - 1 - t, None, None))
            return t + 1

        lax.while_loop(live, past, 1)
        o_ref[0, pl.ds(r0, tq), :] = jnp.where(head0, acc_sc[0], acc_sc[1]).astype(o_ref.dtype)
        return 0

    lax.fori_loop(0, q_ref.shape[1] // tq, tile, 0)


def _attention(qkv3, ntri):
    b, s, _ = qkv3.shape
    tq = TQ_ATTN * ATTN_TILES_PER_STEP
    npair = SB_WIDTH // LANES
    return pl.pallas_call(
        _attn_kernel,
        grid=(b, npair, s // tq),
        in_specs=[
            pl.BlockSpec((1, tq, LANES), lambda bi, p, i: (bi, i, p)),
            pl.BlockSpec((1, s, LANES), lambda bi, p, i: (bi, 0, npair + p)),
            pl.BlockSpec((1, s, LANES), lambda bi, p, i: (bi, 0, 2 * npair + p)),
            pl.BlockSpec((KC_ATTN, KC_ATTN), lambda bi, p, i: (0, 0)),
        ],
        out_specs=pl.BlockSpec((1, tq, LANES), lambda bi, p, i: (bi, i, p)),
        out_shape=jax.ShapeDtypeStruct((b, s, SB_WIDTH), BF16),
        scratch_shapes=[pltpu.VMEM((2, TQ_ATTN, LANES), F32),
                        pltpu.VMEM((2, TQ_ATTN, LANES), F32)],
        compiler_params=_cparams(("arbitrary", "arbitrary", "arbitrary")),
        name="stickbreak_attn",
    )(qkv3, qkv3, qkv3, ntri)


def _outproj_kernel(oa_ref, ob_ref, sga_ref, sgb_ref, x_ref, woa_ref, wob_ref, wo_ref,
                    g_ref, wr_ref, br_ref, upper_ref,
                    h_ref, c_ref, lslot_ref, gate_ref, cnt_ref):
    tm = x_ref.shape[0]

    ma = jnp.dot(oa_ref[...], woa_ref[...], preferred_element_type=F32)
    mb = jnp.dot(ob_ref[...], wob_ref[...], preferred_element_type=F32)
    merged = sga_ref[...].astype(F32) * ma + sgb_ref[...].astype(F32) * mb
    h = x_ref[...] + jnp.dot(merged.astype(BF16), wo_ref[...], preferred_element_type=F32)
    h_ref[...] = h
    c = _rms(h, g_ref[...])
    c_ref[...] = c.astype(BF16)

    logits = lax.dot_general(wr_ref[...], c, (((1,), (1,)), ((), ())),
                             precision=lax.Precision.HIGHEST,
                             preferred_element_type=F32) + br_ref[...]
    eid = lax.broadcasted_iota(jnp.int32, (N_EXPERTS, tm), 0).astype(F32)
    work = logits
    vals, sels = [], []
    for _ in range(TOP_K):
        m = jnp.max(work, axis=0, keepdims=True)
        ik = jnp.min(jnp.where(work == m, eid, float(N_EXPERTS)), axis=0, keepdims=True)
        sel = eid == ik
        vals.append(m)
        sels.append(sel)
        work = jnp.where(sel, -jnp.inf, work)
    es = [jnp.exp(v - vals[0]) for v in vals]
    inv = 1.0 / (es[0] + es[1] + es[2] + es[3])
    onehot = jnp.zeros((N_EXPERTS, tm), F32)
    for sel in sels:
        onehot = onehot + jnp.where(sel, 1.0, 0.0)

    prefix = jnp.dot(onehot.astype(BF16), upper_ref[...], preferred_element_type=F32)
    n = jnp.sum(onehot, axis=1, keepdims=True)
    n_al = jnp.ceil(n * (1.0 / ROW_ALIGN)) * ROW_ALIGN
    er = lax.broadcasted_iota(jnp.int32, (N_EXPERTS, N_EXPERTS), 0)
    ec = lax.broadcasted_iota(jnp.int32, (N_EXPERTS, N_EXPERTS), 1)
    run_off = jnp.dot(jnp.where(ec < er, 1.0, 0.0), jnp.broadcast_to(n_al, (N_EXPERTS, LANES)),
                      precision=lax.Precision.HIGHEST, preferred_element_type=F32)
    slot = prefix + run_off[:, 0:1]
    for k in range(TOP_K):
        gate_ref[k:k + 1, :] = es[k] * inv
        lslot_ref[k:k + 1, :] = jnp.sum(jnp.where(sels[k], slot, 0.0), axis=0,
                                        keepdims=True).astype(jnp.int32)
    cnt_ref[0] = jnp.broadcast_to(n, (N_EXPERTS, LANES))


def _outproj(o_a, o_b, sga, sgb, x2, w_out_a, w_out_b, w_out, g_moe, w_router_t, b_router, upper):
    t, d = x2.shape
    tm = TM_PROJ
    const = lambda *shape: pl.BlockSpec(shape, lambda i: (0,) * len(shape))
    row = lambda w: pl.BlockSpec((tm, w), lambda i: (i, 0))
    colk = pl.BlockSpec((TOP_K, tm), lambda i: (0, i))
    return pl.pallas_call(
        _outproj_kernel,
        grid=(t // tm,),
        in_specs=[row(SB_WIDTH), row(SGU_WIDTH), row(d), row(d), row(d),
                  const(SB_WIDTH, d), const(SGU_WIDTH, d), const(d, d),
                  const(1, d), const(N_EXPERTS, d), const(N_EXPERTS, 1), const(tm, tm)],
        out_specs=[row(d), row(d), colk, colk,
                   pl.BlockSpec((1, N_EXPERTS, LANES), lambda i: (i, 0, 0))],
        out_shape=[
            jax.ShapeDtypeStruct((t, d), F32),
            jax.ShapeDtypeStruct((t, d), BF16),
            jax.ShapeDtypeStruct((TOP_K, t), jnp.int32),
            jax.ShapeDtypeStruct((TOP_K, t), F32),
            jax.ShapeDtypeStruct((t // tm, N_EXPERTS, LANES), F32),
        ],
        compiler_params=_cparams(("arbitrary",)),
        name="outproj_router",
    )(o_a, o_b, sga, sgb, x2, w_out_a, w_out_b, w_out, g_moe, w_router_t, b_router, upper)


def _piece_loops(tables, win, fn):
    n_ref, local_ref, global_ref = tables
    for cls, rows in enumerate(PIECE_ROWS):
        head = win * len(PIECE_ROWS) + cls

        def body(i, _, off=head * MAX_PIECES, rows=rows):
            fn(pl.multiple_of(local_ref[off + i], ROW_ALIGN),
               pl.multiple_of(global_ref[off + i], ROW_ALIGN), rows)
            return 0

        lax.fori_loop(0, n_ref[head], body, 0)


def _piece_tables(run, loff, base):
    big = PIECE_ROWS[0]
    nbig = run // big
    nsmall = (run - nbig * big) // ROW_ALIGN
    i = jnp.arange(MAX_PIECES, dtype=jnp.int32)
    experts = jnp.arange(N_EXPERTS, dtype=jnp.int32)

    def flat(cnt, rows, skip):
        cum = jnp.cumsum(cnt, axis=1)
        owner = jnp.minimum(jnp.sum(i[None, :, None] >= cum[:, None, :], axis=-1), N_EXPERTS - 1)
        onehot = owner[:, :, None] == experts
        pick = lambda tab: jnp.sum(jnp.where(onehot, tab[:, None, :], 0), axis=-1)
        j = i[None, :] - pick(cum - cnt)
        return cum[:, -1], pick(loff + skip) + j * rows, pick(base + skip) + j * rows

    nb, lb, gb = flat(nbig, big, 0)
    ns, lsm, gsm = flat(nsmall, ROW_ALIGN, nbig * big)
    pack = lambda a, b: jnp.stack([a, b], axis=1).reshape(-1).astype(jnp.int32)
    return pack(nb, ns), pack(lb, lsm), pack(gb, gsm)


def _dispatch_kernel(pn_ref, pl_ref, pg_ref, pad_end_ref, lslot_ref, c_ref, xs_ref,
                     perm_sc, loc_sc, zero_sc, sem, zsem):
    w = pl.program_id(0)
    tm = c_ref.shape[0]
    tb = zero_sc.shape[0]
    nloc = loc_sc.shape[1]

    @pl.when(w == 0)
    def _():
        zero_sc[...] = jnp.zeros_like(zero_sc)

        def pad_copy(e):
            return pltpu.make_async_copy(
                zero_sc, xs_ref.at[pl.ds(pl.multiple_of(pad_end_ref[e] - tb, tb), tb), :], zsem)

        def nonempty(e):
            prev = jnp.where(e > 0, pad_end_ref[jnp.maximum(e - 1, 0)], 0)
            return pad_end_ref[e] > prev

        def zstart(e, _):
            @pl.when(nonempty(e))
            def _():
                pad_copy(e).start()
            return 0

        def zwait(e, _):
            @pl.when(nonempty(e))
            def _():
                pad_copy(e).wait()
            return 0

        lax.fori_loop(0, N_EXPERTS, zstart, 0)
        lax.fori_loop(0, N_EXPERTS, zwait, 0)

    ls = [lslot_ref[k:k + 1, :] for k in range(TOP_K)]
    for r0 in range(0, nloc, SLOT_CHUNK):
        srow = lax.broadcasted_iota(jnp.int32, (SLOT_CHUNK, tm), 0) + r0
        sel = jnp.zeros((SLOT_CHUNK, tm), F32)
        for k in range(TOP_K):
            sel = jnp.where(srow == ls[k], 1.0, sel)
        perm_sc[r0:r0 + SLOT_CHUNK, :] = sel.astype(BF16)
    buf = lax.bitwise_and(w, 1)
    loc_sc[buf] = jnp.dot(perm_sc[...], c_ref[...], preferred_element_type=F32)

    tables = (pn_ref, pl_ref, pg_ref)

    def piece_copy(slot):
        def build(local_row, global_row, rows):
            return pltpu.make_async_copy(loc_sc.at[slot, pl.ds(local_row, rows), :],
                                         xs_ref.at[pl.ds(global_row, rows), :], sem.at[slot])
        return build

    _piece_loops(tables, w, lambda *a: piece_copy(buf)(*a).start())

    @pl.when(w > 0)
    def _():
        _piece_loops(tables, w - 1, lambda *a: piece_copy(1 - buf)(*a).wait())

    @pl.when(w == pl.num_programs(0) - 1)
    def _():
        _piece_loops(tables, w, lambda *a: piece_copy(buf)(*a).wait())


def _dispatch(pieces, pad_end, lslot, c, n_slots):
    t, d = c.shape
    tm = TM_PROJ
    grid_spec = pltpu.PrefetchScalarGridSpec(
        num_scalar_prefetch=4,
        grid=(t // tm,),
        in_specs=[
            pl.BlockSpec((TOP_K, tm), lambda i, *_: (0, i)),
            pl.BlockSpec((tm, d), lambda i, *_: (i, 0)),
        ],
        out_specs=pl.BlockSpec(memory_space=pl.ANY),
        scratch_shapes=[pltpu.VMEM((LOCAL_SLOTS, tm), BF16),
                        pltpu.VMEM((2, LOCAL_SLOTS, d), F32),
                        pltpu.VMEM((TB_EXPERT, d), F32),
                        pltpu.SemaphoreType.DMA((2,)), pltpu.SemaphoreType.DMA(())],
    )
    return pl.pallas_call(
        _dispatch_kernel,
        grid_spec=grid_spec,
        out_shape=jax.ShapeDtypeStruct((n_slots, d), F32),
        compiler_params=_cparams(("arbitrary",)),
        name="moe_dispatch",
    )(*pieces, pad_end, lslot, c)


def _expert_kernel(first_ref, count_ref, xs_ref, w1_ref, w2_ref, b1g_ref, b1l_ref, b2_ref,
                   perm_ref, ys_ref, xbuf, ybuf, w1g_sc, w1l_sc, w2_sc, in_sem, out_sem):
    e = pl.program_id(0)
    nblk = count_ref[e]
    tb = xbuf.shape[1]

    def rows(b):
        return pl.ds(pl.multiple_of((first_ref[e] + b) * tb, tb), tb)

    def in_copy(b, slot):
        return pltpu.make_async_copy(xs_ref.at[rows(b), :], xbuf.at[slot], in_sem.at[slot])

    def out_copy(b, slot):
        return pltpu.make_async_copy(ybuf.at[slot], ys_ref.at[rows(b), :], out_sem.at[slot])

    @pl.when(nblk > 0)
    def _():
        in_copy(0, 0).start()

        half = MXU_DIM // 2
        for cb in range(w1_ref.shape[2] // MXU_DIM):
            blk = w1_ref[0, :, cb * MXU_DIM:(cb + 1) * MXU_DIM].astype(BF16)
            de = jnp.dot(blk, perm_ref[...], preferred_element_type=F32)
            w1g_sc[:, cb * half:(cb + 1) * half] = de[:, :half].astype(BF16)
            w1l_sc[:, cb * half:(cb + 1) * half] = de[:, half:].astype(BF16)
        w2_sc[...] = w2_ref[0].astype(BF16)

        def block(b, _):
            slot = lax.bitwise_and(b, 1)

            @pl.when(b + 1 < nblk)
            def _():
                in_copy(b + 1, 1 - slot).start()

            in_copy(b, slot).wait()

            @pl.when(b >= 2)
            def _():
                out_copy(b - 2, slot).wait()

            xb = xbuf[slot].astype(BF16)
            hg = jnp.dot(xb, w1g_sc[...], preferred_element_type=F32) + b1g_ref[0]
            hl = jnp.dot(xb, w1l_sc[...], preferred_element_type=F32) + b1l_ref[0]
            xg = jnp.minimum(hg, SWIGLU_LIMIT)
            xl = jnp.clip(hl, -SWIGLU_LIMIT, SWIGLU_LIMIT)
            act = xg * jax.nn.sigmoid(SWIGLU_ALPHA * xg) * (xl + 1.0)
            ybuf[slot] = jnp.dot(act.astype(BF16), w2_sc[...],
                                 preferred_element_type=F32) + b2_ref[0]
            out_copy(b, slot).start()
            return 0

        lax.fori_loop(0, nblk, block, 0)

        @pl.when(nblk >= 2)
        def _():
            out_copy(nblk - 2, lax.bitwise_and(nblk, 1)).wait()

        out_copy(nblk - 1, 1 - lax.bitwise_and(nblk, 1)).wait()


def _experts(first_blk, blk_count, xs, w1, w2, b1g, b1l, b2, perm):
    n_slots, d = xs.shape
    n_exp, _, f2 = w1.shape
    f = f2 // 2
    tb = TB_EXPERT
    wsel = lambda e, *_: (e, 0, 0)
    grid_spec = pltpu.PrefetchScalarGridSpec(
        num_scalar_prefetch=2,
        grid=(n_exp,),
        in_specs=[
            pl.BlockSpec(memory_space=pl.ANY),
            pl.BlockSpec((1, d, f2), wsel),
            pl.BlockSpec((1, f, d), wsel),
            pl.BlockSpec((1, 1, f), wsel),
            pl.BlockSpec((1, 1, f), wsel),
            pl.BlockSpec((1, 1, d), wsel),
            pl.BlockSpec((MXU_DIM, MXU_DIM), lambda e, *_: (0, 0)),
        ],
        out_specs=pl.BlockSpec(memory_space=pl.ANY),
        scratch_shapes=[pltpu.VMEM((2, tb, d), F32), pltpu.VMEM((2, tb, d), F32),
                        pltpu.VMEM((d, f), BF16), pltpu.VMEM((d, f), BF16),
                        pltpu.VMEM((f, d), BF16),
                        pltpu.SemaphoreType.DMA((2,)), pltpu.SemaphoreType.DMA((2,))],
    )
    return pl.pallas_call(
        _expert_kernel,
        grid_spec=grid_spec,
        out_shape=jax.ShapeDtypeStruct((n_slots, d), F32),
        compiler_params=_cparams(("arbitrary",)),
        name="moe_experts",
    )(first_blk, blk_count, xs, w1, w2, b1g, b1l, b2, perm)


def _combine_kernel(pn_ref, pl_ref, pg_ref, lslot_ref, gate_ref, h_ref, p_ref, ys_ref,
                    gp_ref, wg_ref, wp_ref, gf_ref, out_ref, loc_sc, locb_sc, comb_sc, sem,
                    *, final):
    w = pl.program_id(0)
    tm = h_ref.shape[0]
    nloc = loc_sc.shape[0]

    tables = (pn_ref, pl_ref, pg_ref)

    def piece_copy(local_row, global_row, rows):
        return pltpu.make_async_copy(ys_ref.at[pl.ds(global_row, rows), :],
                                     loc_sc.at[pl.ds(local_row, rows), :], sem)

    @pl.when(w == 0)
    def _():
        loc_sc[...] = jnp.zeros_like(loc_sc)
        _piece_loops(tables, w, lambda *a: piece_copy(*a).start())

    _piece_loops(tables, w, lambda *a: piece_copy(*a).wait())
    for c0 in range(0, nloc, SLOT_CHUNK):
        locb_sc[c0:c0 + SLOT_CHUNK, :] = loc_sc[c0:c0 + SLOT_CHUNK, :].astype(BF16)

    @pl.when(w + 1 < pl.num_programs(0))
    def _():
        _piece_loops(tables, w + 1, lambda *a: piece_copy(*a).start())

    lcol = [lslot_ref[:, k:k + 1] for k in range(TOP_K)]
    gcol = [gate_ref[:, k:k + 1] for k in range(TOP_K)]
    for c0 in range(0, nloc, SLOT_CHUNK):
        scol = lax.broadcasted_iota(jnp.int32, (tm, SLOT_CHUNK), 1) + c0
        g = jnp.zeros((tm, SLOT_CHUNK), F32)
        for k in range(TOP_K):
            g = jnp.where(scol == lcol[k], gcol[k], g)
        comb_sc[:, c0:c0 + SLOT_CHUNK] = g.astype(BF16)
    h = h_ref[...] + jnp.dot(comb_sc[...], locb_sc[...], preferred_element_type=F32)

    r = _rms(h, gp_ref[...]).astype(BF16)
    pg = jax.nn.sigmoid(jnp.dot(r, wg_ref[...], preferred_element_type=F32))
    pp = jnp.dot(p_ref[...].astype(BF16), wp_ref[...], preferred_element_type=F32)
    h = h + pg * pp
    out_ref[...] = _rms(h, gf_ref[...]) if final else h


def _combine(pieces, lslot_tk, gate_tk, h1, p2, ys, g_ple, w_ple_gate, w_ple_proj, g_final,
             final):
    t, d = h1.shape
    pd = p2.shape[1]
    tm = TM_PROJ
    const = lambda *shape: pl.BlockSpec(shape, lambda i, *_: (0,) * len(shape))
    row = lambda width: pl.BlockSpec((tm, width), lambda i, *_: (i, 0))
    grid_spec = pltpu.PrefetchScalarGridSpec(
        num_scalar_prefetch=3,
        grid=(t // tm,),
        in_specs=[row(TOP_K), row(TOP_K), row(d), row(pd),
                  pl.BlockSpec(memory_space=pl.ANY),
                  const(1, d), const(d, d), const(pd, d), const(1, d)],
        out_specs=row(d),
        scratch_shapes=[pltpu.VMEM((LOCAL_SLOTS, d), F32), pltpu.VMEM((LOCAL_SLOTS, d), BF16),
                        pltpu.VMEM((tm, LOCAL_SLOTS), BF16), pltpu.SemaphoreType.DMA(())],
    )
    return pl.pallas_call(
        functools.partial(_combine_kernel, final=final),
        grid_spec=grid_spec,
        out_shape=jax.ShapeDtypeStruct((t, d), F32),
        compiler_params=_cparams(("arbitrary",)),
        name="combine_ple",
    )(*pieces, lslot_tk, gate_tk, h1, p2, ys, g_ple, w_ple_gate, w_ple_proj, g_final)


def kernel(x, p, g_mix, w_in, w_out_a, w_out_b, w_out, sgu_norm, sgu_w, sgu_b, g_moe,
           w_router, b_router, w1, b1, w2, b2, g_ple, w_ple_gate, w_ple_proj, g_final):
    b, s, d = x.shape
    depth = w_in.shape[0]
    t = b * s
    assert s % (TQ_ATTN * ATTN_TILES_PER_STEP) == 0 and t % TM_PROJ == 0

    kk = jnp.arange(KC_ATTN)
    ntri = -(kk[:, None] >= kk[None, :]).astype(BF16)
    tt = jnp.arange(TM_PROJ)
    upper = (tt[:, None] < tt[None, :]).astype(BF16)
    rr = jnp.arange(MXU_DIM)[:, None]
    cc = jnp.arange(MXU_DIM)[None, :]
    half = MXU_DIM // 2
    perm = (rr == jnp.where(cc < half, 2 * cc, 2 * (cc - half) + 1)).astype(BF16)

    tb = TB_EXPERT
    n_win = t // TM_PROJ
    n_blocks = -(-(t * TOP_K + n_win * N_EXPERTS * (ROW_ALIGN - 1) + N_EXPERTS * (tb - 1)) // tb)
    n_slots = n_blocks * tb

    h = x.reshape(t, d)
    for i in range(depth):
        sgu_bias = jnp.repeat(sgu_b[i].T, SGU_GROUP_DIM, axis=1)
        qkv, o_b, sga, sgb = _inproj(h, g_mix[i][None], w_in[i].astype(BF16), sgu_w[i],
                                     sgu_bias, sgu_norm[i][None])
        o_a = _attention(qkv.reshape(b, s, 3 * SB_WIDTH), ntri).reshape(t, SB_WIDTH)
        h1, c, lslot, gate, cnt = _outproj(
            o_a, o_b, sga, sgb, h, w_out_a[i].astype(BF16), w_out_b[i].astype(BF16),
            w_out[i].astype(BF16), g_moe[i][None], w_router[i].T, b_router[i][:, None], upper)

        counts = cnt[:, :, 0].astype(jnp.int32)
        run = ((counts + ROW_ALIGN - 1) // ROW_ALIGN) * ROW_ALIGN
        loff = jnp.cumsum(run, axis=1) - run
        region = ((jnp.sum(run, axis=0) + tb - 1) // tb) * tb
        pad_end = jnp.cumsum(region).astype(jnp.int32)
        base = (pad_end - region)[None, :] + jnp.cumsum(run, axis=0) - run
        pieces = _piece_tables(run, loff, base)
        first_blk = ((pad_end - region) // tb).astype(jnp.int32)
        blk_count = (region // tb).astype(jnp.int32)

        xs = _dispatch(pieces, pad_end, lslot, c, n_slots)
        ys = _experts(first_blk, blk_count, xs, w1[i], w2[i],
                      b1[i][:, None, 0::2], b1[i][:, None, 1::2], b2[i][:, None, :], perm)
        h = _combine(pieces, lslot.T, gate.T, h1, p[i].reshape(t, -1), ys, g_ple[i][None],
                     w_ple_gate[i].astype(BF16), w_ple_proj[i].astype(BF16), g_final[None],
                     final=(i == depth - 1))
    return h.reshape(b, s, d)
```

```python
import functools
import math

import jax
import jax.numpy as jnp
from jax import lax
from jax.experimental import pallas as pl
from jax.experimental.pallas import tpu as pltpu

F32 = jnp.float32
BF16 = jnp.bfloat16

EPS = 1e-6
CHUNK = 64
SB_HEADS = 8
SB_HEAD_DIM = 64
SB_WIDTH = SB_HEADS * SB_HEAD_DIM
SGU_GROUPS = 8
SGU_WIDTH = 512
SGU_GROUP_DIM = SGU_WIDTH // SGU_GROUPS
SGU_BLOCK = 128
N_EXPERTS = 32
TOP_K = 4
SWIGLU_ALPHA = 1.702
SWIGLU_LIMIT = 7.0
LOG2E = 1.4426950408889634
UNDERFLOW_LOG2 = -160.0

LANES = 128
MXU_DIM = 256
VMEM_LIMIT_BYTES = 56 * 1024 * 1024

TM_PROJ = 512
TQ_ATTN = 512
ATTN_TILES_PER_STEP = 2
KC_ATTN = MXU_DIM
TB_EXPERT = 512
ROW_ALIGN = 8
PIECE_ROWS = (32, ROW_ALIGN)
SLOT_CHUNK = 256
LOCAL_SLOTS = -(-(TM_PROJ * TOP_K + N_EXPERTS * (ROW_ALIGN - 1)) // SLOT_CHUNK) * SLOT_CHUNK
MAX_PIECES = max(LOCAL_SLOTS // PIECE_ROWS[0],
                 N_EXPERTS * (PIECE_ROWS[0] // PIECE_ROWS[1] - 1))


def _cparams(sem):
    return pltpu.CompilerParams(dimension_semantics=sem,
                                vmem_limit_bytes=VMEM_LIMIT_BYTES)


def _rms(x, g):
    ms = jnp.mean(x * x, axis=-1, keepdims=True)
    return x * lax.rsqrt(ms + EPS) * g


def _sigmoid(x):
    return 0.5 * jnp.tanh(0.5 * x) + 0.5


def _inproj_kernel(x_ref, g_ref, w_ref, sw_ref, sbias_ref, sn_ref,
                   qkv_ref, ob_ref, sga_ref, sgb_ref):
    tm = x_ref.shape[0]
    a = _rms(x_ref[...], g_ref[...]).astype(BF16)

    def proj(c0, width):
        return jnp.dot(a, w_ref[:, c0:c0 + width], preferred_element_type=F32)

    qkv_ref[:, 0:SB_WIDTH] = (proj(0, SB_WIDTH) * (LOG2E / math.sqrt(SB_HEAD_DIM))).astype(BF16)
    qkv_ref[:, SB_WIDTH:2 * SB_WIDTH] = proj(SB_WIDTH, SB_WIDTH).astype(BF16)
    qkv_ref[:, 2 * SB_WIDTH:3 * SB_WIDTH] = proj(2 * SB_WIDTH, SB_WIDTH).astype(BF16)

    c_u = 3 * SB_WIDTH
    gu = jax.nn.gelu(proj(c_u, SGU_WIDTH))
    gv = jax.nn.gelu(proj(c_u + SGU_WIDTH, SGU_WIDTH))
    mu = jnp.mean(gv, axis=-1, keepdims=True)
    d = gv - mu
    var = jnp.mean(d * d, axis=-1, keepdims=True)
    vn = (d * lax.rsqrt(var + EPS) * sn_ref[...]).astype(BF16)

    rr = lax.broadcasted_iota(jnp.int32, (SGU_BLOCK, SGU_BLOCK), 0) // CHUNK
    cc = lax.broadcasted_iota(jnp.int32, (SGU_BLOCK, SGU_BLOCK), 1) // CHUNK
    cmask = cc <= rr
    lane = lax.broadcasted_iota(jnp.int32, (SGU_BLOCK, LANES), 1)
    lo_mask = lane < SGU_GROUP_DIM
    wcat = []
    for j in range(SGU_GROUPS // 2):
        w0 = jnp.where(cmask, sw_ref[2 * j], 0.0).astype(BF16)
        w1 = jnp.where(cmask, sw_ref[2 * j + 1], 0.0).astype(BF16)
        wcat.append(jnp.concatenate([w0, w1], axis=1))

    zero = jnp.zeros((), BF16)
    for blk in range(tm // SGU_BLOCK):
        r0 = blk * SGU_BLOCK
        for j in range(SGU_GROUPS // 2):
            vp = vn[r0:r0 + SGU_BLOCK, j * LANES:(j + 1) * LANES]
            rhs = jnp.concatenate([jnp.where(lo_mask, vp, zero),
                                   jnp.where(lo_mask, zero, vp)], axis=0)
            sv = jnp.dot(wcat[j], rhs, preferred_element_type=F32)
            sv = sv + sbias_ref[:, j * LANES:(j + 1) * LANES]
            ob_ref[r0:r0 + SGU_BLOCK, j * LANES:(j + 1) * LANES] = (
                gu[r0:r0 + SGU_BLOCK, j * LANES:(j + 1) * LANES] * sv).astype(BF16)

    c_g = c_u + 2 * SGU_WIDTH
    d_model = sga_ref.shape[1]
    for c0 in range(0, d_model, 512):
        sga_ref[:, c0:c0 + 512] = _sigmoid(proj(c_g + c0, 512)).astype(BF16)
        sgb_ref[:, c0:c0 + 512] = _sigmoid(proj(c_g + d_model + c0, 512)).astype(BF16)


def _inproj(x2, g_mix, w_in, sgu_w, sgu_bias, sgu_norm):
    t, d = x2.shape
    ncol = w_in.shape[1]
    tm = TM_PROJ
    const = lambda *shape: pl.BlockSpec(shape, lambda i: (0,) * len(shape))
    return pl.pallas_call(
        _inproj_kernel,
        grid=(t // tm,),
        in_specs=[
            pl.BlockSpec((tm, d), lambda i: (i, 0)),
            const(1, d),
            const(d, ncol),
            const(SGU_GROUPS, SGU_BLOCK, SGU_BLOCK),
            const(SGU_BLOCK, SGU_WIDTH),
            const(1, SGU_WIDTH),
        ],
        out_specs=[
            pl.BlockSpec((tm, 3 * SB_WIDTH), lambda i: (i, 0)),
            pl.BlockSpec((tm, SGU_WIDTH), lambda i: (i, 0)),
            pl.BlockSpec((tm, d), lambda i: (i, 0)),
            pl.BlockSpec((tm, d), lambda i: (i, 0)),
        ],
        out_shape=[
            jax.ShapeDtypeStruct((t, 3 * SB_WIDTH), BF16),
            jax.ShapeDtypeStruct((t, SGU_WIDTH), BF16),
            jax.ShapeDtypeStruct((t, d), BF16),
            jax.ShapeDtypeStruct((t, d), BF16),
        ],
        compiler_params=_cparams(("arbitrary",)),
        name="inproj_sgu",
    )(x2, g_mix, w_in, sgu_w, sgu_bias, sgu_norm)


def _attn_kernel(q_ref, k_ref, v_ref, ntri_ref, o_ref, carry_sc, acc_sc):
    tq = TQ_ATTN
    kc = KC_ATTN
    lane = lax.broadcasted_iota(jnp.int32, (tq, LANES), 1)
    head0 = lane < SB_HEAD_DIM
    zero = jnp.zeros((), BF16)
    row = lax.broadcasted_iota(jnp.int32, (tq, kc), 0)
    col = lax.broadcasted_iota(jnp.int32, (tq, kc), 1)

    def tile(sub, _):
        qi = pl.program_id(2) * (q_ref.shape[1] // tq) + sub
        r0 = pl.multiple_of(sub * tq, tq)
        q = q_ref[0, pl.ds(r0, tq), :]
        qh = (jnp.where(head0, q, zero), jnp.where(head0, zero, q))
        carry_sc[...] = jnp.zeros_like(carry_sc)
        acc_sc[...] = jnp.zeros_like(acc_sc)

        def scores(j, key_off, keep):
            k0 = pl.multiple_of(j * kc, kc)
            kj = k_ref[0, pl.ds(k0, kc), :]
            mask = None if key_off is None else (col + key_off) < row
            out = []
            for h in range(2):
                z = lax.dot_general(qh[h], kj, (((1,), (1,)), ((), ())), preferred_element_type=F32)
                sp = jnp.maximum(z, 0.0) + jnp.log2(1.0 + jnp.exp2(-jnp.abs(z)))
                if mask is not None:
                    sp = jnp.where(mask, sp, 0.0)
                if keep is not None:
                    sp = jnp.where(keep, sp, 0.0)
                out.append((z, jnp.dot(sp.astype(BF16), ntri_ref[...], preferred_element_type=F32)))
            return k0, mask, keep, out

        def weigh(chunk):
            k0, mask, keep, out = chunk
            vj = v_ref[0, pl.ds(k0, kc), :]
            for h in range(2):
                z, cum = out[h]
                carry = carry_sc[h]
                w = jnp.exp2(z + cum + jnp.concatenate([carry] * (kc // LANES), axis=1))
                carry_sc[h] = carry + jnp.broadcast_to(cum[:, 0:1], carry.shape)
                if mask is not None:
                    w = jnp.where(mask, w, 0.0)
                if keep is not None:
                    w = jnp.where(keep, w, 0.0)
                acc_sc[h] += jnp.dot(w.astype(BF16), vj, preferred_element_type=F32)

        ndiag = tq // kc
        npast = qi * ndiag
        head = [scores(npast + c, c * kc, None) for c in reversed(range(ndiag))]
        head.append(scores(jnp.maximum(npast - 1, 0), None, npast > 0))
        for chunk in head:
            weigh(chunk)

        def live(t):
            return jnp.logical_and(t < npast, jnp.max(carry_sc[...]) > UNDERFLOW_LOG2)

        def past(t):
            weigh(scores(npast - 1 - t, None, None))
            return t + 1

        lax.while_loop(live, past, 1)
        o_ref[0, pl.ds(r0, tq), :] = jnp.where(head0, acc_sc[0], acc_sc[1]).astype(o_ref.dtype)
        return 0

    lax.fori_loop(0, q_ref.shape[1] // tq, tile, 0)


def _attention(qkv3, ntri):
    b, s, _ = qkv3.shape
    tq = TQ_ATTN * ATTN_TILES_PER_STEP
    npair = SB_WIDTH // LANES
    return pl.pallas_call(
        _attn_kernel,
        grid=(b, npair, s // tq),
        in_specs=[
            pl.BlockSpec((1, tq, LANES), lambda bi, p, i: (bi, i, p)),
            pl.BlockSpec((1, s, LANES), lambda bi, p, i: (bi, 0, npair + p)),
            pl.BlockSpec((1, s, LANES), lambda bi, p, i: (bi, 0, 2 * npair + p)),
            pl.BlockSpec((KC_ATTN, KC_ATTN), lambda bi, p, i: (0, 0)),
        ],
        out_specs=pl.BlockSpec((1, tq, LANES), lambda bi, p, i: (bi, i, p)),
        out_shape=jax.ShapeDtypeStruct((b, s, SB_WIDTH), BF16),
        scratch_shapes=[pltpu.VMEM((2, TQ_ATTN, LANES), F32),
                        pltpu.VMEM((2, TQ_ATTN, LANES), F32)],
        compiler_params=_cparams(("arbitrary", "arbitrary", "arbitrary")),
        name="stickbreak_attn",
    )(qkv3, qkv3, qkv3, ntri)


def _outproj_kernel(oa_ref, ob_ref, sga_ref, sgb_ref, x_ref, woa_ref, wob_ref, wo_ref,
                    g_ref, wr_ref, br_ref, upper_ref,
                    h_ref, c_ref, lslot_ref, gate_ref, cnt_ref):
    tm = x_ref.shape[0]
    ma = jnp.dot(oa_ref[...], woa_ref[...], preferred_element_type=F32)
    mb = jnp.dot(ob_ref[...], wob_ref[...], preferred_element_type=F32)
    merged = sga_ref[...].astype(F32) * ma + sgb_ref[...].astype(F32) * mb
    h = x_ref[...] + jnp.dot(merged.astype(BF16), wo_ref[...], preferred_element_type=F32)
    h_ref[...] = h
    c = _rms(h, g_ref[...])
    c_hi = c.astype(BF16)
    c_ref[...] = c_hi

    c_lo = (c - c_hi.astype(F32)).astype(BF16)
    wr = wr_ref[...]
    wr_hi = wr.astype(BF16)
    wr_lo = (wr - wr_hi.astype(F32)).astype(BF16)
    nt = (((1,), (1,)), ((), ()))
    logits = (lax.dot_general(wr_hi, c_hi, nt, preferred_element_type=F32)
              + lax.dot_general(wr_hi, c_lo, nt, preferred_element_type=F32)
              + lax.dot_general(wr_lo, c_hi, nt, preferred_element_type=F32)) + br_ref[...]
    eid = lax.broadcasted_iota(jnp.int32, (N_EXPERTS, tm), 0).astype(F32)
    work = logits
    vals, sels = [], []
    for _ in range(TOP_K):
        m = jnp.max(work, axis=0, keepdims=True)
        ik = jnp.min(jnp.where(work == m, eid, float(N_EXPERTS)), axis=0, keepdims=True)
        sel = eid == ik
        vals.append(m)
        sels.append(sel)
        work = jnp.where(sel, -jnp.inf, work)
    es = [jnp.exp(v - vals[0]) for v in vals]
    inv = 1.0 / (es[0] + es[1] + es[2] + es[3])
    onehot = jnp.zeros((N_EXPERTS, tm), F32)
    for sel in sels:
        onehot = onehot + jnp.where(sel, 1.0, 0.0)

    prefix = jnp.dot(onehot.astype(BF16), upper_ref[...], preferred_element_type=F32)
    n = jnp.sum(onehot, axis=1, keepdims=True)
    n_al = jnp.ceil(n * (1.0 / ROW_ALIGN)) * ROW_ALIGN
    er = lax.broadcasted_iota(jnp.int32, (N_EXPERTS, N_EXPERTS), 0)
    ec = lax.broadcasted_iota(jnp.int32, (N_EXPERTS, N_EXPERTS), 1)
    run_off = jnp.dot(jnp.where(ec < er, 1.0, 0.0), jnp.broadcast_to(n_al, (N_EXPERTS, LANES)),
                      precision=lax.Precision.HIGHEST, preferred_element_type=F32)
    slot = prefix + run_off[:, 0:1]
    for k in range(TOP_K):
        gate_ref[k:k + 1, :] = es[k] * inv
        lslot_ref[k:k + 1, :] = jnp.sum(jnp.where(sels[k], slot, 0.0), axis=0,
                                        keepdims=True).astype(jnp.int32)
    cnt_ref[0] = jnp.broadcast_to(n, (N_EXPERTS, LANES))


def _outproj(o_a, o_b, sga, sgb, x2, w_out_a, w_out_b, w_out, g_moe, w_router_t, b_router, upper):
    t, d = x2.shape
    tm = TM_PROJ
    const = lambda *shape: pl.BlockSpec(shape, lambda i: (0,) * len(shape))
    row = lambda w: pl.BlockSpec((tm, w), lambda i: (i, 0))
    colk = pl.BlockSpec((TOP_K, tm), lambda i: (0, i))
    return pl.pallas_call(
        _outproj_kernel,
        grid=(t // tm,),
        in_specs=[row(SB_WIDTH), row(SGU_WIDTH), row(d), row(d), row(d),
                  const(SB_WIDTH, d), const(SGU_WIDTH, d), const(d, d),
                  const(1, d), const(N_EXPERTS, d), const(N_EXPERTS, 1), const(tm, tm)],
        out_specs=[row(d), row(d), colk, colk,
                   pl.BlockSpec((1, N_EXPERTS, LANES), lambda i: (i, 0, 0))],
        out_shape=[
            jax.ShapeDtypeStruct((t, d), F32),
            jax.ShapeDtypeStruct((t, d), BF16),
            jax.ShapeDtypeStruct((TOP_K, t), jnp.int32),
            jax.ShapeDtypeStruct((TOP_K, t), F32),
            jax.ShapeDtypeStruct((t // tm, N_EXPERTS, LANES), F32),
        ],
        compiler_params=_cparams(("arbitrary",)),
        name="outproj_router",
    )(o_a, o_b, sga, sgb, x2, w_out_a, w_out_b, w_out, g_moe, w_router_t, b_router, upper)


def _piece_loops(tables, win, fn):
    n_ref, local_ref, global_ref = tables
    for cls, rows in enumerate(PIECE_ROWS):
        head = win * len(PIECE_ROWS) + cls

        def body(i, _, off=head * MAX_PIECES, rows=rows):
            fn(pl.multiple_of(local_ref[off + i], ROW_ALIGN),
               pl.multiple_of(global_ref[off + i], ROW_ALIGN), rows)
            return 0

        lax.fori_loop(0, n_ref[head], body, 0)


def _piece_tables(run, loff, base):
    big = PIECE_ROWS[0]
    nbig = run // big
    nsmall = (run - nbig * big) // ROW_ALIGN
    i = jnp.arange(MAX_PIECES, dtype=jnp.int32)
    experts = jnp.arange(N_EXPERTS, dtype=jnp.int32)

    def flat(cnt, rows, skip):
        cum = jnp.cumsum(cnt, axis=1)
        owner = jnp.minimum(jnp.sum(i[None, :, None] >= cum[:, None, :], axis=-1), N_EXPERTS - 1)
        onehot = owner[:, :, None] == experts
        pick = lambda tab: jnp.sum(jnp.where(onehot, tab[:, None, :], 0), axis=-1)
        j = i[None, :] - pick(cum - cnt)
        return cum[:, -1], pick(loff + skip) + j * rows, pick(base + skip) + j * rows

    nb, lb, gb = flat(nbig, big, 0)
    ns, lsm, gsm = flat(nsmall, ROW_ALIGN, nbig * big)
    pack = lambda a, b: jnp.stack([a, b], axis=1).reshape(-1).astype(jnp.int32)
    return pack(nb, ns), pack(lb, lsm), pack(gb, gsm)


def _dispatch_kernel(pn_ref, pl_ref, pg_ref, pad_end_ref, lslot_ref, c_ref, xs_ref,
                     perm_sc, loc_sc, zero_sc, sem, zsem):
    w = pl.program_id(0)
    tm = c_ref.shape[0]
    tb = zero_sc.shape[0]
    nloc = loc_sc.shape[1]

    @pl.when(w == 0)
    def _():
        zero_sc[...] = jnp.zeros_like(zero_sc)

        def pad_copy(e):
            return pltpu.make_async_copy(
                zero_sc, xs_ref.at[pl.ds(pl.multiple_of(pad_end_ref[e] - tb, tb), tb), :], zsem)

        def nonempty(e):
            prev = jnp.where(e > 0, pad_end_ref[jnp.maximum(e - 1, 0)], 0)
            return pad_end_ref[e] > prev

        def zstart(e, _):
            @pl.when(nonempty(e))
            def _():
                pad_copy(e).start()
            return 0

        def zwait(e, _):
            @pl.when(nonempty(e))
            def _():
                pad_copy(e).wait()
            return 0

        lax.fori_loop(0, N_EXPERTS, zstart, 0)
        lax.fori_loop(0, N_EXPERTS, zwait, 0)

    ls = [lslot_ref[k:k + 1, :] for k in range(TOP_K)]
    for r0 in range(0, nloc, SLOT_CHUNK):
        srow = lax.broadcasted_iota(jnp.int32, (SLOT_CHUNK, tm), 0) + r0
        sel = jnp.zeros((SLOT_CHUNK, tm), F32)
        for k in range(TOP_K):
            sel = jnp.where(srow == ls[k], 1.0, sel)
        perm_sc[r0:r0 + SLOT_CHUNK, :] = sel.astype(BF16)
    buf = lax.bitwise_and(w, 1)
    loc_sc[buf] = jnp.dot(perm_sc[...], c_ref[...], preferred_element_type=F32)

    tables = (pn_ref, pl_ref, pg_ref)

    def piece_copy(slot):
        def build(local_row, global_row, rows):
            return pltpu.make_async_copy(loc_sc.at[slot, pl.ds(local_row, rows), :],
                                         xs_ref.at[pl.ds(global_row, rows), :], sem.at[slot])
        return build

    _piece_loops(tables, w, lambda *a: piece_copy(buf)(*a).start())

    @pl.when(w > 0)
    def _():
        _piece_loops(tables, w - 1, lambda *a: piece_copy(1 - buf)(*a).wait())

    @pl.when(w == pl.num_programs(0) - 1)
    def _():
        _piece_loops(tables, w, lambda *a: piece_copy(buf)(*a).wait())


def _dispatch(pieces, pad_end, lslot, c, n_slots):
    t, d = c.shape
    tm = TM_PROJ
    grid_spec = pltpu.PrefetchScalarGridSpec(
        num_scalar_prefetch=4,
        grid=(t // tm,),
        in_specs=[
            pl.BlockSpec((TOP_K, tm), lambda i, *_: (0, i)),
            pl.BlockSpec((tm, d), lambda i, *_: (i, 0)),
        ],
        out_specs=pl.BlockSpec(memory_space=pl.ANY),
        scratch_shapes=[pltpu.VMEM((LOCAL_SLOTS, tm), BF16),
                        pltpu.VMEM((2, LOCAL_SLOTS, d), F32),
                        pltpu.VMEM((TB_EXPERT, d), F32),
                        pltpu.SemaphoreType.DMA((2,)), pltpu.SemaphoreType.DMA(())],
    )
    return pl.pallas_call(
        _dispatch_kernel,
        grid_spec=grid_spec,
        out_shape=jax.ShapeDtypeStruct((n_slots, d), F32),
        compiler_params=_cparams(("arbitrary",)),
        name="moe_dispatch",
    )(*pieces, pad_end, lslot, c)


def _expert_kernel(be_ref, nb_ref, xs_ref, w1_ref, w2_ref, b1g_ref, b1l_ref, b2_ref, perm_ref,
                   ys_ref, w1g_sc, w1l_sc, w2_sc):
    i = pl.program_id(0)
    active = i < nb_ref[0]
    fresh = jnp.logical_or(i == 0, be_ref[i] != be_ref[jnp.maximum(i - 1, 0)])

    @pl.when(jnp.logical_and(active, fresh))
    def _():
        half = MXU_DIM // 2
        for cb in range(w1_ref.shape[2] // MXU_DIM):
            blk = w1_ref[0, :, cb * MXU_DIM:(cb + 1) * MXU_DIM].astype(BF16)
            de = jnp.dot(blk, perm_ref[...], preferred_element_type=F32)
            w1g_sc[:, cb * half:(cb + 1) * half] = de[:, :half].astype(BF16)
            w1l_sc[:, cb * half:(cb + 1) * half] = de[:, half:].astype(BF16)
        w2_sc[...] = w2_ref[0].astype(BF16)

    @pl.when(active)
    def _():
        xb = xs_ref[...].astype(BF16)
        hg = jnp.dot(xb, w1g_sc[...], preferred_element_type=F32) + b1g_ref[0]
        hl = jnp.dot(xb, w1l_sc[...], preferred_element_type=F32) + b1l_ref[0]
        xg = jnp.minimum(hg, SWIGLU_LIMIT)
        xl = jnp.clip(hl, -SWIGLU_LIMIT, SWIGLU_LIMIT)
        act = xg * _sigmoid(SWIGLU_ALPHA * xg) * (xl + 1.0)
        ys_ref[...] = jnp.dot(act.astype(BF16), w2_sc[...], preferred_element_type=F32) + b2_ref[0]


def _experts(block_e, n_used, xs, w1, w2, b1g, b1l, b2, perm):
    n_slots, d = xs.shape
    f2 = w1.shape[2]
    f = f2 // 2
    tb = TB_EXPERT
    blk = lambda i, be, nb: (jnp.minimum(i, nb[0] - 1), 0)
    wsel = lambda i, be, nb: (be[i], 0, 0)
    grid_spec = pltpu.PrefetchScalarGridSpec(
        num_scalar_prefetch=2,
        grid=(n_slots // tb,),
        in_specs=[
            pl.BlockSpec((tb, d), blk),
            pl.BlockSpec((1, d, f2), wsel),
            pl.BlockSpec((1, f, d), wsel),
            pl.BlockSpec((1, 1, f), wsel),
            pl.BlockSpec((1, 1, f), wsel),
            pl.BlockSpec((1, 1, d), wsel),
            pl.BlockSpec((MXU_DIM, MXU_DIM), lambda i, be, nb: (0, 0)),
        ],
        out_specs=pl.BlockSpec((tb, d), blk),
        scratch_shapes=[pltpu.VMEM((d, f), BF16), pltpu.VMEM((d, f), BF16),
                        pltpu.VMEM((f, d), BF16)],
    )
    return pl.pallas_call(
        _expert_kernel,
        grid_spec=grid_spec,
        out_shape=jax.ShapeDtypeStruct((n_slots, d), F32),
        compiler_params=_cparams(("arbitrary",)),
        name="moe_experts",
    )(block_e, n_used, xs, w1, w2, b1g, b1l, b2, perm)


def _combine_kernel(pn_ref, pl_ref, pg_ref, lslot_ref, gate_ref, h_ref, p_ref, ys_ref,
                    gp_ref, wg_ref, wp_ref, gf_ref, out_ref, loc_sc, locb_sc, comb_sc, sem,
                    *, final):
    w = pl.program_id(0)
    tm = h_ref.shape[0]
    nloc = loc_sc.shape[0]

    tables = (pn_ref, pl_ref, pg_ref)

    def piece_copy(local_row, global_row, rows):
        return pltpu.make_async_copy(ys_ref.at[pl.ds(global_row, rows), :],
                                     loc_sc.at[pl.ds(local_row, rows), :], sem)

    @pl.when(w == 0)
    def _():
        loc_sc[...] = jnp.zeros_like(loc_sc)
        _piece_loops(tables, w, lambda *a: piece_copy(*a).start())

    _piece_loops(tables, w, lambda *a: piece_copy(*a).wait())
    for c0 in range(0, nloc, SLOT_CHUNK):
        locb_sc[c0:c0 + SLOT_CHUNK, :] = loc_sc[c0:c0 + SLOT_CHUNK, :].astype(BF16)

    @pl.when(w + 1 < pl.num_programs(0))
    def _():
        _piece_loops(tables, w + 1, lambda *a: piece_copy(*a).start())

    lcol = [lslot_ref[:, k:k + 1] for k in range(TOP_K)]
    gcol = [gate_ref[:, k:k + 1] for k in range(TOP_K)]
    for c0 in range(0, nloc, SLOT_CHUNK):
        scol = lax.broadcasted_iota(jnp.int32, (tm, SLOT_CHUNK), 1) + c0
        g = jnp.zeros((tm, SLOT_CHUNK), F32)
        for k in range(TOP_K):
            g = jnp.where(scol == lcol[k], gcol[k], g)
        comb_sc[:, c0:c0 + SLOT_CHUNK] = g.astype(BF16)
    h = h_ref[...] + jnp.dot(comb_sc[...], locb_sc[...], preferred_element_type=F32)

    r = _rms(h, gp_ref[...]).astype(BF16)
    pg = _sigmoid(jnp.dot(r, wg_ref[...], preferred_element_type=F32))
    pp = jnp.dot(p_ref[...].astype(BF16), wp_ref[...], preferred_element_type=F32)
    h = h + pg * pp
    out_ref[...] = _rms(h, gf_ref[...]) if final else h


def _combine(pieces, lslot_tk, gate_tk, h1, p2, ys, g_ple, w_ple_gate, w_ple_proj, g_final,
             final):
    t, d = h1.shape
    pd = p2.shape[1]
    tm = TM_PROJ
    const = lambda *shape: pl.BlockSpec(shape, lambda i, *_: (0,) * len(shape))
    row = lambda width: pl.BlockSpec((tm, width), lambda i, *_: (i, 0))
    grid_spec = pltpu.PrefetchScalarGridSpec(
        num_scalar_prefetch=3,
        grid=(t // tm,),
        in_specs=[row(TOP_K), row(TOP_K), row(d), row(pd),
                  pl.BlockSpec(memory_space=pl.ANY),
                  const(1, d), const(d, d), const(pd, d), const(1, d)],
        out_specs=row(d),
        scratch_shapes=[pltpu.VMEM((LOCAL_SLOTS, d), F32), pltpu.VMEM((LOCAL_SLOTS, d), BF16),
                        pltpu.VMEM((tm, LOCAL_SLOTS), BF16), pltpu.SemaphoreType.DMA(())],
    )
    return pl.pallas_call(
        functools.partial(_combine_kernel, final=final),
        grid_spec=grid_spec,
        out_shape=jax.ShapeDtypeStruct((t, d), F32),
        compiler_params=_cparams(("arbitrary",)),
        name="combine_ple",
    )(*pieces, lslot_tk, gate_tk, h1, p2, ys, g_ple, w_ple_gate, w_ple_proj, g_final)


def kernel(x, p, g_mix, w_in, w_out_a, w_out_b, w_out, sgu_norm, sgu_w, sgu_b, g_moe,
           w_router, b_router, w1, b1, w2, b2, g_ple, w_ple_gate, w_ple_proj, g_final):
    b, s, d = x.shape
    depth = w_in.shape[0]
    t = b * s
    assert s % (TQ_ATTN * ATTN_TILES_PER_STEP) == 0 and t % TM_PROJ == 0

    kk = jnp.arange(KC_ATTN)
    ntri = -(kk[:, None] >= kk[None, :]).astype(BF16)
    tt = jnp.arange(TM_PROJ)
    upper = (tt[:, None] < tt[None, :]).astype(BF16)
    rr = jnp.arange(MXU_DIM)[:, None]
    cc = jnp.arange(MXU_DIM)[None, :]
    half = MXU_DIM // 2
    perm = (rr == jnp.where(cc < half, 2 * cc, 2 * (cc - half) + 1)).astype(BF16)

    tb = TB_EXPERT
    n_win = t // TM_PROJ
    n_blocks = -(-(t * TOP_K + n_win * N_EXPERTS * (ROW_ALIGN - 1) + N_EXPERTS * (tb - 1)) // tb)
    n_slots = n_blocks * tb

    h = x.reshape(t, d)
    for i in range(depth):
        sgu_bias = jnp.repeat(sgu_b[i].T, SGU_GROUP_DIM, axis=1)
        qkv, o_b, sga, sgb = _inproj(h, g_mix[i][None], w_in[i].astype(BF16), sgu_w[i],
                                     sgu_bias, sgu_norm[i][None])
        o_a = _attention(qkv.reshape(b, s, 3 * SB_WIDTH), ntri).reshape(t, SB_WIDTH)
        h1, c, lslot, gate, cnt = _outproj(
            o_a, o_b, sga, sgb, h, w_out_a[i].astype(BF16), w_out_b[i].astype(BF16),
            w_out[i].astype(BF16), g_moe[i][None], w_router[i].T, b_router[i][:, None], upper)

        counts = cnt[:, :, 0].astype(jnp.int32)
        run = ((counts + ROW_ALIGN - 1) // ROW_ALIGN) * ROW_ALIGN
        loff = jnp.cumsum(run, axis=1) - run
        region = ((jnp.sum(run, axis=0) + tb - 1) // tb) * tb
        pad_end = jnp.cumsum(region).astype(jnp.int32)
        base = (pad_end - region)[None, :] + jnp.cumsum(run, axis=0) - run
        pieces = _piece_tables(run, loff, base)
        starts = jnp.arange(n_blocks, dtype=jnp.int32) * tb
        block_e = jnp.minimum(jnp.sum(starts[:, None] >= pad_end[None, :], axis=1),
                              N_EXPERTS - 1).astype(jnp.int32)
        n_used = (pad_end[-1:] // tb).astype(jnp.int32)

        xs = _dispatch(pieces, pad_end, lslot, c, n_slots)
        ys = _experts(block_e, n_used, xs, w1[i], w2[i],
                      b1[i][:, None, 0::2], b1[i][:, None, 1::2], b2[i][:, None, :], perm)
        h = _combine(pieces, lslot.T, gate.T, h1, p[i].reshape(t, -1), ys, g_ple[i][None],
                     w_ple_gate[i].astype(BF16), w_ple_proj[i].astype(BF16), g_final[None],
                     final=(i == depth - 1))
    return h.reshape(b, s, d)
```

```python
import functools
import math

import jax
import jax.numpy as jnp
from jax import lax
from jax.experimental import pallas as pl
from jax.experimental.pallas import tpu as pltpu

F32 = jnp.float32
BF16 = jnp.bfloat16

EPS = 1e-6
CHUNK = 64
SB_HEADS = 8
SB_HEAD_DIM = 64
SB_WIDTH = SB_HEADS * SB_HEAD_DIM
SGU_GROUPS = 8
SGU_WIDTH = 512
SGU_GROUP_DIM = SGU_WIDTH // SGU_GROUPS
SGU_BLOCK = 128
N_EXPERTS = 32
TOP_K = 4
SWIGLU_ALPHA = 1.702
SWIGLU_LIMIT = 7.0
LOG2E = 1.4426950408889634
UNDERFLOW_LOG2 = -160.0

LANES = 128
MXU_DIM = 256
VMEM_LIMIT_BYTES = 56 * 1024 * 1024

TM_PROJ = 512
TQ_ATTN = 512
ATTN_TILES_PER_STEP = 2
KC_ATTN = MXU_DIM
TB_EXPERT = 512
ROW_ALIGN = 8
PIECE_ROWS = (32, ROW_ALIGN)
SLOT_CHUNK = 256
LOCAL_SLOTS = -(-(TM_PROJ * TOP_K + N_EXPERTS * (ROW_ALIGN - 1)) // SLOT_CHUNK) * SLOT_CHUNK
MAX_PIECES = max(LOCAL_SLOTS // PIECE_ROWS[0],
                 N_EXPERTS * (PIECE_ROWS[0] // PIECE_ROWS[1] - 1))


def _cparams(sem):
    return pltpu.CompilerParams(dimension_semantics=sem,
                                vmem_limit_bytes=VMEM_LIMIT_BYTES)


def _rms(x, g):
    ms = jnp.mean(x * x, axis=-1, keepdims=True)
    return x * lax.rsqrt(ms + EPS) * g


def _sigmoid(x):
    return 0.5 * jnp.tanh(0.5 * x) + 0.5


def _inproj_kernel(x_ref, g_ref, w_ref, sw_ref, sbias_ref, sn_ref,
                   qkv_ref, ob_ref, sga_ref, sgb_ref):
    tm = x_ref.shape[0]
    a = _rms(x_ref[...], g_ref[...]).astype(BF16)

    def proj(c0, width):
        return jnp.dot(a, w_ref[:, c0:c0 + width], preferred_element_type=F32)

    qkv_ref[:, 0:SB_WIDTH] = (proj(0, SB_WIDTH) * (LOG2E / math.sqrt(SB_HEAD_DIM))).astype(BF16)
    qkv_ref[:, SB_WIDTH:2 * SB_WIDTH] = proj(SB_WIDTH, SB_WIDTH).astype(BF16)
    qkv_ref[:, 2 * SB_WIDTH:3 * SB_WIDTH] = proj(2 * SB_WIDTH, SB_WIDTH).astype(BF16)

    c_u = 3 * SB_WIDTH
    gu = jax.nn.gelu(proj(c_u, SGU_WIDTH))
    gv = jax.nn.gelu(proj(c_u + SGU_WIDTH, SGU_WIDTH))
    mu = jnp.mean(gv, axis=-1, keepdims=True)
    d = gv - mu
    var = jnp.mean(d * d, axis=-1, keepdims=True)
    vn = (d * lax.rsqrt(var + EPS) * sn_ref[...]).astype(BF16)

    rr = lax.broadcasted_iota(jnp.int32, (SGU_BLOCK, SGU_BLOCK), 0) // CHUNK
    cc = lax.broadcasted_iota(jnp.int32, (SGU_BLOCK, SGU_BLOCK), 1) // CHUNK
    cmask = cc <= rr
    lane = lax.broadcasted_iota(jnp.int32, (SGU_BLOCK, LANES), 1)
    lo_mask = lane < SGU_GROUP_DIM
    wcat = []
    for j in range(SGU_GROUPS // 2):
        w0 = jnp.where(cmask, sw_ref[2 * j], 0.0).astype(BF16)
        w1 = jnp.where(cmask, sw_ref[2 * j + 1], 0.0).astype(BF16)
        wcat.append(jnp.concatenate([w0, w1], axis=1))

    zero = jnp.zeros((), BF16)
    for blk in range(tm // SGU_BLOCK):
        r0 = blk * SGU_BLOCK
        for j in range(SGU_GROUPS // 2):
            vp = vn[r0:r0 + SGU_BLOCK, j * LANES:(j + 1) * LANES]
            rhs = jnp.concatenate([jnp.where(lo_mask, vp, zero),
                                   jnp.where(lo_mask, zero, vp)], axis=0)
            sv = jnp.dot(wcat[j], rhs, preferred_element_type=F32)
            sv = sv + sbias_ref[:, j * LANES:(j + 1) * LANES]
            ob_ref[r0:r0 + SGU_BLOCK, j * LANES:(j + 1) * LANES] = (
                gu[r0:r0 + SGU_BLOCK, j * LANES:(j + 1) * LANES] * sv).astype(BF16)

    c_g = c_u + 2 * SGU_WIDTH
    d_model = sga_ref.shape[1]
    for c0 in range(0, d_model, 512):
        sga_ref[:, c0:c0 + 512] = _sigmoid(proj(c_g + c0, 512)).astype(BF16)
        sgb_ref[:, c0:c0 + 512] = _sigmoid(proj(c_g + d_model + c0, 512)).astype(BF16)


def _inproj(x2, g_mix, w_in, sgu_w, sgu_bias, sgu_norm):
    t, d = x2.shape
    ncol = w_in.shape[1]
    tm = TM_PROJ
    const = lambda *shape: pl.BlockSpec(shape, lambda i: (0,) * len(shape))
    return pl.pallas_call(
        _inproj_kernel,
        grid=(t // tm,),
        in_specs=[
            pl.BlockSpec((tm, d), lambda i: (i, 0)),
            const(1, d),
            const(d, ncol),
            const(SGU_GROUPS, SGU_BLOCK, SGU_BLOCK),
            const(SGU_BLOCK, SGU_WIDTH),
            const(1, SGU_WIDTH),
        ],
        out_specs=[
            pl.BlockSpec((tm, 3 * SB_WIDTH), lambda i: (i, 0)),
            pl.BlockSpec((tm, SGU_WIDTH), lambda i: (i, 0)),
            pl.BlockSpec((tm, d), lambda i: (i, 0)),
            pl.BlockSpec((tm, d), lambda i: (i, 0)),
        ],
        out_shape=[
            jax.ShapeDtypeStruct((t, 3 * SB_WIDTH), BF16),
            jax.ShapeDtypeStruct((t, SGU_WIDTH), BF16),
            jax.ShapeDtypeStruct((t, d), BF16),
            jax.ShapeDtypeStruct((t, d), BF16),
        ],
        compiler_params=_cparams(("arbitrary",)),
        name="inproj_sgu",
    )(x2, g_mix, w_in, sgu_w, sgu_bias, sgu_norm)


def _attn_kernel(q_ref, k_ref, v_ref, ntri_ref, o_ref, carry_sc, acc_sc):
    tq = TQ_ATTN
    kc = KC_ATTN
    lane = lax.broadcasted_iota(jnp.int32, (tq, LANES), 1)
    head0 = lane < SB_HEAD_DIM
    zero = jnp.zeros((), BF16)
    tri = (lax.broadcasted_iota(jnp.int32, (kc, kc), 1)
           < lax.broadcasted_iota(jnp.int32, (kc, kc), 0))

    def tile(sub, _):
        qi = pl.program_id(2) * (q_ref.shape[1] // tq) + sub
        r0 = pl.multiple_of(sub * tq, tq)
        q = q_ref[0, pl.ds(r0, tq), :]
        qh = (jnp.where(head0, q, zero), jnp.where(head0, zero, q))
        carry_sc[...] = jnp.zeros_like(carry_sc)
        acc_sc[...] = jnp.zeros_like(acc_sc)

        def scores(j, lo, hi, causal, keep):
            k0 = pl.multiple_of(j * kc, kc)
            kj = k_ref[0, pl.ds(k0, kc), :]
            out = []
            for h in range(2):
                z = lax.dot_general(qh[h][lo:hi], kj, (((1,), (1,)), ((), ())),
                                    preferred_element_type=F32)
                sp = jnp.maximum(z, 0.0) + jnp.log2(1.0 + jnp.exp2(-jnp.abs(z)))
                if causal:
                    sp = jnp.where(tri, sp, 0.0)
                if keep is not None:
                    sp = jnp.where(keep, sp, 0.0)
                out.append((z, jnp.dot(sp.astype(BF16), ntri_ref[...], preferred_element_type=F32)))
            return k0, lo, hi, causal, keep, out

        def weigh(part):
            k0, lo, hi, causal, keep, out = part
            vj = v_ref[0, pl.ds(k0, kc), :]
            for h in range(2):
                z, cum = out[h]
                carry = carry_sc[h, lo:hi]
                w = jnp.exp2(z + cum + jnp.concatenate([carry] * (kc // LANES), axis=1))
                carry_sc[h, lo:hi] = carry + jnp.broadcast_to(cum[:, 0:1], carry.shape)
                if causal:
                    w = jnp.where(tri, w, 0.0)
                if keep is not None:
                    w = jnp.where(keep, w, 0.0)
                acc_sc[h, lo:hi] += jnp.dot(w.astype(BF16), vj, preferred_element_type=F32)

        ndiag = tq // kc
        npast = qi * ndiag
        head = []
        for c in reversed(range(ndiag)):
            head.append(scores(npast + c, c * kc, (c + 1) * kc, True, None))
            if c + 1 < ndiag:
                head.append(scores(npast + c, (c + 1) * kc, tq, False, None))
        head.append(scores(jnp.maximum(npast - 1, 0), 0, tq, False, npast > 0))
        for part in head:
            weigh(part)

        def live(t):
            return jnp.logical_and(t < npast, jnp.max(carry_sc[...]) > UNDERFLOW_LOG2)

        def past(t):
            weigh(scores(npast - 1 - t, 0, tq, False, None))
            return t + 1

        lax.while_loop(live, past, 1)
        o_ref[0, pl.ds(r0, tq), :] = jnp.where(head0, acc_sc[0], acc_sc[1]).astype(o_ref.dtype)
        return 0

    lax.fori_loop(0, q_ref.shape[1] // tq, tile, 0)


def _attention(qkv3, ntri):
    b, s, _ = qkv3.shape
    tq = TQ_ATTN * ATTN_TILES_PER_STEP
    npair = SB_WIDTH // LANES
    return pl.pallas_call(
        _attn_kernel,
        grid=(b, npair, s // tq),
        in_specs=[
            pl.BlockSpec((1, tq, LANES), lambda bi, p, i: (bi, i, p)),
            pl.BlockSpec((1, s, LANES), lambda bi, p, i: (bi, 0, npair + p)),
            pl.BlockSpec((1, s, LANES), lambda bi, p, i: (bi, 0, 2 * npair + p)),
            pl.BlockSpec((KC_ATTN, KC_ATTN), lambda bi, p, i: (0, 0)),
        ],
        out_specs=pl.BlockSpec((1, tq, LANES), lambda bi, p, i: (bi, i, p)),
        out_shape=jax.ShapeDtypeStruct((b, s, SB_WIDTH), BF16),
        scratch_shapes=[pltpu.VMEM((2, TQ_ATTN, LANES), F32),
                        pltpu.VMEM((2, TQ_ATTN, LANES), F32)],
        compiler_params=_cparams(("arbitrary", "arbitrary", "arbitrary")),
        name="stickbreak_attn",
    )(qkv3, qkv3, qkv3, ntri)


def _outproj_kernel(oa_ref, ob_ref, sga_ref, sgb_ref, x_ref, woa_ref, wob_ref, wo_ref,
                    g_ref, wr_ref, br_ref, upper_ref,
                    h_ref, c_ref, lslot_ref, gate_ref, cnt_ref):
    tm = x_ref.shape[0]
    ma = jnp.dot(oa_ref[...], woa_ref[...], preferred_element_type=F32)
    mb = jnp.dot(ob_ref[...], wob_ref[...], preferred_element_type=F32)
    merged = sga_ref[...].astype(F32) * ma + sgb_ref[...].astype(F32) * mb
    h = x_ref[...] + jnp.dot(merged.astype(BF16), wo_ref[...], preferred_element_type=F32)
    h_ref[...] = h
    c = _rms(h, g_ref[...])
    c_hi = c.astype(BF16)
    c_ref[...] = c_hi

    c_lo = (c - c_hi.astype(F32)).astype(BF16)
    wr = wr_ref[...]
    wr_hi = wr.astype(BF16)
    wr_lo = (wr - wr_hi.astype(F32)).astype(BF16)
    nt = (((1,), (1,)), ((), ()))
    logits = (lax.dot_general(wr_hi, c_hi, nt, preferred_element_type=F32)
              + lax.dot_general(wr_hi, c_lo, nt, preferred_element_type=F32)
              + lax.dot_general(wr_lo, c_hi, nt, preferred_element_type=F32)) + br_ref[...]
    eid = lax.broadcasted_iota(jnp.int32, (N_EXPERTS, tm), 0).astype(F32)
    work = logits
    vals, sels = [], []
    for _ in range(TOP_K):
        m = jnp.max(work, axis=0, keepdims=True)
        ik = jnp.min(jnp.where(work == m, eid, float(N_EXPERTS)), axis=0, keepdims=True)
        sel = eid == ik
        vals.append(m)
        sels.append(sel)
        work = jnp.where(sel, -jnp.inf, work)
    es = [jnp.exp(v - vals[0]) for v in vals]
    inv = 1.0 / (es[0] + es[1] + es[2] + es[3])
    onehot = jnp.zeros((N_EXPERTS, tm), F32)
    for sel in sels:
        onehot = onehot + jnp.where(sel, 1.0, 0.0)

    prefix = jnp.dot(onehot.astype(BF16), upper_ref[...], preferred_element_type=F32)
    n = jnp.sum(onehot, axis=1, keepdims=True)
    n_al = jnp.ceil(n * (1.0 / ROW_ALIGN)) * ROW_ALIGN
    er = lax.broadcasted_iota(jnp.int32, (N_EXPERTS, N_EXPERTS), 0)
    ec = lax.broadcasted_iota(jnp.int32, (N_EXPERTS, N_EXPERTS), 1)
    run_off = jnp.dot(jnp.where(ec < er, 1.0, 0.0), jnp.broadcast_to(n_al, (N_EXPERTS, LANES)),
                      precision=lax.Precision.HIGHEST, preferred_element_type=F32)
    slot = prefix + run_off[:, 0:1]
    for k in range(TOP_K):
        gate_ref[k:k + 1, :] = es[k] * inv
        lslot_ref[k:k + 1, :] = jnp.sum(jnp.where(sels[k], slot, 0.0), axis=0,
                                        keepdims=True).astype(jnp.int32)
    cnt_ref[0] = jnp.broadcast_to(n, (N_EXPERTS, LANES))


def _outproj(o_a, o_b, sga, sgb, x2, w_out_a, w_out_b, w_out, g_moe, w_router_t, b_router, upper):
    t, d = x2.shape
    tm = TM_PROJ
    const = lambda *shape: pl.BlockSpec(shape, lambda i: (0,) * len(shape))
    row = lambda w: pl.BlockSpec((tm, w), lambda i: (i, 0))
    colk = pl.BlockSpec((TOP_K, tm), lambda i: (0, i))
    return pl.pallas_call(
        _outproj_kernel,
        grid=(t // tm,),
        in_specs=[row(SB_WIDTH), row(SGU_WIDTH), row(d), row(d), row(d),
                  const(SB_WIDTH, d), const(SGU_WIDTH, d), const(d, d),
                  const(1, d), const(N_EXPERTS, d), const(N_EXPERTS, 1), const(tm, tm)],
        out_specs=[row(d), row(d), colk, colk,
                   pl.BlockSpec((1, N_EXPERTS, LANES), lambda i: (i, 0, 0))],
        out_shape=[
            jax.ShapeDtypeStruct((t, d), F32),
            jax.ShapeDtypeStruct((t, d), BF16),
            jax.ShapeDtypeStruct((TOP_K, t), jnp.int32),
            jax.ShapeDtypeStruct((TOP_K, t), F32),
            jax.ShapeDtypeStruct((t // tm, N_EXPERTS, LANES), F32),
        ],
        compiler_params=_cparams(("arbitrary",)),
        name="outproj_router",
    )(o_a, o_b, sga, sgb, x2, w_out_a, w_out_b, w_out, g_moe, w_router_t, b_router, upper)


def _piece_loops(tables, win, fn):
    n_ref, local_ref, global_ref = tables
    for cls, rows in enumerate(PIECE_ROWS):
        head = win * len(PIECE_ROWS) + cls

        def body(i, _, off=head * MAX_PIECES, rows=rows):
            fn(pl.multiple_of(local_ref[off + i], ROW_ALIGN),
               pl.multiple_of(global_ref[off + i], ROW_ALIGN), rows)
            return 0

        lax.fori_loop(0, n_ref[head], body, 0)


def _piece_tables(run, loff, base):
    big = PIECE_ROWS[0]
    nbig = run // big
    nsmall = (run - nbig * big) // ROW_ALIGN
    i = jnp.arange(MAX_PIECES, dtype=jnp.int32)
    experts = jnp.arange(N_EXPERTS, dtype=jnp.int32)

    def flat(cnt, rows, skip):
        cum = jnp.cumsum(cnt, axis=1)
        owner = jnp.minimum(jnp.sum(i[None, :, None] >= cum[:, None, :], axis=-1), N_EXPERTS - 1)
        onehot = owner[:, :, None] == experts
        pick = lambda tab: jnp.sum(jnp.where(onehot, tab[:, None, :], 0), axis=-1)
        j = i[None, :] - pick(cum - cnt)
        return cum[:, -1], pick(loff + skip) + j * rows, pick(base + skip) + j * rows

    nb, lb, gb = flat(nbig, big, 0)
    ns, lsm, gsm = flat(nsmall, ROW_ALIGN, nbig * big)
    pack = lambda a, b: jnp.stack([a, b], axis=1).reshape(-1).astype(jnp.int32)
    return pack(nb, ns), pack(lb, lsm), pack(gb, gsm)


def _dispatch_kernel(pn_ref, pl_ref, pg_ref, pad_end_ref, lslot_ref, c_ref, xs_ref,
                     loc_sc, zero_sc, sem, zsem):
    w = pl.program_id(0)
    tm = c_ref.shape[0]
    tb = zero_sc.shape[0]
    nloc = loc_sc.shape[1]

    @pl.when(w == 0)
    def _():
        zero_sc[...] = jnp.zeros_like(zero_sc)

        def pad_copy(e):
            return pltpu.make_async_copy(
                zero_sc, xs_ref.at[pl.ds(pl.multiple_of(pad_end_ref[e] - tb, tb), tb), :], zsem)

        def nonempty(e):
            prev = jnp.where(e > 0, pad_end_ref[jnp.maximum(e - 1, 0)], 0)
            return pad_end_ref[e] > prev

        def zstart(e, _):
            @pl.when(nonempty(e))
            def _():
                pad_copy(e).start()
            return 0

        def zwait(e, _):
            @pl.when(nonempty(e))
            def _():
                pad_copy(e).wait()
            return 0

        lax.fori_loop(0, N_EXPERTS, zstart, 0)
        lax.fori_loop(0, N_EXPERTS, zwait, 0)

    ls = [lslot_ref[k:k + 1, :] for k in range(TOP_K)]
    buf = lax.bitwise_and(w, 1)
    for r0 in range(0, nloc, SLOT_CHUNK):
        srow = lax.broadcasted_iota(jnp.int32, (SLOT_CHUNK, tm), 0) + r0
        sel = jnp.zeros((SLOT_CHUNK, tm), F32)
        for k in range(TOP_K):
            sel = jnp.where(srow == ls[k], 1.0, sel)
        loc_sc[buf, r0:r0 + SLOT_CHUNK, :] = jnp.dot(sel.astype(BF16), c_ref[...],
                                                     preferred_element_type=F32)

    tables = (pn_ref, pl_ref, pg_ref)

    def piece_copy(slot):
        def build(local_row, global_row, rows):
            return pltpu.make_async_copy(loc_sc.at[slot, pl.ds(local_row, rows), :],
                                         xs_ref.at[pl.ds(global_row, rows), :], sem.at[slot])
        return build

    _piece_loops(tables, w, lambda *a: piece_copy(buf)(*a).start())

    @pl.when(w > 0)
    def _():
        _piece_loops(tables, w - 1, lambda *a: piece_copy(1 - buf)(*a).wait())

    @pl.when(w == pl.num_programs(0) - 1)
    def _():
        _piece_loops(tables, w, lambda *a: piece_copy(buf)(*a).wait())


def _dispatch(pieces, pad_end, lslot, c, n_slots):
    t, d = c.shape
    tm = TM_PROJ
    grid_spec = pltpu.PrefetchScalarGridSpec(
        num_scalar_prefetch=4,
        grid=(t // tm,),
        in_specs=[
            pl.BlockSpec((TOP_K, tm), lambda i, *_: (0, i)),
            pl.BlockSpec((tm, d), lambda i, *_: (i, 0)),
        ],
        out_specs=pl.BlockSpec(memory_space=pl.ANY),
        scratch_shapes=[pltpu.VMEM((2, LOCAL_SLOTS, d), F32),
                        pltpu.VMEM((TB_EXPERT, d), F32),
                        pltpu.SemaphoreType.DMA((2,)), pltpu.SemaphoreType.DMA(())],
    )
    return pl.pallas_call(
        _dispatch_kernel,
        grid_spec=grid_spec,
        out_shape=jax.ShapeDtypeStruct((n_slots, d), F32),
        compiler_params=_cparams(("arbitrary",)),
        name="moe_dispatch",
    )(*pieces, pad_end, lslot, c)


def _expert_kernel(be_ref, nb_ref, xs_ref, w1_ref, w2_ref, b1g_ref, b1l_ref, b2_ref, perm_ref,
                   ys_ref, w1g_sc, w1l_sc, w2_sc):
    i = pl.program_id(0)
    active = i < nb_ref[0]
    fresh = jnp.logical_or(i == 0, be_ref[i] != be_ref[jnp.maximum(i - 1, 0)])

    @pl.when(jnp.logical_and(active, fresh))
    def _():
        half = MXU_DIM // 2
        for cb in range(w1_ref.shape[2] // MXU_DIM):
            blk = w1_ref[0, :, cb * MXU_DIM:(cb + 1) * MXU_DIM].astype(BF16)
            de = jnp.dot(blk, perm_ref[...], preferred_element_type=F32)
            w1g_sc[:, cb * half:(cb + 1) * half] = de[:, :half].astype(BF16)
            w1l_sc[:, cb * half:(cb + 1) * half] = de[:, half:].astype(BF16)
        w2_sc[...] = w2_ref[0].astype(BF16)

    @pl.when(active)
    def _():
        xb = xs_ref[...].astype(BF16)
        hg = jnp.dot(xb, w1g_sc[...], preferred_element_type=F32) + b1g_ref[0]
        hl = jnp.dot(xb, w1l_sc[...], preferred_element_type=F32) + b1l_ref[0]
        xg = jnp.minimum(hg, SWIGLU_LIMIT)
        xl = jnp.clip(hl, -SWIGLU_LIMIT, SWIGLU_LIMIT)
        act = xg * _sigmoid(SWIGLU_ALPHA * xg) * (xl + 1.0)
        ys_ref[...] = jnp.dot(act.astype(BF16), w2_sc[...], preferred_element_type=F32) + b2_ref[0]


def _experts(block_e, n_used, xs, w1, w2, b1g, b1l, b2, perm):
    n_slots, d = xs.shape
    f2 = w1.shape[2]
    f = f2 // 2
    tb = TB_EXPERT
    blk = lambda i, be, nb: (jnp.minimum(i, nb[0] - 1), 0)
    wsel = lambda i, be, nb: (be[i], 0, 0)
    grid_spec = pltpu.PrefetchScalarGridSpec(
        num_scalar_prefetch=2,
        grid=(n_slots // tb,),
        in_specs=[
            pl.BlockSpec((tb, d), blk),
            pl.BlockSpec((1, d, f2), wsel),
            pl.BlockSpec((1, f, d), wsel),
            pl.BlockSpec((1, 1, f), wsel),
            pl.BlockSpec((1, 1, f), wsel),
            pl.BlockSpec((1, 1, d), wsel),
            pl.BlockSpec((MXU_DIM, MXU_DIM), lambda i, be, nb: (0, 0)),
        ],
        out_specs=pl.BlockSpec((tb, d), blk),
        scratch_shapes=[pltpu.VMEM((d, f), BF16), pltpu.VMEM((d, f), BF16),
                        pltpu.VMEM((f, d), BF16)],
    )
    return pl.pallas_call(
        _expert_kernel,
        grid_spec=grid_spec,
        out_shape=jax.ShapeDtypeStruct((n_slots, d), F32),
        compiler_params=_cparams(("arbitrary",)),
        name="moe_experts",
    )(block_e, n_used, xs, w1, w2, b1g, b1l, b2, perm)


def _combine_kernel(pn_ref, pl_ref, pg_ref, lslot_ref, gate_ref, h_ref, p_ref, ys_ref,
                    gp_ref, wg_ref, wp_ref, gf_ref, out_ref, loc_sc, locb_sc, sem,
                    *, final):
    w = pl.program_id(0)
    tm = h_ref.shape[0]
    nloc = loc_sc.shape[0]

    tables = (pn_ref, pl_ref, pg_ref)

    def piece_copy(local_row, global_row, rows):
        return pltpu.make_async_copy(ys_ref.at[pl.ds(global_row, rows), :],
                                     loc_sc.at[pl.ds(local_row, rows), :], sem)

    @pl.when(w == 0)
    def _():
        loc_sc[...] = jnp.zeros_like(loc_sc)
        _piece_loops(tables, w, lambda *a: piece_copy(*a).start())

    _piece_loops(tables, w, lambda *a: piece_copy(*a).wait())
    for c0 in range(0, nloc, SLOT_CHUNK):
        locb_sc[c0:c0 + SLOT_CHUNK, :] = loc_sc[c0:c0 + SLOT_CHUNK, :].astype(BF16)

    @pl.when(w + 1 < pl.num_programs(0))
    def _():
        _piece_loops(tables, w + 1, lambda *a: piece_copy(*a).start())

    lcol = [lslot_ref[:, k:k + 1] for k in range(TOP_K)]
    gcol = [gate_ref[:, k:k + 1] for k in range(TOP_K)]
    h = h_ref[...]
    for c0 in range(0, nloc, SLOT_CHUNK):
        scol = lax.broadcasted_iota(jnp.int32, (tm, SLOT_CHUNK), 1) + c0
        g = jnp.zeros((tm, SLOT_CHUNK), F32)
        for k in range(TOP_K):
            g = jnp.where(scol == lcol[k], gcol[k], g)
        h = h + jnp.dot(g.astype(BF16), locb_sc[c0:c0 + SLOT_CHUNK, :],
                        preferred_element_type=F32)

    r = _rms(h, gp_ref[...]).astype(BF16)
    pg = _sigmoid(jnp.dot(r, wg_ref[...], preferred_element_type=F32))
    pp = jnp.dot(p_ref[...].astype(BF16), wp_ref[...], preferred_element_type=F32)
    h = h + pg * pp
    out_ref[...] = _rms(h, gf_ref[...]) if final else h


def _combine(pieces, lslot_tk, gate_tk, h1, p2, ys, g_ple, w_ple_gate, w_ple_proj, g_final,
             final):
    t, d = h1.shape
    pd = p2.shape[1]
    tm = TM_PROJ
    const = lambda *shape: pl.BlockSpec(shape, lambda i, *_: (0,) * len(shape))
    row = lambda width: pl.BlockSpec((tm, width), lambda i, *_: (i, 0))
    grid_spec = pltpu.PrefetchScalarGridSpec(
        num_scalar_prefetch=3,
        grid=(t // tm,),
        in_specs=[row(TOP_K), row(TOP_K), row(d), row(pd),
                  pl.BlockSpec(memory_space=pl.ANY),
                  const(1, d), const(d, d), const(pd, d), const(1, d)],
        out_specs=row(d),
        scratch_shapes=[pltpu.VMEM((LOCAL_SLOTS, d), F32), pltpu.VMEM((LOCAL_SLOTS, d), BF16),
                        pltpu.SemaphoreType.DMA(())],
    )
    return pl.pallas_call(
        functools.partial(_combine_kernel, final=final),
        grid_spec=grid_spec,
        out_shape=jax.ShapeDtypeStruct((t, d), F32),
        compiler_params=_cparams(("arbitrary",)),
        name="combine_ple",
    )(*pieces, lslot_tk, gate_tk, h1, p2, ys, g_ple, w_ple_gate, w_ple_proj, g_final)


def kernel(x, p, g_mix, w_in, w_out_a, w_out_b, w_out, sgu_norm, sgu_w, sgu_b, g_moe,
           w_router, b_router, w1, b1, w2, b2, g_ple, w_ple_gate, w_ple_proj, g_final):
    b, s, d = x.shape
    depth = w_in.shape[0]
    t = b * s
    assert s % (TQ_ATTN * ATTN_TILES_PER_STEP) == 0 and t % TM_PROJ == 0

    kk = jnp.arange(KC_ATTN)
    ntri = -(kk[:, None] >= kk[None, :]).astype(BF16)
    tt = jnp.arange(TM_PROJ)
    upper = (tt[:, None] < tt[None, :]).astype(BF16)
    rr = jnp.arange(MXU_DIM)[:, None]
    cc = jnp.arange(MXU_DIM)[None, :]
    half = MXU_DIM // 2
    perm = (rr == jnp.where(cc < half, 2 * cc, 2 * (cc - half) + 1)).astype(BF16)

    tb = TB_EXPERT
    n_win = t // TM_PROJ
    n_blocks = -(-(t * TOP_K + n_win * N_EXPERTS * (ROW_ALIGN - 1) + N_EXPERTS * (tb - 1)) // tb)
    n_slots = n_blocks * tb

    h = x.reshape(t, d)
    for i in range(depth):
        sgu_bias = jnp.repeat(sgu_b[i].T, SGU_GROUP_DIM, axis=1)
        qkv, o_b, sga, sgb = _inproj(h, g_mix[i][None], w_in[i].astype(BF16), sgu_w[i],
                                     sgu_bias, sgu_norm[i][None])
        o_a = _attention(qkv.reshape(b, s, 3 * SB_WIDTH), ntri).reshape(t, SB_WIDTH)
        h1, c, lslot, gate, cnt = _outproj(
            o_a, o_b, sga, sgb, h, w_out_a[i].astype(BF16), w_out_b[i].astype(BF16),
            w_out[i].astype(BF16), g_moe[i][None], w_router[i].T, b_router[i][:, None], upper)

        counts = cnt[:, :, 0].astype(jnp.int32)
        run = ((counts + ROW_ALIGN - 1) // ROW_ALIGN) * ROW_ALIGN
        loff = jnp.cumsum(run, axis=1) - run
        region = ((jnp.sum(run, axis=0) + tb - 1) // tb) * tb
        pad_end = jnp.cumsum(region).astype(jnp.int32)
        base = (pad_end - region)[None, :] + jnp.cumsum(run, axis=0) - run
        pieces = _piece_tables(run, loff, base)
        starts = jnp.arange(n_blocks, dtype=jnp.int32) * tb
        block_e = jnp.minimum(jnp.sum(starts[:, None] >= pad_end[None, :], axis=1),
                              N_EXPERTS - 1).astype(jnp.int32)
        n_used = (pad_end[-1:] // tb).astype(jnp.int32)

        xs = _dispatch(pieces, pad_end, lslot, c, n_slots)
        ys = _experts(block_e, n_used, xs, w1[i], w2[i],
                      b1[i][:, None, 0::2], b1[i][:, None, 1::2], b2[i][:, None, :], perm)
        h = _combine(pieces, lslot.T, gate.T, h1, p[i].reshape(t, -1), ys, g_ple[i][None],
                     w_ple_gate[i].astype(BF16), w_ple_proj[i].astype(BF16), g_final[None],
                     final=(i == depth - 1))
    return h.reshape(b, s, d)
```

```python
import functools
import math

import jax
import jax.numpy as jnp
from jax import lax
from jax.experimental import pallas as pl
from jax.experimental.pallas import tpu as pltpu

F32 = jnp.float32
BF16 = jnp.bfloat16

EPS = 1e-6
CHUNK = 64
SB_HEADS = 8
SB_HEAD_DIM = 64
SB_WIDTH = SB_HEADS * SB_HEAD_DIM
SGU_GROUPS = 8
SGU_WIDTH = 512
SGU_GROUP_DIM = SGU_WIDTH // SGU_GROUPS
SGU_BLOCK = 128
N_EXPERTS = 32
TOP_K = 4
SWIGLU_ALPHA = 1.702
SWIGLU_LIMIT = 7.0
LOG2E = 1.4426950408889634
UNDERFLOW_LOG2 = -160.0

LANES = 128
MXU_DIM = 256
VMEM_LIMIT_BYTES = 56 * 1024 * 1024

TM_PROJ = 512
TQ_ATTN = 512
ATTN_TILES_PER_STEP = 2
KC_ATTN = MXU_DIM
TB_EXPERT = 512
ROW_ALIGN = 8
PIECE_ROWS = (32, ROW_ALIGN)
SLOT_CHUNK = 256
LOCAL_SLOTS = -(-(TM_PROJ * TOP_K + N_EXPERTS * (ROW_ALIGN - 1)) // SLOT_CHUNK) * SLOT_CHUNK
MAX_PIECES = max(LOCAL_SLOTS // PIECE_ROWS[0],
                 N_EXPERTS * (PIECE_ROWS[0] // PIECE_ROWS[1] - 1))


def _cparams(sem):
    return pltpu.CompilerParams(dimension_semantics=sem,
                                vmem_limit_bytes=VMEM_LIMIT_BYTES)


def _rms(x, g):
    ms = jnp.mean(x * x, axis=-1, keepdims=True)
    return x * lax.rsqrt(ms + EPS) * g


def _sigmoid(x):
    return 0.5 * jnp.tanh(0.5 * x) + 0.5


def _inproj_kernel(x_ref, g_ref, w_ref, sw_ref, sbias_ref, sn_ref,
                   qkv_ref, ob_ref, sga_ref, sgb_ref):
    tm = x_ref.shape[0]
    a = _rms(x_ref[...], g_ref[...]).astype(BF16)

    def proj(c0, width):
        return jnp.dot(a, w_ref[:, c0:c0 + width], preferred_element_type=F32)

    qkv_ref[:, 0:SB_WIDTH] = (proj(0, SB_WIDTH) * (LOG2E / math.sqrt(SB_HEAD_DIM))).astype(BF16)
    qkv_ref[:, SB_WIDTH:2 * SB_WIDTH] = proj(SB_WIDTH, SB_WIDTH).astype(BF16)
    qkv_ref[:, 2 * SB_WIDTH:3 * SB_WIDTH] = proj(2 * SB_WIDTH, SB_WIDTH).astype(BF16)

    c_u = 3 * SB_WIDTH
    gu = jax.nn.gelu(proj(c_u, SGU_WIDTH))
    gv = jax.nn.gelu(proj(c_u + SGU_WIDTH, SGU_WIDTH))
    mu = jnp.mean(gv, axis=-1, keepdims=True)
    d = gv - mu
    var = jnp.mean(d * d, axis=-1, keepdims=True)
    vn = (d * lax.rsqrt(var + EPS) * sn_ref[...]).astype(BF16)

    rr = lax.broadcasted_iota(jnp.int32, (SGU_BLOCK, SGU_BLOCK), 0) // CHUNK
    cc = lax.broadcasted_iota(jnp.int32, (SGU_BLOCK, SGU_BLOCK), 1) // CHUNK
    cmask = cc <= rr
    lane = lax.broadcasted_iota(jnp.int32, (SGU_BLOCK, LANES), 1)
    lo_mask = lane < SGU_GROUP_DIM
    wcat = []
    for j in range(SGU_GROUPS // 2):
        w0 = jnp.where(cmask, sw_ref[2 * j], 0.0).astype(BF16)
        w1 = jnp.where(cmask, sw_ref[2 * j + 1], 0.0).astype(BF16)
        wcat.append(jnp.concatenate([w0, w1], axis=1))

    zero = jnp.zeros((), BF16)
    for blk in range(tm // SGU_BLOCK):
        r0 = blk * SGU_BLOCK
        for j in range(SGU_GROUPS // 2):
            vp = vn[r0:r0 + SGU_BLOCK, j * LANES:(j + 1) * LANES]
            rhs = jnp.concatenate([jnp.where(lo_mask, vp, zero),
                                   jnp.where(lo_mask, zero, vp)], axis=0)
            sv = jnp.dot(wcat[j], rhs, preferred_element_type=F32)
            sv = sv + sbias_ref[:, j * LANES:(j + 1) * LANES]
            ob_ref[r0:r0 + SGU_BLOCK, j * LANES:(j + 1) * LANES] = (
                gu[r0:r0 + SGU_BLOCK, j * LANES:(j + 1) * LANES] * sv).astype(BF16)

    c_g = c_u + 2 * SGU_WIDTH
    d_model = sga_ref.shape[1]
    for c0 in range(0, d_model, 512):
        sga_ref[:, c0:c0 + 512] = _sigmoid(proj(c_g + c0, 512)).astype(BF16)
        sgb_ref[:, c0:c0 + 512] = _sigmoid(proj(c_g + d_model + c0, 512)).astype(BF16)


def _inproj(x2, g_mix, w_in, sgu_w, sgu_bias, sgu_norm):
    t, d = x2.shape
    ncol = w_in.shape[1]
    tm = TM_PROJ
    const = lambda *shape: pl.BlockSpec(shape, lambda i: (0,) * len(shape))
    return pl.pallas_call(
        _inproj_kernel,
        grid=(t // tm,),
        in_specs=[
            pl.BlockSpec((tm, d), lambda i: (i, 0)),
            const(1, d),
            const(d, ncol),
            const(SGU_GROUPS, SGU_BLOCK, SGU_BLOCK),
            const(SGU_BLOCK, SGU_WIDTH),
            const(1, SGU_WIDTH),
        ],
        out_specs=[
            pl.BlockSpec((tm, 3 * SB_WIDTH), lambda i: (i, 0)),
            pl.BlockSpec((tm, SGU_WIDTH), lambda i: (i, 0)),
            pl.BlockSpec((tm, d), lambda i: (i, 0)),
            pl.BlockSpec((tm, d), lambda i: (i, 0)),
        ],
        out_shape=[
            jax.ShapeDtypeStruct((t, 3 * SB_WIDTH), BF16),
            jax.ShapeDtypeStruct((t, SGU_WIDTH), BF16),
            jax.ShapeDtypeStruct((t, d), BF16),
            jax.ShapeDtypeStruct((t, d), BF16),
        ],
        compiler_params=_cparams(("arbitrary",)),
        name="inproj_sgu",
    )(x2, g_mix, w_in, sgu_w, sgu_bias, sgu_norm)


def _attn_kernel(q_ref, k_ref, v_ref, ntri_ref, o_ref, carry_sc, acc_sc):
    tq = TQ_ATTN
    kc = KC_ATTN
    lane = lax.broadcasted_iota(jnp.int32, (tq, LANES), 1)
    head0 = lane < SB_HEAD_DIM
    zero = jnp.zeros((), BF16)
    tri = (lax.broadcasted_iota(jnp.int32, (kc, kc), 1)
           < lax.broadcasted_iota(jnp.int32, (kc, kc), 0))

    def tile(sub, _):
        qi = pl.program_id(2) * (q_ref.shape[1] // tq) + sub
        r0 = pl.multiple_of(sub * tq, tq)
        q = q_ref[0, pl.ds(r0, tq), :]
        qh = (jnp.where(head0, q, zero), jnp.where(head0, zero, q))
        carry_sc[...] = jnp.zeros_like(carry_sc)
        acc_sc[...] = jnp.zeros_like(acc_sc)

        def scores(j, lo, hi, causal, keep):
            k0 = pl.multiple_of(j * kc, kc)
            kj = k_ref[0, pl.ds(k0, kc), :]
            out = []
            for h in range(2):
                z = lax.dot_general(qh[h][lo:hi], kj, (((1,), (1,)), ((), ())),
                                    preferred_element_type=F32)
                sp = jnp.maximum(z, 0.0) + jnp.log2(1.0 + jnp.exp2(-jnp.abs(z)))
                if causal:
                    sp = jnp.where(tri, sp, 0.0)
                if keep is not None:
                    sp = jnp.where(keep, sp, 0.0)
                out.append((z, jnp.dot(sp.astype(BF16), ntri_ref[...], preferred_element_type=F32)))
            return k0, lo, hi, causal, keep, out

        def weigh(part):
            k0, lo, hi, causal, keep, out = part
            vj = v_ref[0, pl.ds(k0, kc), :]
            for h in range(2):
                z, cum = out[h]
                carry = carry_sc[h, lo:hi]
                w = jnp.exp2(z + cum + jnp.concatenate([carry] * (kc // LANES), axis=1))
                carry_sc[h, lo:hi] = carry + jnp.broadcast_to(cum[:, 0:1], carry.shape)
                if causal:
                    w = jnp.where(tri, w, 0.0)
                if keep is not None:
                    w = jnp.where(keep, w, 0.0)
                acc_sc[h, lo:hi] += jnp.dot(w.astype(BF16), vj, preferred_element_type=F32)

        ndiag = tq // kc
        npast = qi * ndiag

        def past_part(rb, p):
            return scores(jnp.maximum(npast - 1 - p, 0), rb * kc, (rb + 1) * kc, False, p < npast)

        head = []
        for c in reversed(range(ndiag)):
            head.append(scores(npast + c, c * kc, (c + 1) * kc, True, None))
            if c + 1 < ndiag:
                head.append(scores(npast + c, (c + 1) * kc, tq, False, None))
        for rb in range(ndiag - 1):
            head += [past_part(rb, p) for p in range(ndiag - 1 - rb)]
        for part in head:
            weigh(part)

        def live(t):
            return jnp.logical_and(t < npast, jnp.max(carry_sc[...]) > UNDERFLOW_LOG2)

        def past(t):
            for part in [past_part(rb, ndiag - 1 - rb + t) for rb in range(ndiag)]:
                weigh(part)
            return t + 1

        lax.while_loop(live, past, 0)
        o_ref[0, pl.ds(r0, tq), :] = jnp.where(head0, acc_sc[0], acc_sc[1]).astype(o_ref.dtype)
        return 0

    lax.fori_loop(0, q_ref.shape[1] // tq, tile, 0)


def _attention(qkv3, ntri):
    b, s, _ = qkv3.shape
    tq = TQ_ATTN * ATTN_TILES_PER_STEP
    npair = SB_WIDTH // LANES
    return pl.pallas_call(
        _attn_kernel,
        grid=(b, npair, s // tq),
        in_specs=[
            pl.BlockSpec((1, tq, LANES), lambda bi, p, i: (bi, i, p)),
            pl.BlockSpec((1, s, LANES), lambda bi, p, i: (bi, 0, npair + p)),
            pl.BlockSpec((1, s, LANES), lambda bi, p, i: (bi, 0, 2 * npair + p)),
            pl.BlockSpec((KC_ATTN, KC_ATTN), lambda bi, p, i: (0, 0)),
        ],
        out_specs=pl.BlockSpec((1, tq, LANES), lambda bi, p, i: (bi, i, p)),
        out_shape=jax.ShapeDtypeStruct((b, s, SB_WIDTH), BF16),
        scratch_shapes=[pltpu.VMEM((2, TQ_ATTN, LANES), F32),
                        pltpu.VMEM((2, TQ_ATTN, LANES), F32)],
        compiler_params=_cparams(("arbitrary", "arbitrary", "arbitrary")),
        name="stickbreak_attn",
    )(qkv3, qkv3, qkv3, ntri)


def _outproj_kernel(oa_ref, ob_ref, sga_ref, sgb_ref, x_ref, woa_ref, wob_ref, wo_ref,
                    g_ref, wr_ref, br_ref, upper_ref,
                    h_ref, c_ref, lslot_ref, gate_ref, cnt_ref):
    tm = x_ref.shape[0]
    ma = jnp.dot(oa_ref[...], woa_ref[...], preferred_element_type=F32)
    mb = jnp.dot(ob_ref[...], wob_ref[...], preferred_element_type=F32)
    merged = sga_ref[...].astype(F32) * ma + sgb_ref[...].astype(F32) * mb
    h = x_ref[...] + jnp.dot(merged.astype(BF16), wo_ref[...], preferred_element_type=F32)
    h_ref[...] = h
    c = _rms(h, g_ref[...])
    c_hi = c.astype(BF16)
    c_ref[...] = c_hi

    c_lo = (c - c_hi.astype(F32)).astype(BF16)
    wr = wr_ref[...]
    wr_hi = wr.astype(BF16)
    wr_lo = (wr - wr_hi.astype(F32)).astype(BF16)
    nt = (((1,), (1,)), ((), ()))
    logits = (lax.dot_general(wr_hi, c_hi, nt, preferred_element_type=F32)
              + lax.dot_general(wr_hi, c_lo, nt, preferred_element_type=F32)
              + lax.dot_general(wr_lo, c_hi, nt, preferred_element_type=F32)) + br_ref[...]
    eid = lax.broadcasted_iota(jnp.int32, (N_EXPERTS, tm), 0).astype(F32)
    work = logits
    vals, sels = [], []
    for _ in range(TOP_K):
        m = jnp.max(work, axis=0, keepdims=True)
        ik = jnp.min(jnp.where(work == m, eid, float(N_EXPERTS)), axis=0, keepdims=True)
        sel = eid == ik
        vals.append(m)
        sels.append(sel)
        work = jnp.where(sel, -jnp.inf, work)
    es = [jnp.exp(v - vals[0]) for v in vals]
    inv = 1.0 / (es[0] + es[1] + es[2] + es[3])
    onehot = jnp.zeros((N_EXPERTS, tm), F32)
    for sel in sels:
        onehot = onehot + jnp.where(sel, 1.0, 0.0)

    prefix = jnp.dot(onehot.astype(BF16), upper_ref[...], preferred_element_type=F32)
    n = jnp.sum(onehot, axis=1, keepdims=True)
    n_al = jnp.ceil(n * (1.0 / ROW_ALIGN)) * ROW_ALIGN
    er = lax.broadcasted_iota(jnp.int32, (N_EXPERTS, N_EXPERTS), 0)
    ec = lax.broadcasted_iota(jnp.int32, (N_EXPERTS, N_EXPERTS), 1)
    run_off = jnp.dot(jnp.where(ec < er, 1.0, 0.0), jnp.broadcast_to(n_al, (N_EXPERTS, LANES)),
                      precision=lax.Precision.HIGHEST, preferred_element_type=F32)
    slot = prefix + run_off[:, 0:1]
    for k in range(TOP_K):
        gate_ref[k:k + 1, :] = es[k] * inv
        lslot_ref[k:k + 1, :] = jnp.sum(jnp.where(sels[k], slot, 0.0), axis=0,
                                        keepdims=True).astype(jnp.int32)
    cnt_ref[0] = jnp.broadcast_to(n, (N_EXPERTS, LANES))


def _outproj(o_a, o_b, sga, sgb, x2, w_out_a, w_out_b, w_out, g_moe, w_router_t, b_router, upper):
    t, d = x2.shape
    tm = TM_PROJ
    const = lambda *shape: pl.BlockSpec(shape, lambda i: (0,) * len(shape))
    row = lambda w: pl.BlockSpec((tm, w), lambda i: (i, 0))
    colk = pl.BlockSpec((TOP_K, tm), lambda i: (0, i))
    return pl.pallas_call(
        _outproj_kernel,
        grid=(t // tm,),
        in_specs=[row(SB_WIDTH), row(SGU_WIDTH), row(d), row(d), row(d),
                  const(SB_WIDTH, d), const(SGU_WIDTH, d), const(d, d),
                  const(1, d), const(N_EXPERTS, d), const(N_EXPERTS, 1), const(tm, tm)],
        out_specs=[row(d), row(d), colk, colk,
                   pl.BlockSpec((1, N_EXPERTS, LANES), lambda i: (i, 0, 0))],
        out_shape=[
            jax.ShapeDtypeStruct((t, d), F32),
            jax.ShapeDtypeStruct((t, d), BF16),
            jax.ShapeDtypeStruct((TOP_K, t), jnp.int32),
            jax.ShapeDtypeStruct((TOP_K, t), F32),
            jax.ShapeDtypeStruct((t // tm, N_EXPERTS, LANES), F32),
        ],
        compiler_params=_cparams(("arbitrary",)),
        name="outproj_router",
    )(o_a, o_b, sga, sgb, x2, w_out_a, w_out_b, w_out, g_moe, w_router_t, b_router, upper)


def _piece_loops(tables, win, fn):
    n_ref, local_ref, global_ref = tables
    for cls, rows in enumerate(PIECE_ROWS):
        head = win * len(PIECE_ROWS) + cls

        def body(i, _, off=head * MAX_PIECES, rows=rows):
            fn(pl.multiple_of(local_ref[off + i], ROW_ALIGN),
               pl.multiple_of(global_ref[off + i], ROW_ALIGN), rows)
            return 0

        lax.fori_loop(0, n_ref[head], body, 0)


def _piece_tables(run, loff, base):
    big = PIECE_ROWS[0]
    nbig = run // big
    nsmall = (run - nbig * big) // ROW_ALIGN
    i = jnp.arange(MAX_PIECES, dtype=jnp.int32)
    experts = jnp.arange(N_EXPERTS, dtype=jnp.int32)

    def flat(cnt, rows, skip):
        cum = jnp.cumsum(cnt, axis=1)
        owner = jnp.minimum(jnp.sum(i[None, :, None] >= cum[:, None, :], axis=-1), N_EXPERTS - 1)
        onehot = owner[:, :, None] == experts
        pick = lambda tab: jnp.sum(jnp.where(onehot, tab[:, None, :], 0), axis=-1)
        j = i[None, :] - pick(cum - cnt)
        return cum[:, -1], pick(loff + skip) + j * rows, pick(base + skip) + j * rows

    nb, lb, gb = flat(nbig, big, 0)
    ns, lsm, gsm = flat(nsmall, ROW_ALIGN, nbig * big)
    pack = lambda a, b: jnp.stack([a, b], axis=1).reshape(-1).astype(jnp.int32)
    return pack(nb, ns), pack(lb, lsm), pack(gb, gsm)


def _dispatch_kernel(pn_ref, pl_ref, pg_ref, pad_end_ref, lslot_ref, c_ref, xs_ref,
                     loc_sc, zero_sc, sem, zsem):
    w = pl.program_id(0)
    tm = c_ref.shape[0]
    tb = zero_sc.shape[0]
    nloc = loc_sc.shape[1]

    @pl.when(w == 0)
    def _():
        zero_sc[...] = jnp.zeros_like(zero_sc)

        def pad_copy(e):
            return pltpu.make_async_copy(
                zero_sc, xs_ref.at[pl.ds(pl.multiple_of(pad_end_ref[e] - tb, tb), tb), :], zsem)

        def nonempty(e):
            prev = jnp.where(e > 0, pad_end_ref[jnp.maximum(e - 1, 0)], 0)
            return pad_end_ref[e] > prev

        def zstart(e, _):
            @pl.when(nonempty(e))
            def _():
                pad_copy(e).start()
            return 0

        def zwait(e, _):
            @pl.when(nonempty(e))
            def _():
                pad_copy(e).wait()
            return 0

        lax.fori_loop(0, N_EXPERTS, zstart, 0)
        lax.fori_loop(0, N_EXPERTS, zwait, 0)

    ls = [lslot_ref[k:k + 1, :] for k in range(TOP_K)]
    buf = lax.bitwise_and(w, 1)
    for r0 in range(0, nloc, SLOT_CHUNK):
        srow = lax.broadcasted_iota(jnp.int32, (SLOT_CHUNK, tm), 0) + r0
        sel = jnp.zeros((SLOT_CHUNK, tm), F32)
        for k in range(TOP_K):
            sel = jnp.where(srow == ls[k], 1.0, sel)
        loc_sc[buf, r0:r0 + SLOT_CHUNK, :] = jnp.dot(sel.astype(BF16), c_ref[...],
                                                     preferred_element_type=F32)

    tables = (pn_ref, pl_ref, pg_ref)

    def piece_copy(slot):
        def build(local_row, global_row, rows):
            return pltpu.make_async_copy(loc_sc.at[slot, pl.ds(local_row, rows), :],
                                         xs_ref.at[pl.ds(global_row, rows), :], sem.at[slot])
        return build

    _piece_loops(tables, w, lambda *a: piece_copy(buf)(*a).start())

    @pl.when(w > 0)
    def _():
        _piece_loops(tables, w - 1, lambda *a: piece_copy(1 - buf)(*a).wait())

    @pl.when(w == pl.num_programs(0) - 1)
    def _():
        _piece_loops(tables, w, lambda *a: piece_copy(buf)(*a).wait())


def _dispatch(pieces, pad_end, lslot, c, n_slots):
    t, d = c.shape
    tm = TM_PROJ
    grid_spec = pltpu.PrefetchScalarGridSpec(
        num_scalar_prefetch=4,
        grid=(t // tm,),
        in_specs=[
            pl.BlockSpec((TOP_K, tm), lambda i, *_: (0, i)),
            pl.BlockSpec((tm, d), lambda i, *_: (i, 0)),
        ],
        out_specs=pl.BlockSpec(memory_space=pl.ANY),
        scratch_shapes=[pltpu.VMEM((2, LOCAL_SLOTS, d), F32),
                        pltpu.VMEM((TB_EXPERT, d), F32),
                        pltpu.SemaphoreType.DMA((2,)), pltpu.SemaphoreType.DMA(())],
    )
    return pl.pallas_call(
        _dispatch_kernel,
        grid_spec=grid_spec,
        out_shape=jax.ShapeDtypeStruct((n_slots, d), F32),
        compiler_params=_cparams(("arbitrary",)),
        name="moe_dispatch",
    )(*pieces, pad_end, lslot, c)


def _expert_kernel(be_ref, nb_ref, xs_ref, w1_ref, w2_ref, b1g_ref, b1l_ref, b2_ref, perm_ref,
                   ys_ref, w1g_sc, w1l_sc, w2_sc):
    i = pl.program_id(0)
    active = i < nb_ref[0]
    fresh = jnp.logical_or(i == 0, be_ref[i] != be_ref[jnp.maximum(i - 1, 0)])

    @pl.when(jnp.logical_and(active, fresh))
    def _():
        half = MXU_DIM // 2
        for cb in range(w1_ref.shape[2] // MXU_DIM):
            blk = w1_ref[0, :, cb * MXU_DIM:(cb + 1) * MXU_DIM].astype(BF16)
            de = jnp.dot(blk, perm_ref[...], preferred_element_type=F32)
            w1g_sc[:, cb * half:(cb + 1) * half] = de[:, :half].astype(BF16)
            w1l_sc[:, cb * half:(cb + 1) * half] = de[:, half:].astype(BF16)
        w2_sc[...] = w2_ref[0].astype(BF16)

    @pl.when(active)
    def _():
        xb = xs_ref[...].astype(BF16)
        hg = jnp.dot(xb, w1g_sc[...], preferred_element_type=F32) + b1g_ref[0]
        hl = jnp.dot(xb, w1l_sc[...], preferred_element_type=F32) + b1l_ref[0]
        xg = jnp.minimum(hg, SWIGLU_LIMIT)
        xl = jnp.clip(hl, -SWIGLU_LIMIT, SWIGLU_LIMIT)
        act = xg * _sigmoid(SWIGLU_ALPHA * xg) * (xl + 1.0)
        ys_ref[...] = jnp.dot(act.astype(BF16), w2_sc[...], preferred_element_type=F32) + b2_ref[0]


def _experts(block_e, n_used, xs, w1, w2, b1g, b1l, b2, perm):
    n_slots, d = xs.shape
    f2 = w1.shape[2]
    f = f2 // 2
    tb = TB_EXPERT
    blk = lambda i, be, nb: (jnp.minimum(i, nb[0] - 1), 0)
    wsel = lambda i, be, nb: (be[i], 0, 0)
    grid_spec = pltpu.PrefetchScalarGridSpec(
        num_scalar_prefetch=2,
        grid=(n_slots // tb,),
        in_specs=[
            pl.BlockSpec((tb, d), blk),
            pl.BlockSpec((1, d, f2), wsel),
            pl.BlockSpec((1, f, d), wsel),
            pl.BlockSpec((1, 1, f), wsel),
            pl.BlockSpec((1, 1, f), wsel),
            pl.BlockSpec((1, 1, d), wsel),
            pl.BlockSpec((MXU_DIM, MXU_DIM), lambda i, be, nb: (0, 0)),
        ],
        out_specs=pl.BlockSpec((tb, d), blk),
        scratch_shapes=[pltpu.VMEM((d, f), BF16), pltpu.VMEM((d, f), BF16),
                        pltpu.VMEM((f, d), BF16)],
    )
    return pl.pallas_call(
        _expert_kernel,
        grid_spec=grid_spec,
        out_shape=jax.ShapeDtypeStruct((n_slots, d), F32),
        compiler_params=_cparams(("arbitrary",)),
        name="moe_experts",
    )(block_e, n_used, xs, w1, w2, b1g, b1l, b2, perm)


def _combine_kernel(pn_ref, pl_ref, pg_ref, lslot_ref, gate_ref, h_ref, p_ref, ys_ref,
                    gp_ref, wg_ref, wp_ref, gf_ref, out_ref, loc_sc, locb_sc, sem,
                    *, final):
    w = pl.program_id(0)
    tm = h_ref.shape[0]
    nloc = loc_sc.shape[0]

    tables = (pn_ref, pl_ref, pg_ref)

    def piece_copy(local_row, global_row, rows):
        return pltpu.make_async_copy(ys_ref.at[pl.ds(global_row, rows), :],
                                     loc_sc.at[pl.ds(local_row, rows), :], sem)

    @pl.when(w == 0)
    def _():
        loc_sc[...] = jnp.zeros_like(loc_sc)
        _piece_loops(tables, w, lambda *a: piece_copy(*a).start())

    _piece_loops(tables, w, lambda *a: piece_copy(*a).wait())
    for c0 in range(0, nloc, SLOT_CHUNK):
        locb_sc[c0:c0 + SLOT_CHUNK, :] = loc_sc[c0:c0 + SLOT_CHUNK, :].astype(BF16)

    @pl.when(w + 1 < pl.num_programs(0))
    def _():
        _piece_loops(tables, w + 1, lambda *a: piece_copy(*a).start())

    lcol = [lslot_ref[:, k:k + 1] for k in range(TOP_K)]
    gcol = [gate_ref[:, k:k + 1] for k in range(TOP_K)]
    h = h_ref[...]
    for c0 in range(0, nloc, SLOT_CHUNK):
        scol = lax.broadcasted_iota(jnp.int32, (tm, SLOT_CHUNK), 1) + c0
        g = jnp.zeros((tm, SLOT_CHUNK), F32)
        for k in range(TOP_K):
            g = jnp.where(scol == lcol[k], gcol[k], g)
        h = h + jnp.dot(g.astype(BF16), locb_sc[c0:c0 + SLOT_CHUNK, :],
                        preferred_element_type=F32)

    r = _rms(h, gp_ref[...]).astype(BF16)
    pg = _sigmoid(jnp.dot(r, wg_ref[...], preferred_element_type=F32))
    pp = jnp.dot(p_ref[...].astype(BF16), wp_ref[...], preferred_element_type=F32)
    h = h + pg * pp
    out_ref[...] = _rms(h, gf_ref[...]) if final else h


def _combine(pieces, lslot_tk, gate_tk, h1, p2, ys, g_ple, w_ple_gate, w_ple_proj, g_final,
             final):
    t, d = h1.shape
    pd = p2.shape[1]
    tm = TM_PROJ
    const = lambda *shape: pl.BlockSpec(shape, lambda i, *_: (0,) * len(shape))
    row = lambda width: pl.BlockSpec((tm, width), lambda i, *_: (i, 0))
    grid_spec = pltpu.PrefetchScalarGridSpec(
        num_scalar_prefetch=3,
        grid=(t // tm,),
        in_specs=[row(TOP_K), row(TOP_K), row(d), row(pd),
                  pl.BlockSpec(memory_space=pl.ANY),
                  const(1, d), const(d, d), const(pd, d), const(1, d)],
        out_specs=row(d),
        scratch_shapes=[pltpu.VMEM((LOCAL_SLOTS, d), F32), pltpu.VMEM((LOCAL_SLOTS, d), BF16),
                        pltpu.SemaphoreType.DMA(())],
    )
    return pl.pallas_call(
        functools.partial(_combine_kernel, final=final),
        grid_spec=grid_spec,
        out_shape=jax.ShapeDtypeStruct((t, d), F32),
        compiler_params=_cparams(("arbitrary",)),
        name="combine_ple",
    )(*pieces, lslot_tk, gate_tk, h1, p2, ys, g_ple, w_ple_gate, w_ple_proj, g_final)


def kernel(x, p, g_mix, w_in, w_out_a, w_out_b, w_out, sgu_norm, sgu_w, sgu_b, g_moe,
           w_router, b_router, w1, b1, w2, b2, g_ple, w_ple_gate, w_ple_proj, g_final):
    b, s, d = x.shape
    depth = w_in.shape[0]
    t = b * s
    assert s % (TQ_ATTN * ATTN_TILES_PER_STEP) == 0 and t % TM_PROJ == 0

    kk = jnp.arange(KC_ATTN)
    ntri = -(kk[:, None] >= kk[None, :]).astype(BF16)
    tt = jnp.arange(TM_PROJ)
    upper = (tt[:, None] < tt[None, :]).astype(BF16)
    rr = jnp.arange(MXU_DIM)[:, None]
    cc = jnp.arange(MXU_DIM)[None, :]
    half = MXU_DIM // 2
    perm = (rr == jnp.where(cc < half, 2 * cc, 2 * (cc - half) + 1)).astype(BF16)

    tb = TB_EXPERT
    n_win = t // TM_PROJ
    n_blocks = -(-(t * TOP_K + n_win * N_EXPERTS * (ROW_ALIGN - 1) + N_EXPERTS * (tb - 1)) // tb)
    n_slots = n_blocks * tb

    h = x.reshape(t, d)
    for i in range(depth):
        sgu_bias = jnp.repeat(sgu_b[i].T, SGU_GROUP_DIM, axis=1)
        qkv, o_b, sga, sgb = _inproj(h, g_mix[i][None], w_in[i].astype(BF16), sgu_w[i],
                                     sgu_bias, sgu_norm[i][None])
        o_a = _attention(qkv.reshape(b, s, 3 * SB_WIDTH), ntri).reshape(t, SB_WIDTH)
        h1, c, lslot, gate, cnt = _outproj(
            o_a, o_b, sga, sgb, h, w_out_a[i].astype(BF16), w_out_b[i].astype(BF16),
            w_out[i].astype(BF16), g_moe[i][None], w_router[i].T, b_router[i][:, None], upper)

        counts = cnt[:, :, 0].astype(jnp.int32)
        run = ((counts + ROW_ALIGN - 1) // ROW_ALIGN) * ROW_ALIGN
        loff = jnp.cumsum(run, axis=1) - run
        region = ((jnp.sum(run, axis=0) + tb - 1) // tb) * tb
        pad_end = jnp.cumsum(region).astype(jnp.int32)
        base = (pad_end - region)[None, :] + jnp.cumsum(run, axis=0) - run
        pieces = _piece_tables(run, loff, base)
        starts = jnp.arange(n_blocks, dtype=jnp.int32) * tb
        block_e = jnp.minimum(jnp.sum(starts[:, None] >= pad_end[None, :], axis=1),
                              N_EXPERTS - 1).astype(jnp.int32)
        n_used = (pad_end[-1:] // tb).astype(jnp.int32)

        xs = _dispatch(pieces, pad_end, lslot, c, n_slots)
        ys = _experts(block_e, n_used, xs, w1[i], w2[i],
                      b1[i][:, None, 0::2], b1[i][:, None, 1::2], b2[i][:, None, :], perm)
        h = _combine(pieces, lslot.T, gate.T, h1, p[i].reshape(t, -1), ys, g_ple[i][None],
                     w_ple_gate[i].astype(BF16), w_ple_proj[i].astype(BF16), g_final[None],
                     final=(i == depth - 1))
    return h.reshape(b, s, d)
```

```python
import functools
import math

import jax
import jax.numpy as jnp
from jax import lax
from jax.experimental import pallas as pl
from jax.experimental.pallas import tpu as pltpu

F32 = jnp.float32
BF16 = jnp.bfloat16

EPS = 1e-6
CHUNK = 64
SB_HEADS = 8
SB_HEAD_DIM = 64
SB_WIDTH = SB_HEADS * SB_HEAD_DIM
SGU_GROUPS = 8
SGU_WIDTH = 512
SGU_GROUP_DIM = SGU_WIDTH // SGU_GROUPS
SGU_BLOCK = 128
N_EXPERTS = 32
TOP_K = 4
SWIGLU_ALPHA = 1.702
SWIGLU_LIMIT = 7.0
LOG2E = 1.4426950408889634
UNDERFLOW_LOG2 = -160.0

LANES = 128
MXU_DIM = 256
VMEM_LIMIT_BYTES = 56 * 1024 * 1024

TM_PROJ = 512
TQ_ATTN = 512
ATTN_TILES_PER_STEP = 2
KC_ATTN = MXU_DIM
TB_EXPERT = 512
ROW_ALIGN = 8
PIECE_ROWS = (32, ROW_ALIGN)
SLOT_CHUNK = 256
LOCAL_SLOTS = -(-(TM_PROJ * TOP_K + N_EXPERTS * (ROW_ALIGN - 1)) // SLOT_CHUNK) * SLOT_CHUNK
MAX_PIECES = max(LOCAL_SLOTS // PIECE_ROWS[0],
                 N_EXPERTS * (PIECE_ROWS[0] // PIECE_ROWS[1] - 1))


def _cparams(sem):
    return pltpu.CompilerParams(dimension_semantics=sem,
                                vmem_limit_bytes=VMEM_LIMIT_BYTES)


def _rms(x, g):
    ms = jnp.mean(x * x, axis=-1, keepdims=True)
    return x * lax.rsqrt(ms + EPS) * g


def _sigmoid(x):
    return 0.5 * jnp.tanh(0.5 * x) + 0.5


def _inproj_kernel(x_ref, g_ref, w_ref, sw_ref, sbias_ref, sn_ref,
                   qkv_ref, ob_ref, sga_ref, sgb_ref):
    tm = x_ref.shape[0]
    a = _rms(x_ref[...], g_ref[...]).astype(BF16)

    def proj(c0, width):
        return jnp.dot(a, w_ref[:, c0:c0 + width], preferred_element_type=F32)

    qkv_ref[:, 0:SB_WIDTH] = (proj(0, SB_WIDTH) * (LOG2E / math.sqrt(SB_HEAD_DIM))).astype(BF16)
    qkv_ref[:, SB_WIDTH:2 * SB_WIDTH] = proj(SB_WIDTH, SB_WIDTH).astype(BF16)
    qkv_ref[:, 2 * SB_WIDTH:3 * SB_WIDTH] = proj(2 * SB_WIDTH, SB_WIDTH).astype(BF16)

    c_u = 3 * SB_WIDTH
    gu = jax.nn.gelu(proj(c_u, SGU_WIDTH))
    gv = jax.nn.gelu(proj(c_u + SGU_WIDTH, SGU_WIDTH))
    mu = jnp.mean(gv, axis=-1, keepdims=True)
    d = gv - mu
    var = jnp.mean(d * d, axis=-1, keepdims=True)
    vn = (d * lax.rsqrt(var + EPS) * sn_ref[...]).astype(BF16)

    rr = lax.broadcasted_iota(jnp.int32, (SGU_BLOCK, SGU_BLOCK), 0) // CHUNK
    cc = lax.broadcasted_iota(jnp.int32, (SGU_BLOCK, SGU_BLOCK), 1) // CHUNK
    cmask = cc <= rr
    lane = lax.broadcasted_iota(jnp.int32, (SGU_BLOCK, LANES), 1)
    lo_mask = lane < SGU_GROUP_DIM
    wcat = []
    for j in range(SGU_GROUPS // 2):
        w0 = jnp.where(cmask, sw_ref[2 * j], 0.0).astype(BF16)
        w1 = jnp.where(cmask, sw_ref[2 * j + 1], 0.0).astype(BF16)
        wcat.append(jnp.concatenate([w0, w1], axis=1))

    zero = jnp.zeros((), BF16)
    for blk in range(tm // SGU_BLOCK):
        r0 = blk * SGU_BLOCK
        for j in range(SGU_GROUPS // 2):
            vp = vn[r0:r0 + SGU_BLOCK, j * LANES:(j + 1) * LANES]
            rhs = jnp.concatenate([jnp.where(lo_mask, vp, zero),
                                   jnp.where(lo_mask, zero, vp)], axis=0)
            sv = jnp.dot(wcat[j], rhs, preferred_element_type=F32)
            sv = sv + sbias_ref[:, j * LANES:(j + 1) * LANES]
            ob_ref[r0:r0 + SGU_BLOCK, j * LANES:(j + 1) * LANES] = (
                gu[r0:r0 + SGU_BLOCK, j * LANES:(j + 1) * LANES] * sv).astype(BF16)

    c_g = c_u + 2 * SGU_WIDTH
    d_model = sga_ref.shape[1]
    for c0 in range(0, d_model, 512):
        sga_ref[:, c0:c0 + 512] = _sigmoid(proj(c_g + c0, 512)).astype(BF16)
        sgb_ref[:, c0:c0 + 512] = _sigmoid(proj(c_g + d_model + c0, 512)).astype(BF16)


def _inproj(x2, g_mix, w_in, sgu_w, sgu_bias, sgu_norm):
    t, d = x2.shape
    ncol = w_in.shape[1]
    tm = TM_PROJ
    const = lambda *shape: pl.BlockSpec(shape, lambda i: (0,) * len(shape))
    return pl.pallas_call(
        _inproj_kernel,
        grid=(t // tm,),
        in_specs=[
            pl.BlockSpec((tm, d), lambda i: (i, 0)),
            const(1, d),
            const(d, ncol),
            const(SGU_GROUPS, SGU_BLOCK, SGU_BLOCK),
            const(SGU_BLOCK, SGU_WIDTH),
            const(1, SGU_WIDTH),
        ],
        out_specs=[
            pl.BlockSpec((tm, 3 * SB_WIDTH), lambda i: (i, 0)),
            pl.BlockSpec((tm, SGU_WIDTH), lambda i: (i, 0)),
            pl.BlockSpec((tm, d), lambda i: (i, 0)),
            pl.BlockSpec((tm, d), lambda i: (i, 0)),
        ],
        out_shape=[
            jax.ShapeDtypeStruct((t, 3 * SB_WIDTH), BF16),
            jax.ShapeDtypeStruct((t, SGU_WIDTH), BF16),
            jax.ShapeDtypeStruct((t, d), BF16),
            jax.ShapeDtypeStruct((t, d), BF16),
        ],
        compiler_params=_cparams(("arbitrary",)),
        name="inproj_sgu",
    )(x2, g_mix, w_in, sgu_w, sgu_bias, sgu_norm)


def _attn_kernel(q_ref, k_ref, v_ref, ntri_ref, o_ref, carry_sc, acc_sc):
    tq = TQ_ATTN
    kc = KC_ATTN
    lane = lax.broadcasted_iota(jnp.int32, (tq, LANES), 1)
    head0 = lane < SB_HEAD_DIM
    zero = jnp.zeros((), BF16)
    tri = (lax.broadcasted_iota(jnp.int32, (kc, kc), 1)
           < lax.broadcasted_iota(jnp.int32, (kc, kc), 0))

    def tile(sub, _):
        qi = pl.program_id(2) * (q_ref.shape[1] // tq) + sub
        r0 = pl.multiple_of(sub * tq, tq)
        q = q_ref[0, pl.ds(r0, tq), :]
        qh = (jnp.where(head0, q, zero), jnp.where(head0, zero, q))
        carry_sc[...] = jnp.zeros_like(carry_sc)
        acc_sc[...] = jnp.zeros_like(acc_sc)

        def scores(j, lo, hi, causal, keep):
            k0 = pl.multiple_of(j * kc, kc)
            kj = k_ref[0, pl.ds(k0, kc), :]
            out = []
            for h in range(2):
                z = lax.dot_general(qh[h][lo:hi], kj, (((1,), (1,)), ((), ())),
                                    preferred_element_type=F32)
                sp = jnp.maximum(z, 0.0) + jnp.log2(1.0 + jnp.exp2(-jnp.abs(z)))
                if causal:
                    sp = jnp.where(tri, sp, 0.0)
                if keep is not None:
                    sp = jnp.where(keep, sp, 0.0)
                out.append((z, jnp.dot(sp.astype(BF16), ntri_ref[...], preferred_element_type=F32)))
            return k0, lo, hi, causal, keep, out

        def weigh(part):
            k0, lo, hi, causal, keep, out = part
            vj = v_ref[0, pl.ds(k0, kc), :]
            for h in range(2):
                z, cum = out[h]
                carry = carry_sc[h, lo:hi]
                w = jnp.exp2(z + cum + jnp.concatenate([carry] * (kc // LANES), axis=1))
                carry_sc[h, lo:hi] = carry + jnp.broadcast_to(cum[:, 0:1], carry.shape)
                if causal:
                    w = jnp.where(tri, w, 0.0)
                if keep is not None:
                    w = jnp.where(keep, w, 0.0)
                acc_sc[h, lo:hi] += jnp.dot(w.astype(BF16), vj, preferred_element_type=F32)

        ndiag = tq // kc
        npast = qi * ndiag

        def past_part(rb, p):
            return scores(jnp.maximum(npast - 1 - p, 0), rb * kc, (rb + 1) * kc, False, p < npast)

        head = []
        for c in reversed(range(ndiag)):
            head.append(scores(npast + c, c * kc, (c + 1) * kc, True, None))
            if c + 1 < ndiag:
                head.append(scores(npast + c, (c + 1) * kc, tq, False, None))
        for rb in range(ndiag - 1):
            head += [past_part(rb, p) for p in range(ndiag - 1 - rb)]
        for part in head:
            weigh(part)

        def live(t):
            return jnp.logical_and(t < npast, jnp.max(carry_sc[...]) > UNDERFLOW_LOG2)

        def past(t):
            for part in [past_part(rb, ndiag - 1 - rb + t) for rb in range(ndiag)]:
                weigh(part)
            return t + 1

        lax.while_loop(live, past, 0)
        o_ref[0, pl.ds(r0, tq), :] = jnp.where(head0, acc_sc[0], acc_sc[1]).astype(o_ref.dtype)
        return 0

    lax.fori_loop(0, q_ref.shape[1] // tq, tile, 0)


def _attention(qkv3, ntri):
    b, s, _ = qkv3.shape
    tq = TQ_ATTN * ATTN_TILES_PER_STEP
    npair = SB_WIDTH // LANES
    return pl.pallas_call(
        _attn_kernel,
        grid=(b, npair, s // tq),
        in_specs=[
            pl.BlockSpec((1, tq, LANES), lambda bi, p, i: (bi, i, p)),
            pl.BlockSpec((1, s, LANES), lambda bi, p, i: (bi, 0, npair + p)),
            pl.BlockSpec((1, s, LANES), lambda bi, p, i: (bi, 0, 2 * npair + p)),
            pl.BlockSpec((KC_ATTN, KC_ATTN), lambda bi, p, i: (0, 0)),
        ],
        out_specs=pl.BlockSpec((1, tq, LANES), lambda bi, p, i: (bi, i, p)),
        out_shape=jax.ShapeDtypeStruct((b, s, SB_WIDTH), BF16),
        scratch_shapes=[pltpu.VMEM((2, TQ_ATTN, LANES), F32),
                        pltpu.VMEM((2, TQ_ATTN, LANES), F32)],
        compiler_params=_cparams(("arbitrary", "arbitrary", "arbitrary")),
        name="stickbreak_attn",
    )(qkv3, qkv3, qkv3, ntri)


def _outproj_kernel(oa_ref, ob_ref, sga_ref, sgb_ref, x_ref, woa_ref, wob_ref, wo_ref,
                    g_ref, wr_ref, br_ref, upper_ref,
                    h_ref, c_ref, lslot_ref, gate_ref, cnt_ref):
    tm = x_ref.shape[0]
    ma = jnp.dot(oa_ref[...], woa_ref[...], preferred_element_type=F32)
    mb = jnp.dot(ob_ref[...], wob_ref[...], preferred_element_type=F32)
    merged = sga_ref[...].astype(F32) * ma + sgb_ref[...].astype(F32) * mb
    h = x_ref[...] + jnp.dot(merged.astype(BF16), wo_ref[...], preferred_element_type=F32)
    h_ref[...] = h
    c = _rms(h, g_ref[...])
    c_hi = c.astype(BF16)
    c_ref[...] = c_hi

    c_lo = (c - c_hi.astype(F32)).astype(BF16)
    wr = wr_ref[...]
    wr_hi = wr.astype(BF16)
    wr_lo = (wr - wr_hi.astype(F32)).astype(BF16)
    nt = (((1,), (1,)), ((), ()))
    logits = (lax.dot_general(wr_hi, c_hi, nt, preferred_element_type=F32)
              + lax.dot_general(wr_hi, c_lo, nt, preferred_element_type=F32)
              + lax.dot_general(wr_lo, c_hi, nt, preferred_element_type=F32)) + br_ref[...]
    eid = lax.broadcasted_iota(jnp.int32, (N_EXPERTS, tm), 0).astype(F32)
    work = logits
    vals, sels = [], []
    for _ in range(TOP_K):
        m = jnp.max(work, axis=0, keepdims=True)
        ik = jnp.min(jnp.where(work == m, eid, float(N_EXPERTS)), axis=0, keepdims=True)
        sel = eid == ik
        vals.append(m)
        sels.append(sel)
        work = jnp.where(sel, -jnp.inf, work)
    es = [jnp.exp(v - vals[0]) for v in vals]
    inv = 1.0 / (es[0] + es[1] + es[2] + es[3])
    onehot = jnp.zeros((N_EXPERTS, tm), F32)
    for sel in sels:
        onehot = onehot + jnp.where(sel, 1.0, 0.0)

    prefix = jnp.dot(onehot.astype(BF16), upper_ref[...], preferred_element_type=F32)
    n = jnp.sum(onehot, axis=1, keepdims=True)
    n_al = jnp.ceil(n * (1.0 / ROW_ALIGN)) * ROW_ALIGN
    er = lax.broadcasted_iota(jnp.int32, (N_EXPERTS, N_EXPERTS), 0)
    ec = lax.broadcasted_iota(jnp.int32, (N_EXPERTS, N_EXPERTS), 1)
    run_off = jnp.dot(jnp.where(ec < er, 1.0, 0.0), jnp.broadcast_to(n_al, (N_EXPERTS, LANES)),
                      precision=lax.Precision.HIGHEST, preferred_element_type=F32)
    slot = prefix + run_off[:, 0:1]
    for k in range(TOP_K):
        gate_ref[k:k + 1, :] = es[k] * inv
        lslot_ref[k:k + 1, :] = jnp.sum(jnp.where(sels[k], slot, 0.0), axis=0,
                                        keepdims=True).astype(jnp.int32)
    cnt_ref[0] = jnp.broadcast_to(n, (N_EXPERTS, LANES))


def _outproj(o_a, o_b, sga, sgb, x2, w_out_a, w_out_b, w_out, g_moe, w_router_t, b_router, upper):
    t, d = x2.shape
    tm = TM_PROJ
    const = lambda *shape: pl.BlockSpec(shape, lambda i: (0,) * len(shape))
    row = lambda w: pl.BlockSpec((tm, w), lambda i: (i, 0))
    colk = pl.BlockSpec((TOP_K, tm), lambda i: (0, i))
    return pl.pallas_call(
        _outproj_kernel,
        grid=(t // tm,),
        in_specs=[row(SB_WIDTH), row(SGU_WIDTH), row(d), row(d), row(d),
                  const(SB_WIDTH, d), const(SGU_WIDTH, d), const(d, d),
                  const(1, d), const(N_EXPERTS, d), const(N_EXPERTS, 1), const(tm, tm)],
        out_specs=[row(d), row(d), colk, colk,
                   pl.BlockSpec((1, N_EXPERTS, LANES), lambda i: (i, 0, 0))],
        out_shape=[
            jax.ShapeDtypeStruct((t, d), F32),
            jax.ShapeDtypeStruct((t, d), BF16),
            jax.ShapeDtypeStruct((TOP_K, t), jnp.int32),
            jax.ShapeDtypeStruct((TOP_K, t), F32),
            jax.ShapeDtypeStruct((t // tm, N_EXPERTS, LANES), F32),
        ],
        compiler_params=_cparams(("arbitrary",)),
        name="outproj_router",
    )(o_a, o_b, sga, sgb, x2, w_out_a, w_out_b, w_out, g_moe, w_router_t, b_router, upper)


def _piece_loops(tables, win, fn):
    n_ref, local_ref, global_ref = tables
    for cls, rows in enumerate(PIECE_ROWS):
        head = win * len(PIECE_ROWS) + cls
        count = n_ref[head]

        def piece(i, off=head * MAX_PIECES, rows=rows):
            fn(pl.multiple_of(local_ref[off + i], ROW_ALIGN),
               pl.multiple_of(global_ref[off + i], ROW_ALIGN), rows)

        def pair(i, _, piece=piece):
            piece(2 * i)
            piece(2 * i + 1)
            return 0

        lax.fori_loop(0, lax.shift_right_logical(count, 1), pair, 0)

        @pl.when(lax.bitwise_and(count, 1) == 1)
        def _(piece=piece, count=count):
            piece(count - 1)


def _piece_tables(run, loff, base):
    big = PIECE_ROWS[0]
    nbig = run // big
    nsmall = (run - nbig * big) // ROW_ALIGN
    i = jnp.arange(MAX_PIECES, dtype=jnp.int32)
    experts = jnp.arange(N_EXPERTS, dtype=jnp.int32)

    def flat(cnt, rows, skip):
        cum = jnp.cumsum(cnt, axis=1)
        owner = jnp.minimum(jnp.sum(i[None, :, None] >= cum[:, None, :], axis=-1), N_EXPERTS - 1)
        onehot = owner[:, :, None] == experts
        pick = lambda tab: jnp.sum(jnp.where(onehot, tab[:, None, :], 0), axis=-1)
        j = i[None, :] - pick(cum - cnt)
        return cum[:, -1], pick(loff + skip) + j * rows, pick(base + skip) + j * rows

    nb, lb, gb = flat(nbig, big, 0)
    ns, lsm, gsm = flat(nsmall, ROW_ALIGN, nbig * big)
    pack = lambda a, b: jnp.stack([a, b], axis=1).reshape(-1).astype(jnp.int32)
    return pack(nb, ns), pack(lb, lsm), pack(gb, gsm)


def _dispatch_kernel(pn_ref, pl_ref, pg_ref, pad_end_ref, lslot_ref, c_ref, xs_ref,
                     loc_sc, zero_sc, sem, zsem):
    w = pl.program_id(0)
    tm = c_ref.shape[0]
    tb = zero_sc.shape[0]
    nloc = loc_sc.shape[1]

    @pl.when(w == 0)
    def _():
        zero_sc[...] = jnp.zeros_like(zero_sc)

        def pad_copy(e):
            return pltpu.make_async_copy(
                zero_sc, xs_ref.at[pl.ds(pl.multiple_of(pad_end_ref[e] - tb, tb), tb), :], zsem)

        def nonempty(e):
            prev = jnp.where(e > 0, pad_end_ref[jnp.maximum(e - 1, 0)], 0)
            return pad_end_ref[e] > prev

        def zstart(e, _):
            @pl.when(nonempty(e))
            def _():
                pad_copy(e).start()
            return 0

        def zwait(e, _):
            @pl.when(nonempty(e))
            def _():
                pad_copy(e).wait()
            return 0

        lax.fori_loop(0, N_EXPERTS, zstart, 0)
        lax.fori_loop(0, N_EXPERTS, zwait, 0)

    ls = [lslot_ref[k:k + 1, :] for k in range(TOP_K)]
    buf = lax.bitwise_and(w, 1)
    for r0 in range(0, nloc, SLOT_CHUNK):
        srow = lax.broadcasted_iota(jnp.int32, (SLOT_CHUNK, tm), 0) + r0
        sel = jnp.zeros((SLOT_CHUNK, tm), F32)
        for k in range(TOP_K):
            sel = jnp.where(srow == ls[k], 1.0, sel)
        loc_sc[buf, r0:r0 + SLOT_CHUNK, :] = jnp.dot(sel.astype(BF16), c_ref[...],
                                                     preferred_element_type=F32)

    tables = (pn_ref, pl_ref, pg_ref)

    def piece_copy(slot):
        def build(local_row, global_row, rows):
            return pltpu.make_async_copy(loc_sc.at[slot, pl.ds(local_row, rows), :],
                                         xs_ref.at[pl.ds(global_row, rows), :], sem.at[slot])
        return build

    _piece_loops(tables, w, lambda *a: piece_copy(buf)(*a).start())

    @pl.when(w > 0)
    def _():
        _piece_loops(tables, w - 1, lambda *a: piece_copy(1 - buf)(*a).wait())

    @pl.when(w == pl.num_programs(0) - 1)
    def _():
        _piece_loops(tables, w, lambda *a: piece_copy(buf)(*a).wait())


def _dispatch(pieces, pad_end, lslot, c, n_slots):
    t, d = c.shape
    tm = TM_PROJ
    grid_spec = pltpu.PrefetchScalarGridSpec(
        num_scalar_prefetch=4,
        grid=(t // tm,),
        in_specs=[
            pl.BlockSpec((TOP_K, tm), lambda i, *_: (0, i)),
            pl.BlockSpec((tm, d), lambda i, *_: (i, 0)),
        ],
        out_specs=pl.BlockSpec(memory_space=pl.ANY),
        scratch_shapes=[pltpu.VMEM((2, LOCAL_SLOTS, d), F32),
                        pltpu.VMEM((TB_EXPERT, d), F32),
                        pltpu.SemaphoreType.DMA((2,)), pltpu.SemaphoreType.DMA(())],
    )
    return pl.pallas_call(
        _dispatch_kernel,
        grid_spec=grid_spec,
        out_shape=jax.ShapeDtypeStruct((n_slots, d), F32),
        compiler_params=_cparams(("arbitrary",)),
        name="moe_dispatch",
    )(*pieces, pad_end, lslot, c)


def _expert_kernel(be_ref, nb_ref, xs_ref, w1_ref, w2_ref, b1g_ref, b1l_ref, b2_ref, perm_ref,
                   ys_ref, w1g_sc, w1l_sc, w2_sc):
    i = pl.program_id(0)
    active = i < nb_ref[0]
    fresh = jnp.logical_or(i == 0, be_ref[i] != be_ref[jnp.maximum(i - 1, 0)])

    @pl.when(jnp.logical_and(active, fresh))
    def _():
        half = MXU_DIM // 2
        for cb in range(w1_ref.shape[2] // MXU_DIM):
            blk = w1_ref[0, :, cb * MXU_DIM:(cb + 1) * MXU_DIM].astype(BF16)
            de = jnp.dot(blk, perm_ref[...], preferred_element_type=F32)
            w1g_sc[:, cb * half:(cb + 1) * half] = de[:, :half].astype(BF16)
            w1l_sc[:, cb * half:(cb + 1) * half] = de[:, half:].astype(BF16)
        w2_sc[...] = w2_ref[0].astype(BF16)

    @pl.when(active)
    def _():
        xb = xs_ref[...].astype(BF16)
        hg = jnp.dot(xb, w1g_sc[...], preferred_element_type=F32) + b1g_ref[0]
        hl = jnp.dot(xb, w1l_sc[...], preferred_element_type=F32) + b1l_ref[0]
        xg = jnp.minimum(hg, SWIGLU_LIMIT)
        xl = jnp.clip(hl, -SWIGLU_LIMIT, SWIGLU_LIMIT)
        act = xg * _sigmoid(SWIGLU_ALPHA * xg) * (xl + 1.0)
        ys_ref[...] = jnp.dot(act.astype(BF16), w2_sc[...], preferred_element_type=F32) + b2_ref[0]


def _experts(block_e, n_used, xs, w1, w2, b1g, b1l, b2, perm):
    n_slots, d = xs.shape
    f2 = w1.shape[2]
    f = f2 // 2
    tb = TB_EXPERT
    blk = lambda i, be, nb: (jnp.minimum(i, nb[0] - 1), 0)
    wsel = lambda i, be, nb: (be[i], 0, 0)
    grid_spec = pltpu.PrefetchScalarGridSpec(
        num_scalar_prefetch=2,
        grid=(n_slots // tb,),
        in_specs=[
            pl.BlockSpec((tb, d), blk),
            pl.BlockSpec((1, d, f2), wsel),
            pl.BlockSpec((1, f, d), wsel),
            pl.BlockSpec((1, 1, f), wsel),
            pl.BlockSpec((1, 1, f), wsel),
            pl.BlockSpec((1, 1, d), wsel),
            pl.BlockSpec((MXU_DIM, MXU_DIM), lambda i, be, nb: (0, 0)),
        ],
        out_specs=pl.BlockSpec((tb, d), blk),
        scratch_shapes=[pltpu.VMEM((d, f), BF16), pltpu.VMEM((d, f), BF16),
                        pltpu.VMEM((f, d), BF16)],
    )
    return pl.pallas_call(
        _expert_kernel,
        grid_spec=grid_spec,
        out_shape=jax.ShapeDtypeStruct((n_slots, d), F32),
        compiler_params=_cparams(("arbitrary",)),
        name="moe_experts",
    )(block_e, n_used, xs, w1, w2, b1g, b1l, b2, perm)


def _combine_kernel(pn_ref, pl_ref, pg_ref, lslot_ref, gate_ref, h_ref, p_ref, ys_ref,
                    gp_ref, wg_ref, wp_ref, gf_ref, out_ref, loc_sc, locb_sc, sem,
                    *, final):
    w = pl.program_id(0)
    tm = h_ref.shape[0]
    nloc = loc_sc.shape[0]

    tables = (pn_ref, pl_ref, pg_ref)

    def piece_copy(local_row, global_row, rows):
        return pltpu.make_async_copy(ys_ref.at[pl.ds(global_row, rows), :],
                                     loc_sc.at[pl.ds(local_row, rows), :], sem)

    @pl.when(w == 0)
    def _():
        loc_sc[...] = jnp.zeros_like(loc_sc)
        _piece_loops(tables, w, lambda *a: piece_copy(*a).start())

    _piece_loops(tables, w, lambda *a: piece_copy(*a).wait())
    for c0 in range(0, nloc, SLOT_CHUNK):
        locb_sc[c0:c0 + SLOT_CHUNK, :] = loc_sc[c0:c0 + SLOT_CHUNK, :].astype(BF16)

    @pl.when(w + 1 < pl.num_programs(0))
    def _():
        _piece_loops(tables, w + 1, lambda *a: piece_copy(*a).start())

    lcol = [lslot_ref[:, k:k + 1] for k in range(TOP_K)]
    gcol = [gate_ref[:, k:k + 1] for k in range(TOP_K)]
    h = h_ref[...]
    for c0 in range(0, nloc, SLOT_CHUNK):
        scol = lax.broadcasted_iota(jnp.int32, (tm, SLOT_CHUNK), 1) + c0
        g = jnp.zeros((tm, SLOT_CHUNK), F32)
        for k in range(TOP_K):
            g = jnp.where(scol == lcol[k], gcol[k], g)
        h = h + jnp.dot(g.astype(BF16), locb_sc[c0:c0 + SLOT_CHUNK, :],
                        preferred_element_type=F32)

    r = _rms(h, gp_ref[...]).astype(BF16)
    pg = _sigmoid(jnp.dot(r, wg_ref[...], preferred_element_type=F32))
    pp = jnp.dot(p_ref[...].astype(BF16), wp_ref[...], preferred_element_type=F32)
    h = h + pg * pp
    out_ref[...] = _rms(h, gf_ref[...]) if final else h


def _combine(pieces, lslot_tk, gate_tk, h1, p2, ys, g_ple, w_ple_gate, w_ple_proj, g_final,
             final):
    t, d = h1.shape
    pd = p2.shape[1]
    tm = TM_PROJ
    const = lambda *shape: pl.BlockSpec(shape, lambda i, *_: (0,) * len(shape))
    row = lambda width: pl.BlockSpec((tm, width), lambda i, *_: (i, 0))
    grid_spec = pltpu.PrefetchScalarGridSpec(
        num_scalar_prefetch=3,
        grid=(t // tm,),
        in_specs=[row(TOP_K), row(TOP_K), row(d), row(pd),
                  pl.BlockSpec(memory_space=pl.ANY),
                  const(1, d), const(d, d), const(pd, d), const(1, d)],
        out_specs=row(d),
        scratch_shapes=[pltpu.VMEM((LOCAL_SLOTS, d), F32), pltpu.VMEM((LOCAL_SLOTS, d), BF16),
                        pltpu.SemaphoreType.DMA(())],
    )
    return pl.pallas_call(
        functools.partial(_combine_kernel, final=final),
        grid_spec=grid_spec,
        out_shape=jax.ShapeDtypeStruct((t, d), F32),
        compiler_params=_cparams(("arbitrary",)),
        name="combine_ple",
    )(*pieces, lslot_tk, gate_tk, h1, p2, ys, g_ple, w_ple_gate, w_ple_proj, g_final)


def kernel(x, p, g_mix, w_in, w_out_a, w_out_b, w_out, sgu_norm, sgu_w, sgu_b, g_moe,
           w_router, b_router, w1, b1, w2, b2, g_ple, w_ple_gate, w_ple_proj, g_final):
    b, s, d = x.shape
    depth = w_in.shape[0]
    t = b * s
    assert s % (TQ_ATTN * ATTN_TILES_PER_STEP) == 0 and t % TM_PROJ == 0

    kk = jnp.arange(KC_ATTN)
    ntri = -(kk[:, None] >= kk[None, :]).astype(BF16)
    tt = jnp.arange(TM_PROJ)
    upper = (tt[:, None] < tt[None, :]).astype(BF16)
    rr = jnp.arange(MXU_DIM)[:, None]
    cc = jnp.arange(MXU_DIM)[None, :]
    half = MXU_DIM // 2
    perm = (rr == jnp.where(cc < half, 2 * cc, 2 * (cc - half) + 1)).astype(BF16)

    tb = TB_EXPERT
    n_win = t // TM_PROJ
    n_blocks = -(-(t * TOP_K + n_win * N_EXPERTS * (ROW_ALIGN - 1) + N_EXPERTS * (tb - 1)) // tb)
    n_slots = n_blocks * tb

    h = x.reshape(t, d)
    for i in range(depth):
        sgu_bias = jnp.repeat(sgu_b[i].T, SGU_GROUP_DIM, axis=1)
        qkv, o_b, sga, sgb = _inproj(h, g_mix[i][None], w_in[i].astype(BF16), sgu_w[i],
                                     sgu_bias, sgu_norm[i][None])
        o_a = _attention(qkv.reshape(b, s, 3 * SB_WIDTH), ntri).reshape(t, SB_WIDTH)
        h1, c, lslot, gate, cnt = _outproj(
            o_a, o_b, sga, sgb, h, w_out_a[i].astype(BF16), w_out_b[i].astype(BF16),
            w_out[i].astype(BF16), g_moe[i][None], w_router[i].T, b_router[i][:, None], upper)

        counts = cnt[:, :, 0].astype(jnp.int32)
        run = ((counts + ROW_ALIGN - 1) // ROW_ALIGN) * ROW_ALIGN
        loff = jnp.cumsum(run, axis=1) - run
        region = ((jnp.sum(run, axis=0) + tb - 1) // tb) * tb
        pad_end = jnp.cumsum(region).astype(jnp.int32)
        base = (pad_end - region)[None, :] + jnp.cumsum(run, axis=0) - run
        pieces = _piece_tables(run, loff, base)
        starts = jnp.arange(n_blocks, dtype=jnp.int32) * tb
        block_e = jnp.minimum(jnp.sum(starts[:, None] >= pad_end[None, :], axis=1),
                              N_EXPERTS - 1).astype(jnp.int32)
        n_used = (pad_end[-1:] // tb).astype(jnp.int32)

        xs = _dispatch(pieces, pad_end, lslot, c, n_slots)
        ys = _experts(block_e, n_used, xs, w1[i], w2[i],
                      b1[i][:, None, 0::2], b1[i][:, None, 1::2], b2[i][:, None, :], perm)
        h = _combine(pieces, lslot.T, gate.T, h1, p[i].reshape(t, -1), ys, g_ple[i][None],
                     w_ple_gate[i].astype(BF16), w_ple_proj[i].astype(BF16), g_final[None],
                     final=(i == depth - 1))
    return h.reshape(b, s, d)
```

```python
import functools
import math

import jax
import jax.numpy as jnp
from jax import lax
from jax.experimental import pallas as pl
from jax.experimental.pallas import tpu as pltpu

F32 = jnp.float32
BF16 = jnp.bfloat16

EPS = 1e-6
CHUNK = 64
SB_HEADS = 8
SB_HEAD_DIM = 64
SB_WIDTH = SB_HEADS * SB_HEAD_DIM
SGU_GROUPS = 8
SGU_WIDTH = 512
SGU_GROUP_DIM = SGU_WIDTH // SGU_GROUPS
SGU_BLOCK = 128
N_EXPERTS = 32
TOP_K = 4
SWIGLU_ALPHA = 1.702
SWIGLU_LIMIT = 7.0
LOG2E = 1.4426950408889634
UNDERFLOW_LOG2 = -160.0

LANES = 128
MXU_DIM = 256
VMEM_LIMIT_BYTES = 56 * 1024 * 1024

TM_PROJ = 512
TQ_ATTN = 512
ATTN_TILES_PER_STEP = 2
KC_ATTN = MXU_DIM
TB_EXPERT = 512
ROW_ALIGN = 8
PIECE_ROWS = (32, ROW_ALIGN)
SLOT_CHUNK = 256
LOCAL_SLOTS = -(-(TM_PROJ * TOP_K + N_EXPERTS * (ROW_ALIGN - 1)) // SLOT_CHUNK) * SLOT_CHUNK
MAX_PIECES = max(LOCAL_SLOTS // PIECE_ROWS[0],
                 N_EXPERTS * (PIECE_ROWS[0] // PIECE_ROWS[1] - 1))


def _cparams(sem):
    return pltpu.CompilerParams(dimension_semantics=sem,
                                vmem_limit_bytes=VMEM_LIMIT_BYTES)


def _rms(x, g):
    ms = jnp.mean(x * x, axis=-1, keepdims=True)
    return x * lax.rsqrt(ms + EPS) * g


def _sigmoid(x):
    return 0.5 * jnp.tanh(0.5 * x) + 0.5


def _inproj_kernel(x_ref, g_ref, w_ref, sw_ref, sbias_ref, sn_ref,
                   qkv_ref, ob_ref, sga_ref, sgb_ref):
    tm = x_ref.shape[0]
    a = _rms(x_ref[...], g_ref[...]).astype(BF16)

    def proj(c0, width):
        return jnp.dot(a, w_ref[:, c0:c0 + width], preferred_element_type=F32)

    qkv_ref[:, 0:SB_WIDTH] = (proj(0, SB_WIDTH) * (LOG2E / math.sqrt(SB_HEAD_DIM))).astype(BF16)
    qkv_ref[:, SB_WIDTH:2 * SB_WIDTH] = proj(SB_WIDTH, SB_WIDTH).astype(BF16)
    qkv_ref[:, 2 * SB_WIDTH:3 * SB_WIDTH] = proj(2 * SB_WIDTH, SB_WIDTH).astype(BF16)

    c_u = 3 * SB_WIDTH
    gu = jax.nn.gelu(proj(c_u, SGU_WIDTH))
    gv = jax.nn.gelu(proj(c_u + SGU_WIDTH, SGU_WIDTH))
    mu = jnp.mean(gv, axis=-1, keepdims=True)
    d = gv - mu
    var = jnp.mean(d * d, axis=-1, keepdims=True)
    vn = (d * lax.rsqrt(var + EPS) * sn_ref[...]).astype(BF16)

    rr = lax.broadcasted_iota(jnp.int32, (SGU_BLOCK, SGU_BLOCK), 0) // CHUNK
    cc = lax.broadcasted_iota(jnp.int32, (SGU_BLOCK, SGU_BLOCK), 1) // CHUNK
    cmask = cc <= rr
    lane = lax.broadcasted_iota(jnp.int32, (SGU_BLOCK, LANES), 1)
    lo_mask = lane < SGU_GROUP_DIM
    wcat = []
    for j in range(SGU_GROUPS // 2):
        w0 = jnp.where(cmask, sw_ref[2 * j], 0.0).astype(BF16)
        w1 = jnp.where(cmask, sw_ref[2 * j + 1], 0.0).astype(BF16)
        wcat.append(jnp.concatenate([w0, w1], axis=1))

    zero = jnp.zeros((), BF16)
    for blk in range(tm // SGU_BLOCK):
        r0 = blk * SGU_BLOCK
        for j in range(SGU_GROUPS // 2):
            vp = vn[r0:r0 + SGU_BLOCK, j * LANES:(j + 1) * LANES]
            rhs = jnp.concatenate([jnp.where(lo_mask, vp, zero),
                                   jnp.where(lo_mask, zero, vp)], axis=0)
            sv = jnp.dot(wcat[j], rhs, preferred_element_type=F32)
            sv = sv + sbias_ref[:, j * LANES:(j + 1) * LANES]
            ob_ref[r0:r0 + SGU_BLOCK, j * LANES:(j + 1) * LANES] = (
                gu[r0:r0 + SGU_BLOCK, j * LANES:(j + 1) * LANES] * sv).astype(BF16)

    c_g = c_u + 2 * SGU_WIDTH
    d_model = sga_ref.shape[1]
    for c0 in range(0, d_model, 512):
        sga_ref[:, c0:c0 + 512] = _sigmoid(proj(c_g + c0, 512)).astype(BF16)
        sgb_ref[:, c0:c0 + 512] = _sigmoid(proj(c_g + d_model + c0, 512)).astype(BF16)


def _inproj(x2, g_mix, w_in, sgu_w, sgu_bias, sgu_norm):
    t, d = x2.shape
    ncol = w_in.shape[1]
    tm = TM_PROJ
    const = lambda *shape: pl.BlockSpec(shape, lambda i: (0,) * len(shape))
    return pl.pallas_call(
        _inproj_kernel,
        grid=(t // tm,),
        in_specs=[
            pl.BlockSpec((tm, d), lambda i: (i, 0)),
            const(1, d),
            const(d, ncol),
            const(SGU_GROUPS, SGU_BLOCK, SGU_BLOCK),
            const(SGU_BLOCK, SGU_WIDTH),
            const(1, SGU_WIDTH),
        ],
        out_specs=[
            pl.BlockSpec((tm, 3 * SB_WIDTH), lambda i: (i, 0)),
            pl.BlockSpec((tm, SGU_WIDTH), lambda i: (i, 0)),
            pl.BlockSpec((tm, d), lambda i: (i, 0)),
            pl.BlockSpec((tm, d), lambda i: (i, 0)),
        ],
        out_shape=[
            jax.ShapeDtypeStruct((t, 3 * SB_WIDTH), BF16),
            jax.ShapeDtypeStruct((t, SGU_WIDTH), BF16),
            jax.ShapeDtypeStruct((t, d), BF16),
            jax.ShapeDtypeStruct((t, d), BF16),
        ],
        compiler_params=_cparams(("arbitrary",)),
        name="inproj_sgu",
    )(x2, g_mix, w_in, sgu_w, sgu_bias, sgu_norm)


def _attn_kernel(q_ref, k_ref, v_ref, ntri_ref, o_ref, carry_sc, acc_sc):
    tq = TQ_ATTN
    kc = KC_ATTN
    head0 = lax.broadcasted_iota(jnp.int32, (kc, LANES), 1) < SB_HEAD_DIM
    zero = jnp.zeros((), BF16)
    tri2 = (lax.broadcasted_iota(jnp.int32, (2 * kc, kc), 1)
            < lax.bitwise_and(lax.broadcasted_iota(jnp.int32, (2 * kc, kc), 0), kc - 1))

    def tile(sub, _):
        qi = pl.program_id(2) * (q_ref.shape[1] // tq) + sub
        r0 = pl.multiple_of(sub * tq, tq)
        q = q_ref[0, pl.ds(r0, tq), :]
        qs = [jnp.concatenate([jnp.where(head0, q[rb * kc:(rb + 1) * kc], zero),
                               jnp.where(head0, zero, q[rb * kc:(rb + 1) * kc])], axis=0)
              for rb in range(tq // kc)]
        carry_sc[...] = jnp.zeros_like(carry_sc)
        acc_sc[...] = jnp.zeros_like(acc_sc)

        def scores(j, rb, causal, keep):
            k0 = pl.multiple_of(j * kc, kc)
            z = lax.dot_general(qs[rb], k_ref[0, pl.ds(k0, kc), :], (((1,), (1,)), ((), ())),
                                preferred_element_type=F32)
            sp = jnp.maximum(z, 0.0) + jnp.log2(1.0 + jnp.exp2(-jnp.abs(z)))
            if causal:
                sp = jnp.where(tri2, sp, 0.0)
            if keep is not None:
                sp = jnp.where(keep, sp, 0.0)
            cum = jnp.dot(sp.astype(BF16), ntri_ref[...], preferred_element_type=F32)
            return k0, rb, causal, keep, z, cum

        def weigh(part):
            k0, rb, causal, keep, z, cum = part
            carry = carry_sc[rb]
            w = jnp.exp2(z + cum + jnp.concatenate([carry] * (kc // LANES), axis=1))
            carry_sc[rb] = carry + jnp.broadcast_to(cum[:, 0:1], carry.shape)
            if causal:
                w = jnp.where(tri2, w, 0.0)
            if keep is not None:
                w = jnp.where(keep, w, 0.0)
            acc_sc[rb] += jnp.dot(w.astype(BF16), v_ref[0, pl.ds(k0, kc), :],
                                  preferred_element_type=F32)

        ndiag = tq // kc
        npast = qi * ndiag

        def past_part(rb, p):
            return scores(jnp.maximum(npast - 1 - p, 0), rb, False, p < npast)

        head = []
        for c in reversed(range(ndiag)):
            head.append(scores(npast + c, c, True, None))
            head += [scores(npast + c, rb, False, None) for rb in range(c + 1, ndiag)]
        for rb in range(ndiag - 1):
            head += [past_part(rb, p) for p in range(ndiag - 1 - rb)]
        for part in head:
            weigh(part)

        def live(t):
            return jnp.logical_and(t < npast, jnp.max(carry_sc[...]) > UNDERFLOW_LOG2)

        def past(t):
            for part in [past_part(rb, ndiag - 1 - rb + t) for rb in range(ndiag)]:
                weigh(part)
            return t + 1

        lax.while_loop(live, past, 0)
        for rb in range(ndiag):
            acc = acc_sc[rb]
            o_ref[0, pl.ds(r0 + rb * kc, kc), :] = jnp.where(head0, acc[:kc],
                                                             acc[kc:]).astype(o_ref.dtype)
        return 0

    lax.fori_loop(0, q_ref.shape[1] // tq, tile, 0)


def _attention(qkv3, ntri):
    b, s, _ = qkv3.shape
    tq = TQ_ATTN * ATTN_TILES_PER_STEP
    npair = SB_WIDTH // LANES
    return pl.pallas_call(
        _attn_kernel,
        grid=(b, npair, s // tq),
        in_specs=[
            pl.BlockSpec((1, tq, LANES), lambda bi, p, i: (bi, i, p)),
            pl.BlockSpec((1, s, LANES), lambda bi, p, i: (bi, 0, npair + p)),
            pl.BlockSpec((1, s, LANES), lambda bi, p, i: (bi, 0, 2 * npair + p)),
            pl.BlockSpec((KC_ATTN, KC_ATTN), lambda bi, p, i: (0, 0)),
        ],
        out_specs=pl.BlockSpec((1, tq, LANES), lambda bi, p, i: (bi, i, p)),
        out_shape=jax.ShapeDtypeStruct((b, s, SB_WIDTH), BF16),
        scratch_shapes=[pltpu.VMEM((TQ_ATTN // KC_ATTN, 2 * KC_ATTN, LANES), F32),
                        pltpu.VMEM((TQ_ATTN // KC_ATTN, 2 * KC_ATTN, LANES), F32)],
        compiler_params=_cparams(("arbitrary", "arbitrary", "arbitrary")),
        name="stickbreak_attn",
    )(qkv3, qkv3, qkv3, ntri)


def _outproj_kernel(oa_ref, ob_ref, sga_ref, sgb_ref, x_ref, woa_ref, wob_ref, wo_ref,
                    g_ref, wr_ref, br_ref, upper_ref,
                    h_ref, c_ref, lslot_ref, gate_ref, cnt_ref):
    tm = x_ref.shape[0]
    ma = jnp.dot(oa_ref[...], woa_ref[...], preferred_element_type=F32)
    mb = jnp.dot(ob_ref[...], wob_ref[...], preferred_element_type=F32)
    merged = sga_ref[...].astype(F32) * ma + sgb_ref[...].astype(F32) * mb
    h = x_ref[...] + jnp.dot(merged.astype(BF16), wo_ref[...], preferred_element_type=F32)
    h_ref[...] = h
    c = _rms(h, g_ref[...])
    c_hi = c.astype(BF16)
    c_ref[...] = c_hi

    c_lo = (c - c_hi.astype(F32)).astype(BF16)
    wr = wr_ref[...]
    wr_hi = wr.astype(BF16)
    wr_lo = (wr - wr_hi.astype(F32)).astype(BF16)
    nt = (((1,), (1,)), ((), ()))
    logits = (lax.dot_general(wr_hi, c_hi, nt, preferred_element_type=F32)
              + lax.dot_general(wr_hi, c_lo, nt, preferred_element_type=F32)
              + lax.dot_general(wr_lo, c_hi, nt, preferred_element_type=F32)) + br_ref[...]
    eid = lax.broadcasted_iota(jnp.int32, (N_EXPERTS, tm), 0).astype(F32)
    work = logits
    vals, sels = [], []
    for _ in range(TOP_K):
        m = jnp.max(work, axis=0, keepdims=True)
        ik = jnp.min(jnp.where(work == m, eid, float(N_EXPERTS)), axis=0, keepdims=True)
        sel = eid == ik
        vals.append(m)
        sels.append(sel)
        work = jnp.where(sel, -jnp.inf, work)
    es = [jnp.exp(v - vals[0]) for v in vals]
    inv = 1.0 / (es[0] + es[1] + es[2] + es[3])
    onehot = jnp.zeros((N_EXPERTS, tm), F32)
    for sel in sels:
        onehot = onehot + jnp.where(sel, 1.0, 0.0)

    prefix = jnp.dot(onehot.astype(BF16), upper_ref[...], preferred_element_type=F32)
    n = jnp.sum(onehot, axis=1, keepdims=True)
    n_al = jnp.ceil(n * (1.0 / ROW_ALIGN)) * ROW_ALIGN
    er = lax.broadcasted_iota(jnp.int32, (N_EXPERTS, N_EXPERTS), 0)
    ec = lax.broadcasted_iota(jnp.int32, (N_EXPERTS, N_EXPERTS), 1)
    run_off = jnp.dot(jnp.where(ec < er, 1.0, 0.0), jnp.broadcast_to(n_al, (N_EXPERTS, LANES)),
                      precision=lax.Precision.HIGHEST, preferred_element_type=F32)
    slot = prefix + run_off[:, 0:1]
    for k in range(TOP_K):
        gate_ref[k:k + 1, :] = es[k] * inv
        lslot_ref[k:k + 1, :] = jnp.sum(jnp.where(sels[k], slot, 0.0), axis=0,
                                        keepdims=True).astype(jnp.int32)
    cnt_ref[0] = jnp.broadcast_to(n, (N_EXPERTS, LANES))


def _outproj(o_a, o_b, sga, sgb, x2, w_out_a, w_out_b, w_out, g_moe, w_router_t, b_router, upper):
    t, d = x2.shape
    tm = TM_PROJ
    const = lambda *shape: pl.BlockSpec(shape, lambda i: (0,) * len(shape))
    row = lambda w: pl.BlockSpec((tm, w), lambda i: (i, 0))
    colk = pl.BlockSpec((TOP_K, tm), lambda i: (0, i))
    return pl.pallas_call(
        _outproj_kernel,
        grid=(t // tm,),
        in_specs=[row(SB_WIDTH), row(SGU_WIDTH), row(d), row(d), row(d),
                  const(SB_WIDTH, d), const(SGU_WIDTH, d), const(d, d),
                  const(1, d), const(N_EXPERTS, d), const(N_EXPERTS, 1), const(tm, tm)],
        out_specs=[row(d), row(d), colk, colk,
                   pl.BlockSpec((1, N_EXPERTS, LANES), lambda i: (i, 0, 0))],
        out_shape=[
            jax.ShapeDtypeStruct((t, d), F32),
            jax.ShapeDtypeStruct((t, d), BF16),
            jax.ShapeDtypeStruct((TOP_K, t), jnp.int32),
            jax.ShapeDtypeStruct((TOP_K, t), F32),
            jax.ShapeDtypeStruct((t // tm, N_EXPERTS, LANES), F32),
        ],
        compiler_params=_cparams(("arbitrary",)),
        name="outproj_router",
    )(o_a, o_b, sga, sgb, x2, w_out_a, w_out_b, w_out, g_moe, w_router_t, b_router, upper)


def _piece_loops(tables, win, fn):
    n_ref, local_ref, global_ref = tables
    for cls, rows in enumerate(PIECE_ROWS):
        head = win * len(PIECE_ROWS) + cls
        count = n_ref[head]

        def piece(i, off=head * MAX_PIECES, rows=rows):
            fn(pl.multiple_of(local_ref[off + i], ROW_ALIGN),
               pl.multiple_of(global_ref[off + i], ROW_ALIGN), rows)

        def pair(i, _, piece=piece):
            piece(2 * i)
            piece(2 * i + 1)
            return 0

        lax.fori_loop(0, lax.shift_right_logical(count, 1), pair, 0)

        @pl.when(lax.bitwise_and(count, 1) == 1)
        def _(piece=piece, count=count):
            piece(count - 1)


def _piece_tables(run, loff, base):
    big = PIECE_ROWS[0]
    nbig = run // big
    nsmall = (run - nbig * big) // ROW_ALIGN
    i = jnp.arange(MAX_PIECES, dtype=jnp.int32)
    experts = jnp.arange(N_EXPERTS, dtype=jnp.int32)

    def flat(cnt, rows, skip):
        cum = jnp.cumsum(cnt, axis=1)
        owner = jnp.minimum(jnp.sum(i[None, :, None] >= cum[:, None, :], axis=-1), N_EXPERTS - 1)
        onehot = owner[:, :, None] == experts
        pick = lambda tab: jnp.sum(jnp.where(onehot, tab[:, None, :], 0), axis=-1)
        j = i[None, :] - pick(cum - cnt)
        return cum[:, -1], pick(loff + skip) + j * rows, pick(base + skip) + j * rows

    nb, lb, gb = flat(nbig, big, 0)
    ns, lsm, gsm = flat(nsmall, ROW_ALIGN, nbig * big)
    pack = lambda a, b: jnp.stack([a, b], axis=1).reshape(-1).astype(jnp.int32)
    return pack(nb, ns), pack(lb, lsm), pack(gb, gsm)


def _dispatch_kernel(pn_ref, pl_ref, pg_ref, pad_end_ref, lslot_ref, c_ref, xs_ref,
                     loc_sc, zero_sc, sem, zsem):
    w = pl.program_id(0)
    tm = c_ref.shape[0]
    tb = zero_sc.shape[0]
    nloc = loc_sc.shape[1]

    @pl.when(w == 0)
    def _():
        zero_sc[...] = jnp.zeros_like(zero_sc)

        def pad_copy(e):
            return pltpu.make_async_copy(
                zero_sc, xs_ref.at[pl.ds(pl.multiple_of(pad_end_ref[e] - tb, tb), tb), :], zsem)

        def nonempty(e):
            prev = jnp.where(e > 0, pad_end_ref[jnp.maximum(e - 1, 0)], 0)
            return pad_end_ref[e] > prev

        def zstart(e, _):
            @pl.when(nonempty(e))
            def _():
                pad_copy(e).start()
            return 0

        def zwait(e, _):
            @pl.when(nonempty(e))
            def _():
                pad_copy(e).wait()
            return 0

        lax.fori_loop(0, N_EXPERTS, zstart, 0)
        lax.fori_loop(0, N_EXPERTS, zwait, 0)

    ls = [lslot_ref[k:k + 1, :] for k in range(TOP_K)]
    buf = lax.bitwise_and(w, 1)
    for r0 in range(0, nloc, SLOT_CHUNK):
        srow = lax.broadcasted_iota(jnp.int32, (SLOT_CHUNK, tm), 0) + r0
        sel = jnp.zeros((SLOT_CHUNK, tm), F32)
        for k in range(TOP_K):
            sel = jnp.where(srow == ls[k], 1.0, sel)
        loc_sc[buf, r0:r0 + SLOT_CHUNK, :] = jnp.dot(sel.astype(BF16), c_ref[...],
                                                     preferred_element_type=F32)

    tables = (pn_ref, pl_ref, pg_ref)

    def piece_copy(slot):
        def build(local_row, global_row, rows):
            return pltpu.make_async_copy(loc_sc.at[slot, pl.ds(local_row, rows), :],
                                         xs_ref.at[pl.ds(global_row, rows), :], sem.at[slot])
        return build

    _piece_loops(tables, w, lambda *a: piece_copy(buf)(*a).start())

    @pl.when(w > 0)
    def _():
        _piece_loops(tables, w - 1, lambda *a: piece_copy(1 - buf)(*a).wait())

    @pl.when(w == pl.num_programs(0) - 1)
    def _():
        _piece_loops(tables, w, lambda *a: piece_copy(buf)(*a).wait())


def _dispatch(pieces, pad_end, lslot, c, n_slots):
    t, d = c.shape
    tm = TM_PROJ
    grid_spec = pltpu.PrefetchScalarGridSpec(
        num_scalar_prefetch=4,
        grid=(t // tm,),
        in_specs=[
            pl.BlockSpec((TOP_K, tm), lambda i, *_: (0, i)),
            pl.BlockSpec((tm, d), lambda i, *_: (i, 0)),
        ],
        out_specs=pl.BlockSpec(memory_space=pl.ANY),
        scratch_shapes=[pltpu.VMEM((2, LOCAL_SLOTS, d), F32),
                        pltpu.VMEM((TB_EXPERT, d), F32),
                        pltpu.SemaphoreType.DMA((2,)), pltpu.SemaphoreType.DMA(())],
    )
    return pl.pallas_call(
        _dispatch_kernel,
        grid_spec=grid_spec,
        out_shape=jax.ShapeDtypeStruct((n_slots, d), F32),
        compiler_params=_cparams(("arbitrary",)),
        name="moe_dispatch",
    )(*pieces, pad_end, lslot, c)


def _expert_kernel(be_ref, nb_ref, xs_ref, w1_ref, w2_ref, b1g_ref, b1l_ref, b2_ref, perm_ref,
                   ys_ref, w1g_sc, w1l_sc, w2_sc):
    i = pl.program_id(0)
    active = i < nb_ref[0]
    fresh = jnp.logical_or(i == 0, be_ref[i] != be_ref[jnp.maximum(i - 1, 0)])

    @pl.when(jnp.logical_and(active, fresh))
    def _():
        half = MXU_DIM // 2
        for cb in range(w1_ref.shape[2] // MXU_DIM):
            blk = w1_ref[0, :, cb * MXU_DIM:(cb + 1) * MXU_DIM].astype(BF16)
            de = jnp.dot(blk, perm_ref[...], preferred_element_type=F32)
            w1g_sc[:, cb * half:(cb + 1) * half] = de[:, :half].astype(BF16)
            w1l_sc[:, cb * half:(cb + 1) * half] = de[:, half:].astype(BF16)
        w2_sc[...] = w2_ref[0].astype(BF16)

    @pl.when(active)
    def _():
        xb = xs_ref[...].astype(BF16)
        hg = jnp.dot(xb, w1g_sc[...], preferred_element_type=F32) + b1g_ref[0]
        hl = jnp.dot(xb, w1l_sc[...], preferred_element_type=F32) + b1l_ref[0]
        xg = jnp.minimum(hg, SWIGLU_LIMIT)
        xl = jnp.clip(hl, -SWIGLU_LIMIT, SWIGLU_LIMIT)
        act = xg * _sigmoid(SWIGLU_ALPHA * xg) * (xl + 1.0)
        ys_ref[...] = jnp.dot(act.astype(BF16), w2_sc[...], preferred_element_type=F32) + b2_ref[0]


def _experts(block_e, n_used, xs, w1, w2, b1g, b1l, b2, perm):
    n_slots, d = xs.shape
    f2 = w1.shape[2]
    f = f2 // 2
    tb = TB_EXPERT
    blk = lambda i, be, nb: (jnp.minimum(i, nb[0] - 1), 0)
    wsel = lambda i, be, nb: (be[i], 0, 0)
    grid_spec = pltpu.PrefetchScalarGridSpec(
        num_scalar_prefetch=2,
        grid=(n_slots // tb,),
        in_specs=[
            pl.BlockSpec((tb, d), blk),
            pl.BlockSpec((1, d, f2), wsel),
            pl.BlockSpec((1, f, d), wsel),
            pl.BlockSpec((1, 1, f), wsel),
            pl.BlockSpec((1, 1, f), wsel),
            pl.BlockSpec((1, 1, d), wsel),
            pl.BlockSpec((MXU_DIM, MXU_DIM), lambda i, be, nb: (0, 0)),
        ],
        out_specs=pl.BlockSpec((tb, d), blk),
        scratch_shapes=[pltpu.VMEM((d, f), BF16), pltpu.VMEM((d, f), BF16),
                        pltpu.VMEM((f, d), BF16)],
    )
    return pl.pallas_call(
        _expert_kernel,
        grid_spec=grid_spec,
        out_shape=jax.ShapeDtypeStruct((n_slots, d), F32),
        compiler_params=_cparams(("arbitrary",)),
        name="moe_experts",
    )(block_e, n_used, xs, w1, w2, b1g, b1l, b2, perm)


def _combine_kernel(pn_ref, pl_ref, pg_ref, lslot_ref, gate_ref, h_ref, p_ref, ys_ref,
                    gp_ref, wg_ref, wp_ref, gf_ref, out_ref, loc_sc, locb_sc, sem,
                    *, final):
    w = pl.program_id(0)
    tm = h_ref.shape[0]
    nloc = loc_sc.shape[0]

    tables = (pn_ref, pl_ref, pg_ref)

    def piece_copy(local_row, global_row, rows):
        return pltpu.make_async_copy(ys_ref.at[pl.ds(global_row, rows), :],
                                     loc_sc.at[pl.ds(local_row, rows), :], sem)

    @pl.when(w == 0)
    def _():
        loc_sc[...] = jnp.zeros_like(loc_sc)
        _piece_loops(tables, w, lambda *a: piece_copy(*a).start())

    _piece_loops(tables, w, lambda *a: piece_copy(*a).wait())
    for c0 in range(0, nloc, SLOT_CHUNK):
        locb_sc[c0:c0 + SLOT_CHUNK, :] = loc_sc[c0:c0 + SLOT_CHUNK, :].astype(BF16)

    @pl.when(w + 1 < pl.num_programs(0))
    def _():
        _piece_loops(tables, w + 1, lambda *a: piece_copy(*a).start())

    lcol = [lslot_ref[:, k:k + 1] for k in range(TOP_K)]
    gcol = [gate_ref[:, k:k + 1] for k in range(TOP_K)]
    h = h_ref[...]
    for c0 in range(0, nloc, SLOT_CHUNK):
        scol = lax.broadcasted_iota(jnp.int32, (tm, SLOT_CHUNK), 1) + c0
        g = jnp.zeros((tm, SLOT_CHUNK), F32)
        for k in range(TOP_K):
            g = jnp.where(scol == lcol[k], gcol[k], g)
        h = h + jnp.dot(g.astype(BF16), locb_sc[c0:c0 + SLOT_CHUNK, :],
                        preferred_element_type=F32)

    r = _rms(h, gp_ref[...]).astype(BF16)
    pg = _sigmoid(jnp.dot(r, wg_ref[...], preferred_element_type=F32))
    pp = jnp.dot(p_ref[...].astype(BF16), wp_ref[...], preferred_element_type=F32)
    h = h + pg * pp
    out_ref[...] = _rms(h, gf_ref[...]) if final else h


def _combine(pieces, lslot_tk, gate_tk, h1, p2, ys, g_ple, w_ple_gate, w_ple_proj, g_final,
             final):
    t, d = h1.shape
    pd = p2.shape[1]
    tm = TM_PROJ
    const = lambda *shape: pl.BlockSpec(shape, lambda i, *_: (0,) * len(shape))
    row = lambda width: pl.BlockSpec((tm, width), lambda i, *_: (i, 0))
    grid_spec = pltpu.PrefetchScalarGridSpec(
        num_scalar_prefetch=3,
        grid=(t // tm,),
        in_specs=[row(TOP_K), row(TOP_K), row(d), row(pd),
                  pl.BlockSpec(memory_space=pl.ANY),
                  const(1, d), const(d, d), const(pd, d), const(1, d)],
        out_specs=row(d),
        scratch_shapes=[pltpu.VMEM((LOCAL_SLOTS, d), F32), pltpu.VMEM((LOCAL_SLOTS, d), BF16),
                        pltpu.SemaphoreType.DMA(())],
    )
    return pl.pallas_call(
        functools.partial(_combine_kernel, final=final),
        grid_spec=grid_spec,
        out_shape=jax.ShapeDtypeStruct((t, d), F32),
        compiler_params=_cparams(("arbitrary",)),
        name="combine_ple",
    )(*pieces, lslot_tk, gate_tk, h1, p2, ys, g_ple, w_ple_gate, w_ple_proj, g_final)


def kernel(x, p, g_mix, w_in, w_out_a, w_out_b, w_out, sgu_norm, sgu_w, sgu_b, g_moe,
           w_router, b_router, w1, b1, w2, b2, g_ple, w_ple_gate, w_ple_proj, g_final):
    b, s, d = x.shape
    depth = w_in.shape[0]
    t = b * s
    assert s % (TQ_ATTN * ATTN_TILES_PER_STEP) == 0 and t % TM_PROJ == 0

    kk = jnp.arange(KC_ATTN)
    ntri = -(kk[:, None] >= kk[None, :]).astype(BF16)
    tt = jnp.arange(TM_PROJ)
    upper = (tt[:, None] < tt[None, :]).astype(BF16)
    rr = jnp.arange(MXU_DIM)[:, None]
    cc = jnp.arange(MXU_DIM)[None, :]
    half = MXU_DIM // 2
    perm = (rr == jnp.where(cc < half, 2 * cc, 2 * (cc - half) + 1)).astype(BF16)

    tb = TB_EXPERT
    n_win = t // TM_PROJ
    n_blocks = -(-(t * TOP_K + n_win * N_EXPERTS * (ROW_ALIGN - 1) + N_EXPERTS * (tb - 1)) // tb)
    n_slots = n_blocks * tb

    h = x.reshape(t, d)
    for i in range(depth):
        sgu_bias = jnp.repeat(sgu_b[i].T, SGU_GROUP_DIM, axis=1)
        qkv, o_b, sga, sgb = _inproj(h, g_mix[i][None], w_in[i].astype(BF16), sgu_w[i],
                                     sgu_bias, sgu_norm[i][None])
        o_a = _attention(qkv.reshape(b, s, 3 * SB_WIDTH), ntri).reshape(t, SB_WIDTH)
        h1, c, lslot, gate, cnt = _outproj(
            o_a, o_b, sga, sgb, h, w_out_a[i].astype(BF16), w_out_b[i].astype(BF16),
            w_out[i].astype(BF16), g_moe[i][None], w_router[i].T, b_router[i][:, None], upper)

        counts = cnt[:, :, 0].astype(jnp.int32)
        run = ((counts + ROW_ALIGN - 1) // ROW_ALIGN) * ROW_ALIGN
        loff = jnp.cumsum(run, axis=1) - run
        region = ((jnp.sum(run, axis=0) + tb - 1) // tb) * tb
        pad_end = jnp.cumsum(region).astype(jnp.int32)
        base = (pad_end - region)[None, :] + jnp.cumsum(run, axis=0) - run
        pieces = _piece_tables(run, loff, base)
        starts = jnp.arange(n_blocks, dtype=jnp.int32) * tb
        block_e = jnp.minimum(jnp.sum(starts[:, None] >= pad_end[None, :], axis=1),
                              N_EXPERTS - 1).astype(jnp.int32)
        n_used = (pad_end[-1:] // tb).astype(jnp.int32)

        xs = _dispatch(pieces, pad_end, lslot, c, n_slots)
        ys = _experts(block_e, n_used, xs, w1[i], w2[i],
                      b1[i][:, None, 0::2], b1[i][:, None, 1::2], b2[i][:, None, :], perm)
        h = _combine(pieces, lslot.T, gate.T, h1, p[i].reshape(t, -1), ys, g_ple[i][None],
                     w_ple_gate[i].astype(BF16), w_ple_proj[i].astype(BF16), g_final[None],
                     final=(i == depth - 1))
    return h.reshape(b, s, d)
```

```python
import functools
import math

import jax
import jax.numpy as jnp
from jax import lax
from jax.experimental import pallas as pl
from jax.experimental.pallas import tpu as pltpu

F32 = jnp.float32
BF16 = jnp.bfloat16

EPS = 1e-6
CHUNK = 64
SB_HEADS = 8
SB_HEAD_DIM = 64
SB_WIDTH = SB_HEADS * SB_HEAD_DIM
SGU_GROUPS = 8
SGU_WIDTH = 512
SGU_GROUP_DIM = SGU_WIDTH // SGU_GROUPS
SGU_BLOCK = 128
N_EXPERTS = 32
TOP_K = 4
SWIGLU_ALPHA = 1.702
SWIGLU_LIMIT = 7.0
LOG2E = 1.4426950408889634
UNDERFLOW_LOG2 = -160.0

LANES = 128
MXU_DIM = 256
VMEM_LIMIT_BYTES = 56 * 1024 * 1024

TM_PROJ = 512
TQ_ATTN = 512
ATTN_TILES_PER_STEP = 2
KC_ATTN = MXU_DIM
TB_EXPERT = 512
ROW_ALIGN = 8
PIECE_ROWS = (32, ROW_ALIGN)
SLOT_CHUNK = 256
LOCAL_SLOTS = -(-(TM_PROJ * TOP_K + N_EXPERTS * (ROW_ALIGN - 1)) // SLOT_CHUNK) * SLOT_CHUNK
MAX_PIECES = max(LOCAL_SLOTS // PIECE_ROWS[0],
                 N_EXPERTS * (PIECE_ROWS[0] // PIECE_ROWS[1] - 1))


def _cparams(sem):
    return pltpu.CompilerParams(dimension_semantics=sem,
                                vmem_limit_bytes=VMEM_LIMIT_BYTES)


def _rms(x, g):
    ms = jnp.mean(x * x, axis=-1, keepdims=True)
    return x * lax.rsqrt(ms + EPS) * g


def _sigmoid(x):
    return 0.5 * jnp.tanh(0.5 * x) + 0.5


def _inproj_kernel(x_ref, g_ref, w_ref, sw_ref, sbias_ref, sn_ref,
                   qkv_ref, ob_ref, sga_ref, sgb_ref):
    tm = x_ref.shape[0]
    a = _rms(x_ref[...], g_ref[...]).astype(BF16)

    def proj(c0, width):
        return jnp.dot(a, w_ref[:, c0:c0 + width], preferred_element_type=F32)

    qkv_ref[:, 0:SB_WIDTH] = (proj(0, SB_WIDTH) * (LOG2E / math.sqrt(SB_HEAD_DIM))).astype(BF16)
    qkv_ref[:, SB_WIDTH:2 * SB_WIDTH] = proj(SB_WIDTH, SB_WIDTH).astype(BF16)
    qkv_ref[:, 2 * SB_WIDTH:3 * SB_WIDTH] = proj(2 * SB_WIDTH, SB_WIDTH).astype(BF16)

    c_u = 3 * SB_WIDTH
    gu = jax.nn.gelu(proj(c_u, SGU_WIDTH))
    gv = jax.nn.gelu(proj(c_u + SGU_WIDTH, SGU_WIDTH))
    mu = jnp.mean(gv, axis=-1, keepdims=True)
    d = gv - mu
    var = jnp.mean(d * d, axis=-1, keepdims=True)
    vn = (d * lax.rsqrt(var + EPS) * sn_ref[...]).astype(BF16)

    rr = lax.broadcasted_iota(jnp.int32, (SGU_BLOCK, SGU_BLOCK), 0) // CHUNK
    cc = lax.broadcasted_iota(jnp.int32, (SGU_BLOCK, SGU_BLOCK), 1) // CHUNK
    cmask = cc <= rr
    lane = lax.broadcasted_iota(jnp.int32, (SGU_BLOCK, LANES), 1)
    lo_mask = lane < SGU_GROUP_DIM
    wcat = []
    for j in range(SGU_GROUPS // 2):
        w0 = jnp.where(cmask, sw_ref[2 * j], 0.0).astype(BF16)
        w1 = jnp.where(cmask, sw_ref[2 * j + 1], 0.0).astype(BF16)
        wcat.append(jnp.concatenate([w0, w1], axis=1))

    zero = jnp.zeros((), BF16)
    for blk in range(tm // SGU_BLOCK):
        r0 = blk * SGU_BLOCK
        for j in range(SGU_GROUPS // 2):
            vp = vn[r0:r0 + SGU_BLOCK, j * LANES:(j + 1) * LANES]
            rhs = jnp.concatenate([jnp.where(lo_mask, vp, zero),
                                   jnp.where(lo_mask, zero, vp)], axis=0)
            sv = jnp.dot(wcat[j], rhs, preferred_element_type=F32)
            sv = sv + sbias_ref[:, j * LANES:(j + 1) * LANES]
            ob_ref[r0:r0 + SGU_BLOCK, j * LANES:(j + 1) * LANES] = (
                gu[r0:r0 + SGU_BLOCK, j * LANES:(j + 1) * LANES] * sv).astype(BF16)

    c_g = c_u + 2 * SGU_WIDTH
    d_model = sga_ref.shape[1]
    for c0 in range(0, d_model, 512):
        sga_ref[:, c0:c0 + 512] = _sigmoid(proj(c_g + c0, 512)).astype(BF16)
        sgb_ref[:, c0:c0 + 512] = _sigmoid(proj(c_g + d_model + c0, 512)).astype(BF16)


def _inproj(x2, g_mix, w_in, sgu_w, sgu_bias, sgu_norm):
    t, d = x2.shape
    ncol = w_in.shape[1]
    tm = TM_PROJ
    const = lambda *shape: pl.BlockSpec(shape, lambda i: (0,) * len(shape))
    return pl.pallas_call(
        _inproj_kernel,
        grid=(t // tm,),
        in_specs=[
            pl.BlockSpec((tm, d), lambda i: (i, 0)),
            const(1, d),
            const(d, ncol),
            const(SGU_GROUPS, SGU_BLOCK, SGU_BLOCK),
            const(SGU_BLOCK, SGU_WIDTH),
            const(1, SGU_WIDTH),
        ],
        out_specs=[
            pl.BlockSpec((tm, 3 * SB_WIDTH), lambda i: (i, 0)),
            pl.BlockSpec((tm, SGU_WIDTH), lambda i: (i, 0)),
            pl.BlockSpec((tm, d), lambda i: (i, 0)),
            pl.BlockSpec((tm, d), lambda i: (i, 0)),
        ],
        out_shape=[
            jax.ShapeDtypeStruct((t, 3 * SB_WIDTH), BF16),
            jax.ShapeDtypeStruct((t, SGU_WIDTH), BF16),
            jax.ShapeDtypeStruct((t, d), BF16),
            jax.ShapeDtypeStruct((t, d), BF16),
        ],
        compiler_params=_cparams(("arbitrary",)),
        name="inproj_sgu",
    )(x2, g_mix, w_in, sgu_w, sgu_bias, sgu_norm)


def _attn_kernel(q_ref, k_ref, v_ref, ntri_ref, o_ref, carry_sc, acc_sc):
    tq = TQ_ATTN
    kc = KC_ATTN
    head0 = lax.broadcasted_iota(jnp.int32, (kc, LANES), 1) < SB_HEAD_DIM
    zero = jnp.zeros((), BF16)
    tri2 = (lax.broadcasted_iota(jnp.int32, (2 * kc, kc), 1)
            < lax.bitwise_and(lax.broadcasted_iota(jnp.int32, (2 * kc, kc), 0), kc - 1))

    def tile(sub, _):
        qi = pl.program_id(2) * (q_ref.shape[1] // tq) + sub
        r0 = pl.multiple_of(sub * tq, tq)
        q = q_ref[0, pl.ds(r0, tq), :]
        qs = [jnp.concatenate([jnp.where(head0, q[rb * kc:(rb + 1) * kc], zero),
                               jnp.where(head0, zero, q[rb * kc:(rb + 1) * kc])], axis=0)
              for rb in range(tq // kc)]
        carry_sc[...] = jnp.zeros_like(carry_sc)
        acc_sc[...] = jnp.zeros_like(acc_sc)

        def scores(j, rb, causal, keep):
            k0 = pl.multiple_of(j * kc, kc)
            z = lax.dot_general(qs[rb], k_ref[0, pl.ds(k0, kc), :], (((1,), (1,)), ((), ())),
                                preferred_element_type=F32)
            sp = jnp.maximum(z, 0.0) + jnp.log2(1.0 + jnp.exp2(-jnp.abs(z)))
            if causal:
                sp = jnp.where(tri2, sp, 0.0)
            if keep is not None:
                sp = jnp.where(keep, sp, 0.0)
            cum = jnp.dot(sp.astype(BF16), ntri_ref[...], preferred_element_type=F32)
            return k0, rb, causal, keep, z, cum

        def weigh(part):
            k0, rb, causal, keep, z, cum = part
            carry = carry_sc[rb]
            w = jnp.exp2(z + cum + jnp.concatenate([carry] * (kc // LANES), axis=1))
            carry_sc[rb] = carry + jnp.broadcast_to(cum[:, 0:1], carry.shape)
            if causal:
                w = jnp.where(tri2, w, 0.0)
            if keep is not None:
                w = jnp.where(keep, w, 0.0)
            acc_sc[rb] += jnp.dot(w.astype(BF16), v_ref[0, pl.ds(k0, kc), :],
                                  preferred_element_type=F32)

        ndiag = tq // kc
        npast = qi * ndiag

        def past_part(rb, p):
            return scores(jnp.maximum(npast - 1 - p, 0), rb, False, p < npast)

        head = []
        for c in reversed(range(ndiag)):
            head.append(scores(npast + c, c, True, None))
            head += [scores(npast + c, rb, False, None) for rb in range(c + 1, ndiag)]
        for rb in range(ndiag - 1):
            head += [past_part(rb, p) for p in range(ndiag - 1 - rb)]
        for part in head:
            weigh(part)

        def live(t):
            return jnp.logical_and(t < npast, jnp.max(carry_sc[...]) > UNDERFLOW_LOG2)

        def past(t):
            for part in [past_part(rb, ndiag - 1 - rb + t) for rb in range(ndiag)]:
                weigh(part)
            return t + 1

        lax.while_loop(live, past, 0)
        for rb in range(ndiag):
            acc = acc_sc[rb]
            o_ref[0, pl.ds(r0 + rb * kc, kc), :] = jnp.where(head0, acc[:kc],
                                                             acc[kc:]).astype(o_ref.dtype)
        return 0

    lax.fori_loop(0, q_ref.shape[1] // tq, tile, 0)


def _attention(qkv3, ntri):
    b, s, _ = qkv3.shape
    tq = TQ_ATTN * ATTN_TILES_PER_STEP
    npair = SB_WIDTH // LANES
    return pl.pallas_call(
        _attn_kernel,
        grid=(b, npair, s // tq),
        in_specs=[
            pl.BlockSpec((1, tq, LANES), lambda bi, p, i: (bi, i, p)),
            pl.BlockSpec((1, s, LANES), lambda bi, p, i: (bi, 0, npair + p)),
            pl.BlockSpec((1, s, LANES), lambda bi, p, i: (bi, 0, 2 * npair + p)),
            pl.BlockSpec((KC_ATTN, KC_ATTN), lambda bi, p, i: (0, 0)),
        ],
        out_specs=pl.BlockSpec((1, tq, LANES), lambda bi, p, i: (bi, i, p)),
        out_shape=jax.ShapeDtypeStruct((b, s, SB_WIDTH), BF16),
        scratch_shapes=[pltpu.VMEM((TQ_ATTN // KC_ATTN, 2 * KC_ATTN, LANES), F32),
                        pltpu.VMEM((TQ_ATTN // KC_ATTN, 2 * KC_ATTN, LANES), F32)],
        compiler_params=_cparams(("arbitrary", "arbitrary", "arbitrary")),
        name="stickbreak_attn",
    )(qkv3, qkv3, qkv3, ntri)


def _outproj_kernel(oa_ref, ob_ref, sga_ref, sgb_ref, x_ref, woa_ref, wob_ref, wo_ref,
                    g_ref, wr_ref, br_ref, upper_ref,
                    h_ref, c_ref, lslot_ref, gate_ref, cnt_ref):
    tm = x_ref.shape[0]
    ma = jnp.dot(oa_ref[...], woa_ref[...], preferred_element_type=F32)
    mb = jnp.dot(ob_ref[...], wob_ref[...], preferred_element_type=F32)
    merged = sga_ref[...].astype(F32) * ma + sgb_ref[...].astype(F32) * mb
    h = x_ref[...] + jnp.dot(merged.astype(BF16), wo_ref[...], preferred_element_type=F32)
    h_ref[...] = h
    c = _rms(h, g_ref[...])
    c_hi = c.astype(BF16)
    c_ref[...] = c_hi

    c_lo = (c - c_hi.astype(F32)).astype(BF16)
    wr = wr_ref[...]
    wr_hi = wr.astype(BF16)
    wr_lo = (wr - wr_hi.astype(F32)).astype(BF16)
    nt = (((1,), (1,)), ((), ()))
    logits = (lax.dot_general(wr_hi, c_hi, nt, preferred_element_type=F32)
              + lax.dot_general(wr_hi, c_lo, nt, preferred_element_type=F32)
              + lax.dot_general(wr_lo, c_hi, nt, preferred_element_type=F32)) + br_ref[...]
    eid = lax.broadcasted_iota(jnp.int32, (N_EXPERTS, tm), 0).astype(F32)
    work = logits
    vals, sels = [], []
    for _ in range(TOP_K):
        m = jnp.max(work, axis=0, keepdims=True)
        ik = jnp.min(jnp.where(work == m, eid, float(N_EXPERTS)), axis=0, keepdims=True)
        sel = eid == ik
        vals.append(m)
        sels.append(sel)
        work = jnp.where(sel, -jnp.inf, work)
    es = [jnp.exp(v - vals[0]) for v in vals]
    inv = 1.0 / (es[0] + es[1] + es[2] + es[3])
    onehot = jnp.zeros((N_EXPERTS, tm), F32)
    for sel in sels:
        onehot = onehot + jnp.where(sel, 1.0, 0.0)

    prefix = jnp.dot(onehot.astype(BF16), upper_ref[...], preferred_element_type=F32)
    n = jnp.sum(onehot, axis=1, keepdims=True)
    n_al = jnp.ceil(n * (1.0 / ROW_ALIGN)) * ROW_ALIGN
    er = lax.broadcasted_iota(jnp.int32, (N_EXPERTS, N_EXPERTS), 0)
    ec = lax.broadcasted_iota(jnp.int32, (N_EXPERTS, N_EXPERTS), 1)
    run_off = jnp.dot(jnp.where(ec < er, 1.0, 0.0), jnp.broadcast_to(n_al, (N_EXPERTS, LANES)),
                      precision=lax.Precision.HIGHEST, preferred_element_type=F32)
    slot = prefix + run_off[:, 0:1]
    for k in range(TOP_K):
        gate_ref[k:k + 1, :] = es[k] * inv
        lslot_ref[k:k + 1, :] = jnp.sum(jnp.where(sels[k], slot, 0.0), axis=0,
                                        keepdims=True).astype(jnp.int32)
    cnt_ref[0] = jnp.broadcast_to(n, (N_EXPERTS, LANES))


def _outproj(o_a, o_b, sga, sgb, x2, w_out_a, w_out_b, w_out, g_moe, w_router_t, b_router, upper):
    t, d = x2.shape
    tm = TM_PROJ
    const = lambda *shape: pl.BlockSpec(shape, lambda i: (0,) * len(shape))
    row = lambda w: pl.BlockSpec((tm, w), lambda i: (i, 0))
    colk = pl.BlockSpec((TOP_K, tm), lambda i: (0, i))
    return pl.pallas_call(
        _outproj_kernel,
        grid=(t // tm,),
        in_specs=[row(SB_WIDTH), row(SGU_WIDTH), row(d), row(d), row(d),
                  const(SB_WIDTH, d), const(SGU_WIDTH, d), const(d, d),
                  const(1, d), const(N_EXPERTS, d), const(N_EXPERTS, 1), const(tm, tm)],
        out_specs=[row(d), row(d), colk, colk,
                   pl.BlockSpec((1, N_EXPERTS, LANES), lambda i: (i, 0, 0))],
        out_shape=[
            jax.ShapeDtypeStruct((t, d), F32),
            jax.ShapeDtypeStruct((t, d), BF16),
            jax.ShapeDtypeStruct((TOP_K, t), jnp.int32),
            jax.ShapeDtypeStruct((TOP_K, t), F32),
            jax.ShapeDtypeStruct((t // tm, N_EXPERTS, LANES), F32),
        ],
        compiler_params=_cparams(("arbitrary",)),
        name="outproj_router",
    )(o_a, o_b, sga, sgb, x2, w_out_a, w_out_b, w_out, g_moe, w_router_t, b_router, upper)


def _piece_loops(tables, win, fn):
    n_ref, local_ref, global_ref = tables
    for cls, rows in enumerate(PIECE_ROWS):
        head = win * len(PIECE_ROWS) + cls
        count = n_ref[head]

        def piece(i, off=head * MAX_PIECES, rows=rows):
            fn(pl.multiple_of(local_ref[off + i], ROW_ALIGN),
               pl.multiple_of(global_ref[off + i], ROW_ALIGN), rows)

        def pair(i, _, piece=piece):
            piece(2 * i)
            piece(2 * i + 1)
            return 0

        lax.fori_loop(0, lax.shift_right_logical(count, 1), pair, 0)

        @pl.when(lax.bitwise_and(count, 1) == 1)
        def _(piece=piece, count=count):
            piece(count - 1)


def _piece_tables(run, loff, base):
    big = PIECE_ROWS[0]
    nbig = run // big
    nsmall = (run - nbig * big) // ROW_ALIGN
    i = jnp.arange(MAX_PIECES, dtype=jnp.int32)
    experts = jnp.arange(N_EXPERTS, dtype=jnp.int32)

    def flat(cnt, rows, skip):
        cum = jnp.cumsum(cnt, axis=1)
        owner = jnp.minimum(jnp.sum(i[None, :, None] >= cum[:, None, :], axis=-1), N_EXPERTS - 1)
        onehot = owner[:, :, None] == experts
        pick = lambda tab: jnp.sum(jnp.where(onehot, tab[:, None, :], 0), axis=-1)
        j = i[None, :] - pick(cum - cnt)
        return cum[:, -1], pick(loff + skip) + j * rows, pick(base + skip) + j * rows

    nb, lb, gb = flat(nbig, big, 0)
    ns, lsm, gsm = flat(nsmall, ROW_ALIGN, nbig * big)
    pack = lambda a, b: jnp.stack([a, b], axis=1).reshape(-1).astype(jnp.int32)
    return pack(nb, ns), pack(lb, lsm), pack(gb, gsm)


def _dispatch_kernel(pn_ref, pl_ref, pg_ref, pad_end_ref, lslot_ref, c_ref, xs_ref,
                     loc_sc, zero_sc, sem, zsem):
    w = pl.program_id(0)
    tm = c_ref.shape[0]
    tb = zero_sc.shape[0]
    nloc = loc_sc.shape[1]

    @pl.when(w == 0)
    def _():
        zero_sc[...] = jnp.zeros_like(zero_sc)

        def pad_copy(e):
            return pltpu.make_async_copy(
                zero_sc, xs_ref.at[pl.ds(pl.multiple_of(pad_end_ref[e] - tb, tb), tb), :], zsem)

        def nonempty(e):
            prev = jnp.where(e > 0, pad_end_ref[jnp.maximum(e - 1, 0)], 0)
            return pad_end_ref[e] > prev

        def zstart(e, _):
            @pl.when(nonempty(e))
            def _():
                pad_copy(e).start()
            return 0

        def zwait(e, _):
            @pl.when(nonempty(e))
            def _():
                pad_copy(e).wait()
            return 0

        lax.fori_loop(0, N_EXPERTS, zstart, 0)
        lax.fori_loop(0, N_EXPERTS, zwait, 0)

    ls = [lslot_ref[k:k + 1, :] for k in range(TOP_K)]
    buf = lax.bitwise_and(w, 1)
    for r0 in range(0, nloc, SLOT_CHUNK):
        srow = lax.broadcasted_iota(jnp.int32, (SLOT_CHUNK, tm), 0) + r0
        sel = jnp.zeros((SLOT_CHUNK, tm), F32)
        for k in range(TOP_K):
            sel = jnp.where(srow == ls[k], 1.0, sel)
        loc_sc[buf, r0:r0 + SLOT_CHUNK, :] = jnp.dot(sel.astype(BF16), c_ref[...],
                                                     preferred_element_type=F32)

    tables = (pn_ref, pl_ref, pg_ref)

    def piece_copy(slot):
        def build(local_row, global_row, rows):
            return pltpu.make_async_copy(loc_sc.at[slot, pl.ds(local_row, rows), :],
                                         xs_ref.at[pl.ds(global_row, rows), :], sem.at[slot])
        return build

    _piece_loops(tables, w, lambda *a: piece_copy(buf)(*a).start())

    @pl.when(w > 0)
    def _():
        _piece_loops(tables, w - 1, lambda *a: piece_copy(1 - buf)(*a).wait())

    @pl.when(w == pl.num_programs(0) - 1)
    def _():
        _piece_loops(tables, w, lambda *a: piece_copy(buf)(*a).wait())


def _dispatch(pieces, pad_end, lslot, c, n_slots):
    t, d = c.shape
    tm = TM_PROJ
    grid_spec = pltpu.PrefetchScalarGridSpec(
        num_scalar_prefetch=4,
        grid=(t // tm,),
        in_specs=[
            pl.BlockSpec((TOP_K, tm), lambda i, *_: (0, i)),
            pl.BlockSpec((tm, d), lambda i, *_: (i, 0)),
        ],
        out_specs=pl.BlockSpec(memory_space=pl.ANY),
        scratch_shapes=[pltpu.VMEM((2, LOCAL_SLOTS, d), F32),
                        pltpu.VMEM((TB_EXPERT, d), F32),
                        pltpu.SemaphoreType.DMA((2,)), pltpu.SemaphoreType.DMA(())],
    )
    return pl.pallas_call(
        _dispatch_kernel,
        grid_spec=grid_spec,
        out_shape=jax.ShapeDtypeStruct((n_slots, d), F32),
        compiler_params=_cparams(("arbitrary",)),
        name="moe_dispatch",
    )(*pieces, pad_end, lslot, c)


def _expert_kernel(be_ref, nb_ref, half_ref, xs_ref, w1_ref, w2_ref, b1g_ref, b1l_ref, b2_ref,
                   perm_ref, ys_ref, w1g_sc, w1l_sc, w2_sc):
    i = pl.program_id(0)
    tb = xs_ref.shape[0]
    active = i < nb_ref[0]
    fresh = jnp.logical_or(i == 0, be_ref[i] != be_ref[jnp.maximum(i - 1, 0)])

    @pl.when(jnp.logical_and(active, fresh))
    def _():
        half = MXU_DIM // 2
        for cb in range(w1_ref.shape[2] // MXU_DIM):
            blk = w1_ref[0, :, cb * MXU_DIM:(cb + 1) * MXU_DIM].astype(BF16)
            de = jnp.dot(blk, perm_ref[...], preferred_element_type=F32)
            w1g_sc[:, cb * half:(cb + 1) * half] = de[:, :half].astype(BF16)
            w1l_sc[:, cb * half:(cb + 1) * half] = de[:, half:].astype(BF16)
        w2_sc[...] = w2_ref[0].astype(BF16)

    def mlp(nrows):
        xb = xs_ref[0:nrows, :].astype(BF16)
        hg = jnp.dot(xb, w1g_sc[...], preferred_element_type=F32) + b1g_ref[0]
        hl = jnp.dot(xb, w1l_sc[...], preferred_element_type=F32) + b1l_ref[0]
        xg = jnp.minimum(hg, SWIGLU_LIMIT)
        xl = jnp.clip(hl, -SWIGLU_LIMIT, SWIGLU_LIMIT)
        act = xg * _sigmoid(SWIGLU_ALPHA * xg) * (xl + 1.0)
        ys_ref[0:nrows, :] = (jnp.dot(act.astype(BF16), w2_sc[...], preferred_element_type=F32)
                              + b2_ref[0])

    half_full = half_ref[i] == 1

    @pl.when(jnp.logical_and(active, jnp.logical_not(half_full)))
    def _():
        mlp(tb)

    @pl.when(jnp.logical_and(active, half_full))
    def _():
        mlp(tb // 2)
        ys_ref[tb // 2:, :] = jnp.zeros((tb - tb // 2, ys_ref.shape[1]), ys_ref.dtype)


def _experts(block_e, n_used, half_full, xs, w1, w2, b1g, b1l, b2, perm):
    n_slots, d = xs.shape
    f2 = w1.shape[2]
    f = f2 // 2
    tb = TB_EXPERT
    blk = lambda i, be, nb, hf: (jnp.minimum(i, nb[0] - 1), 0)
    wsel = lambda i, be, nb, hf: (be[i], 0, 0)
    grid_spec = pltpu.PrefetchScalarGridSpec(
        num_scalar_prefetch=3,
        grid=(n_slots // tb,),
        in_specs=[
            pl.BlockSpec((tb, d), blk),
            pl.BlockSpec((1, d, f2), wsel),
            pl.BlockSpec((1, f, d), wsel),
            pl.BlockSpec((1, 1, f), wsel),
            pl.BlockSpec((1, 1, f), wsel),
            pl.BlockSpec((1, 1, d), wsel),
            pl.BlockSpec((MXU_DIM, MXU_DIM), lambda i, be, nb, hf: (0, 0)),
        ],
        out_specs=pl.BlockSpec((tb, d), blk),
        scratch_shapes=[pltpu.VMEM((d, f), BF16), pltpu.VMEM((d, f), BF16),
                        pltpu.VMEM((f, d), BF16)],
    )
    return pl.pallas_call(
        _expert_kernel,
        grid_spec=grid_spec,
        out_shape=jax.ShapeDtypeStruct((n_slots, d), F32),
        compiler_params=_cparams(("arbitrary",)),
        name="moe_experts",
    )(block_e, n_used, half_full, xs, w1, w2, b1g, b1l, b2, perm)


def _combine_kernel(pn_ref, pl_ref, pg_ref, lslot_ref, gate_ref, h_ref, p_ref, ys_ref,
                    gp_ref, wg_ref, wp_ref, gf_ref, out_ref, loc_sc, locb_sc, sem,
                    *, final):
    w = pl.program_id(0)
    tm = h_ref.shape[0]
    nloc = loc_sc.shape[0]

    tables = (pn_ref, pl_ref, pg_ref)

    def piece_copy(local_row, global_row, rows):
        return pltpu.make_async_copy(ys_ref.at[pl.ds(global_row, rows), :],
                                     loc_sc.at[pl.ds(local_row, rows), :], sem)

    @pl.when(w == 0)
    def _():
        loc_sc[...] = jnp.zeros_like(loc_sc)
        _piece_loops(tables, w, lambda *a: piece_copy(*a).start())

    _piece_loops(tables, w, lambda *a: piece_copy(*a).wait())
    for c0 in range(0, nloc, SLOT_CHUNK):
        locb_sc[c0:c0 + SLOT_CHUNK, :] = loc_sc[c0:c0 + SLOT_CHUNK, :].astype(BF16)

    @pl.when(w + 1 < pl.num_programs(0))
    def _():
        _piece_loops(tables, w + 1, lambda *a: piece_copy(*a).start())

    lcol = [lslot_ref[:, k:k + 1] for k in range(TOP_K)]
    gcol = [gate_ref[:, k:k + 1] for k in range(TOP_K)]
    h = h_ref[...]
    for c0 in range(0, nloc, SLOT_CHUNK):
        scol = lax.broadcasted_iota(jnp.int32, (tm, SLOT_CHUNK), 1) + c0
        g = jnp.zeros((tm, SLOT_CHUNK), F32)
        for k in range(TOP_K):
            g = jnp.where(scol == lcol[k], gcol[k], g)
        h = h + jnp.dot(g.astype(BF16), locb_sc[c0:c0 + SLOT_CHUNK, :],
                        preferred_element_type=F32)

    r = _rms(h, gp_ref[...]).astype(BF16)
    pg = _sigmoid(jnp.dot(r, wg_ref[...], preferred_element_type=F32))
    pp = jnp.dot(p_ref[...].astype(BF16), wp_ref[...], preferred_element_type=F32)
    h = h + pg * pp
    out_ref[...] = _rms(h, gf_ref[...]) if final else h


def _combine(pieces, lslot_tk, gate_tk, h1, p2, ys, g_ple, w_ple_gate, w_ple_proj, g_final,
             final):
    t, d = h1.shape
    pd = p2.shape[1]
    tm = TM_PROJ
    const = lambda *shape: pl.BlockSpec(shape, lambda i, *_: (0,) * len(shape))
    row = lambda width: pl.BlockSpec((tm, width), lambda i, *_: (i, 0))
    grid_spec = pltpu.PrefetchScalarGridSpec(
        num_scalar_prefetch=3,
        grid=(t // tm,),
        in_specs=[row(TOP_K), row(TOP_K), row(d), row(pd),
                  pl.BlockSpec(memory_space=pl.ANY),
                  const(1, d), const(d, d), const(pd, d), const(1, d)],
        out_specs=row(d),
        scratch_shapes=[pltpu.VMEM((LOCAL_SLOTS, d), F32), pltpu.VMEM((LOCAL_SLOTS, d), BF16),
                        pltpu.SemaphoreType.DMA(())],
    )
    return pl.pallas_call(
        functools.partial(_combine_kernel, final=final),
        grid_spec=grid_spec,
        out_shape=jax.ShapeDtypeStruct((t, d), F32),
        compiler_params=_cparams(("arbitrary",)),
        name="combine_ple",
    )(*pieces, lslot_tk, gate_tk, h1, p2, ys, g_ple, w_ple_gate, w_ple_proj, g_final)


def kernel(x, p, g_mix, w_in, w_out_a, w_out_b, w_out, sgu_norm, sgu_w, sgu_b, g_moe,
           w_router, b_router, w1, b1, w2, b2, g_ple, w_ple_gate, w_ple_proj, g_final):
    b, s, d = x.shape
    depth = w_in.shape[0]
    t = b * s
    assert s % (TQ_ATTN * ATTN_TILES_PER_STEP) == 0 and t % TM_PROJ == 0

    kk = jnp.arange(KC_ATTN)
    ntri = -(kk[:, None] >= kk[None, :]).astype(BF16)
    tt = jnp.arange(TM_PROJ)
    upper = (tt[:, None] < tt[None, :]).astype(BF16)
    rr = jnp.arange(MXU_DIM)[:, None]
    cc = jnp.arange(MXU_DIM)[None, :]
    half = MXU_DIM // 2
    perm = (rr == jnp.where(cc < half, 2 * cc, 2 * (cc - half) + 1)).astype(BF16)

    tb = TB_EXPERT
    n_win = t // TM_PROJ
    n_blocks = -(-(t * TOP_K + n_win * N_EXPERTS * (ROW_ALIGN - 1) + N_EXPERTS * (tb - 1)) // tb)
    n_slots = n_blocks * tb

    h = x.reshape(t, d)
    for i in range(depth):
        sgu_bias = jnp.repeat(sgu_b[i].T, SGU_GROUP_DIM, axis=1)
        qkv, o_b, sga, sgb = _inproj(h, g_mix[i][None], w_in[i].astype(BF16), sgu_w[i],
                                     sgu_bias, sgu_norm[i][None])
        o_a = _attention(qkv.reshape(b, s, 3 * SB_WIDTH), ntri).reshape(t, SB_WIDTH)
        h1, c, lslot, gate, cnt = _outproj(
            o_a, o_b, sga, sgb, h, w_out_a[i].astype(BF16), w_out_b[i].astype(BF16),
            w_out[i].astype(BF16), g_moe[i][None], w_router[i].T, b_router[i][:, None], upper)

        counts = cnt[:, :, 0].astype(jnp.int32)
        run = ((counts + ROW_ALIGN - 1) // ROW_ALIGN) * ROW_ALIGN
        loff = jnp.cumsum(run, axis=1) - run
        region = ((jnp.sum(run, axis=0) + tb - 1) // tb) * tb
        pad_end = jnp.cumsum(region).astype(jnp.int32)
        base = (pad_end - region)[None, :] + jnp.cumsum(run, axis=0) - run
        pieces = _piece_tables(run, loff, base)
        starts = jnp.arange(n_blocks, dtype=jnp.int32) * tb
        block_e = jnp.minimum(jnp.sum(starts[:, None] >= pad_end[None, :], axis=1),
                              N_EXPERTS - 1).astype(jnp.int32)
        n_used = (pad_end[-1:] // tb).astype(jnp.int32)
        blk_in_region = jnp.arange(n_blocks, dtype=jnp.int32)[:, None] - ((pad_end - region) // tb)[None, :]
        used = jnp.sum(run, axis=0)[None, :] - blk_in_region * tb
        used = jnp.sum(jnp.where(block_e[:, None] == jnp.arange(N_EXPERTS), used, 0), axis=1)
        half_full = (used <= tb // 2).astype(jnp.int32)

        xs = _dispatch(pieces, pad_end, lslot, c, n_slots)
        ys = _experts(block_e, n_used, half_full, xs, w1[i], w2[i],
                      b1[i][:, None, 0::2], b1[i][:, None, 1::2], b2[i][:, None, :], perm)
        h = _combine(pieces, lslot.T, gate.T, h1, p[i].reshape(t, -1), ys, g_ple[i][None],
                     w_ple_gate[i].astype(BF16), w_ple_proj[i].astype(BF16), g_final[None],
                     final=(i == depth - 1))
    return h.reshape(b, s, d)
```

```python
import functools
import math

import jax
import jax.numpy as jnp
from jax import lax
from jax.experimental import pallas as pl
from jax.experimental.pallas import tpu as pltpu

F32 = jnp.float32
BF16 = jnp.bfloat16

EPS = 1e-6
CHUNK = 64
SB_HEADS = 8
SB_HEAD_DIM = 64
SB_WIDTH = SB_HEADS * SB_HEAD_DIM
SGU_GROUPS = 8
SGU_WIDTH = 512
SGU_GROUP_DIM = SGU_WIDTH // SGU_GROUPS
SGU_BLOCK = 128
N_EXPERTS = 32
TOP_K = 4
SWIGLU_ALPHA = 1.702
SWIGLU_LIMIT = 7.0
LOG2E = 1.4426950408889634
UNDERFLOW_LOG2 = -160.0

LANES = 128
MXU_DIM = 256
VMEM_LIMIT_BYTES = 56 * 1024 * 1024

TM_PROJ = 512
TQ_ATTN = 512
ATTN_TILES_PER_STEP = 2
KC_ATTN = MXU_DIM
TB_EXPERT = 512
ROW_ALIGN = 8
PIECE_ROWS = (32, ROW_ALIGN)
SLOT_CHUNK = 256
LOCAL_SLOTS = -(-(TM_PROJ * TOP_K + N_EXPERTS * (ROW_ALIGN - 1)) // SLOT_CHUNK) * SLOT_CHUNK
MAX_PIECES = max(LOCAL_SLOTS // PIECE_ROWS[0],
                 N_EXPERTS * (PIECE_ROWS[0] // PIECE_ROWS[1] - 1))


def _cparams(sem):
    return pltpu.CompilerParams(dimension_semantics=sem,
                                vmem_limit_bytes=VMEM_LIMIT_BYTES)


def _rms(x, g):
    ms = jnp.mean(x * x, axis=-1, keepdims=True)
    return x * lax.rsqrt(ms + EPS) * g


def _sigmoid(x):
    return 0.5 * jnp.tanh(0.5 * x) + 0.5


def _inproj_kernel(x_ref, g_ref, w_ref, sw_ref, sbias_ref, sn_ref,
                   qkv_ref, ob_ref, sga_ref, sgb_ref):
    tm = x_ref.shape[0]
    a = _rms(x_ref[...], g_ref[...]).astype(BF16)

    def proj(c0, width):
        return jnp.dot(a, w_ref[:, c0:c0 + width], preferred_element_type=F32)

    qkv_ref[:, 0:SB_WIDTH] = (proj(0, SB_WIDTH) * (LOG2E / math.sqrt(SB_HEAD_DIM))).astype(BF16)
    qkv_ref[:, SB_WIDTH:2 * SB_WIDTH] = proj(SB_WIDTH, SB_WIDTH).astype(BF16)
    qkv_ref[:, 2 * SB_WIDTH:3 * SB_WIDTH] = proj(2 * SB_WIDTH, SB_WIDTH).astype(BF16)

    c_u = 3 * SB_WIDTH
    gu = jax.nn.gelu(proj(c_u, SGU_WIDTH))
    gv = jax.nn.gelu(proj(c_u + SGU_WIDTH, SGU_WIDTH))
    mu = jnp.mean(gv, axis=-1, keepdims=True)
    d = gv - mu
    var = jnp.mean(d * d, axis=-1, keepdims=True)
    vn = (d * lax.rsqrt(var + EPS) * sn_ref[...]).astype(BF16)

    rr = lax.broadcasted_iota(jnp.int32, (SGU_BLOCK, SGU_BLOCK), 0) // CHUNK
    cc = lax.broadcasted_iota(jnp.int32, (SGU_BLOCK, SGU_BLOCK), 1) // CHUNK
    cmask = cc <= rr
    lane = lax.broadcasted_iota(jnp.int32, (SGU_BLOCK, LANES), 1)
    lo_mask = lane < SGU_GROUP_DIM
    wcat = []
    for j in range(SGU_GROUPS // 2):
        w0 = jnp.where(cmask, sw_ref[2 * j], 0.0).astype(BF16)
        w1 = jnp.where(cmask, sw_ref[2 * j + 1], 0.0).astype(BF16)
        wcat.append(jnp.concatenate([w0, w1], axis=1))

    zero = jnp.zeros((), BF16)
    for blk in range(tm // SGU_BLOCK):
        r0 = blk * SGU_BLOCK
        for j in range(SGU_GROUPS // 2):
            vp = vn[r0:r0 + SGU_BLOCK, j * LANES:(j + 1) * LANES]
            rhs = jnp.concatenate([jnp.where(lo_mask, vp, zero),
                                   jnp.where(lo_mask, zero, vp)], axis=0)
            sv = jnp.dot(wcat[j], rhs, preferred_element_type=F32)
            sv = sv + sbias_ref[:, j * LANES:(j + 1) * LANES]
            ob_ref[r0:r0 + SGU_BLOCK, j * LANES:(j + 1) * LANES] = (
                gu[r0:r0 + SGU_BLOCK, j * LANES:(j + 1) * LANES] * sv).astype(BF16)

    c_g = c_u + 2 * SGU_WIDTH
    d_model = sga_ref.shape[1]
    for c0 in range(0, d_model, 512):
        sga_ref[:, c0:c0 + 512] = _sigmoid(proj(c_g + c0, 512)).astype(BF16)
        sgb_ref[:, c0:c0 + 512] = _sigmoid(proj(c_g + d_model + c0, 512)).astype(BF16)


def _inproj(x2, g_mix, w_in, sgu_w, sgu_bias, sgu_norm):
    t, d = x2.shape
    ncol = w_in.shape[1]
    tm = TM_PROJ
    const = lambda *shape: pl.BlockSpec(shape, lambda i: (0,) * len(shape))
    return pl.pallas_call(
        _inproj_kernel,
        grid=(t // tm,),
        in_specs=[
            pl.BlockSpec((tm, d), lambda i: (i, 0)),
            const(1, d),
            const(d, ncol),
            const(SGU_GROUPS, SGU_BLOCK, SGU_BLOCK),
            const(SGU_BLOCK, SGU_WIDTH),
            const(1, SGU_WIDTH),
        ],
        out_specs=[
            pl.BlockSpec((tm, 3 * SB_WIDTH), lambda i: (i, 0)),
            pl.BlockSpec((tm, SGU_WIDTH), lambda i: (i, 0)),
            pl.BlockSpec((tm, d), lambda i: (i, 0)),
            pl.BlockSpec((tm, d), lambda i: (i, 0)),
        ],
        out_shape=[
            jax.ShapeDtypeStruct((t, 3 * SB_WIDTH), BF16),
            jax.ShapeDtypeStruct((t, SGU_WIDTH), BF16),
            jax.ShapeDtypeStruct((t, d), BF16),
            jax.ShapeDtypeStruct((t, d), BF16),
        ],
        compiler_params=_cparams(("arbitrary",)),
        name="inproj_sgu",
    )(x2, g_mix, w_in, sgu_w, sgu_bias, sgu_norm)


def _attn_kernel(q_ref, k_ref, v_ref, ntri_ref, o_ref, carry_sc, acc_sc):
    tq = TQ_ATTN
    kc = KC_ATTN
    head0 = lax.broadcasted_iota(jnp.int32, (kc, LANES), 1) < SB_HEAD_DIM
    zero = jnp.zeros((), BF16)
    tri2 = (lax.broadcasted_iota(jnp.int32, (2 * kc, kc), 1)
            < lax.bitwise_and(lax.broadcasted_iota(jnp.int32, (2 * kc, kc), 0), kc - 1))

    def tile(sub, _):
        qi = pl.program_id(2) * (q_ref.shape[1] // tq) + sub
        r0 = pl.multiple_of(sub * tq, tq)
        q = q_ref[0, pl.ds(r0, tq), :]
        qs = [jnp.concatenate([jnp.where(head0, q[rb * kc:(rb + 1) * kc], zero),
                               jnp.where(head0, zero, q[rb * kc:(rb + 1) * kc])], axis=0)
              for rb in range(tq // kc)]
        carry_sc[...] = jnp.zeros_like(carry_sc)
        acc_sc[...] = jnp.zeros_like(acc_sc)

        def scores(j, rb, causal, keep):
            k0 = pl.multiple_of(j * kc, kc)
            z = lax.dot_general(qs[rb], k_ref[0, pl.ds(k0, kc), :], (((1,), (1,)), ((), ())),
                                preferred_element_type=F32)
            sp = jnp.maximum(z, 0.0) + jnp.log2(1.0 + jnp.exp2(-jnp.abs(z)))
            if causal:
                sp = jnp.where(tri2, sp, 0.0)
            if keep is not None:
                sp = jnp.where(keep, sp, 0.0)
            cum = jnp.dot(sp.astype(BF16), ntri_ref[...], preferred_element_type=F32)
            return k0, rb, causal, keep, z, cum

        def weigh(part):
            k0, rb, causal, keep, z, cum = part
            carry = carry_sc[rb]
            w = jnp.exp2(z + cum + jnp.concatenate([carry] * (kc // LANES), axis=1))
            carry_sc[rb] = carry + jnp.broadcast_to(cum[:, 0:1], carry.shape)
            if causal:
                w = jnp.where(tri2, w, 0.0)
            if keep is not None:
                w = jnp.where(keep, w, 0.0)
            acc_sc[rb] += jnp.dot(w.astype(BF16), v_ref[0, pl.ds(k0, kc), :],
                                  preferred_element_type=F32)

        ndiag = tq // kc
        npast = qi * ndiag

        def past_part(rb, p):
            return scores(jnp.maximum(npast - 1 - p, 0), rb, False, p < npast)

        head = []
        for c in reversed(range(ndiag)):
            head.append(scores(npast + c, c, True, None))
            head += [scores(npast + c, rb, False, None) for rb in range(c + 1, ndiag)]
        for rb in range(ndiag - 1):
            head += [past_part(rb, p) for p in range(ndiag - 1 - rb)]
        for part in head:
            weigh(part)

        def live(t):
            return jnp.logical_and(t < npast, jnp.max(carry_sc[...]) > UNDERFLOW_LOG2)

        def past(t):
            for part in [past_part(rb, ndiag - 1 - rb + t) for rb in range(ndiag)]:
                weigh(part)
            return t + 1

        lax.while_loop(live, past, 0)
        for rb in range(ndiag):
            acc = acc_sc[rb]
            o_ref[0, pl.ds(r0 + rb * kc, kc), :] = jnp.where(head0, acc[:kc],
                                                             acc[kc:]).astype(o_ref.dtype)
        return 0

    lax.fori_loop(0, q_ref.shape[1] // tq, tile, 0)


def _attention(qkv3, ntri):
    b, s, _ = qkv3.shape
    tq = TQ_ATTN * ATTN_TILES_PER_STEP
    npair = SB_WIDTH // LANES
    return pl.pallas_call(
        _attn_kernel,
        grid=(b, npair, s // tq),
        in_specs=[
            pl.BlockSpec((1, tq, LANES), lambda bi, p, i: (bi, i, p)),
            pl.BlockSpec((1, s, LANES), lambda bi, p, i: (bi, 0, npair + p)),
            pl.BlockSpec((1, s, LANES), lambda bi, p, i: (bi, 0, 2 * npair + p)),
            pl.BlockSpec((KC_ATTN, KC_ATTN), lambda bi, p, i: (0, 0)),
        ],
        out_specs=pl.BlockSpec((1, tq, LANES), lambda bi, p, i: (bi, i, p)),
        out_shape=jax.ShapeDtypeStruct((b, s, SB_WIDTH), BF16),
        scratch_shapes=[pltpu.VMEM((TQ_ATTN // KC_ATTN, 2 * KC_ATTN, LANES), F32),
                        pltpu.VMEM((TQ_ATTN // KC_ATTN, 2 * KC_ATTN, LANES), F32)],
        compiler_params=_cparams(("arbitrary", "arbitrary", "arbitrary")),
        name="stickbreak_attn",
    )(qkv3, qkv3, qkv3, ntri)


def _outproj_kernel(oa_ref, ob_ref, sga_ref, sgb_ref, x_ref, woa_ref, wob_ref, wo_ref,
                    g_ref, wr_ref, br_ref, upper_ref,
                    h_ref, c_ref, lslot_ref, gate_ref, cnt_ref):
    tm = x_ref.shape[0]
    ma = jnp.dot(oa_ref[...], woa_ref[...], preferred_element_type=F32)
    mb = jnp.dot(ob_ref[...], wob_ref[...], preferred_element_type=F32)
    merged = sga_ref[...].astype(F32) * ma + sgb_ref[...].astype(F32) * mb
    h = x_ref[...] + jnp.dot(merged.astype(BF16), wo_ref[...], preferred_element_type=F32)
    h_ref[...] = h
    c = _rms(h, g_ref[...])
    c_hi = c.astype(BF16)
    c_ref[...] = c_hi

    c_lo = (c - c_hi.astype(F32)).astype(BF16)
    wr = wr_ref[...]
    wr_hi = wr.astype(BF16)
    wr_lo = (wr - wr_hi.astype(F32)).astype(BF16)
    nt = (((1,), (1,)), ((), ()))
    logits = (lax.dot_general(wr_hi, c_hi, nt, preferred_element_type=F32)
              + lax.dot_general(wr_hi, c_lo, nt, preferred_element_type=F32)
              + lax.dot_general(wr_lo, c_hi, nt, preferred_element_type=F32)) + br_ref[...]
    eid = lax.broadcasted_iota(jnp.int32, (N_EXPERTS, tm), 0).astype(F32)
    work = logits
    vals, sels = [], []
    for _ in range(TOP_K):
        m = jnp.max(work, axis=0, keepdims=True)
        ik = jnp.min(jnp.where(work == m, eid, float(N_EXPERTS)), axis=0, keepdims=True)
        sel = eid == ik
        vals.append(m)
        sels.append(sel)
        work = jnp.where(sel, -jnp.inf, work)
    es = [jnp.exp(v - vals[0]) for v in vals]
    inv = 1.0 / (es[0] + es[1] + es[2] + es[3])
    onehot = jnp.zeros((N_EXPERTS, tm), F32)
    for sel in sels:
        onehot = onehot + jnp.where(sel, 1.0, 0.0)

    prefix = jnp.dot(onehot.astype(BF16), upper_ref[...], preferred_element_type=F32)
    n = jnp.sum(onehot, axis=1, keepdims=True)
    n_al = jnp.ceil(n * (1.0 / ROW_ALIGN)) * ROW_ALIGN
    er = lax.broadcasted_iota(jnp.int32, (N_EXPERTS, N_EXPERTS), 0)
    ec = lax.broadcasted_iota(jnp.int32, (N_EXPERTS, N_EXPERTS), 1)
    run_off = jnp.dot(jnp.where(ec < er, 1.0, 0.0), jnp.broadcast_to(n_al, (N_EXPERTS, LANES)),
                      precision=lax.Precision.HIGHEST, preferred_element_type=F32)
    slot = prefix + run_off[:, 0:1]
    for k in range(TOP_K):
        gate_ref[k:k + 1, :] = es[k] * inv
        lslot_ref[k:k + 1, :] = jnp.sum(jnp.where(sels[k], slot, 0.0), axis=0,
                                        keepdims=True).astype(jnp.int32)
    cnt_ref[0] = jnp.broadcast_to(n, (N_EXPERTS, LANES))


def _outproj(o_a, o_b, sga, sgb, x2, w_out_a, w_out_b, w_out, g_moe, w_router_t, b_router, upper):
    t, d = x2.shape
    tm = TM_PROJ
    const = lambda *shape: pl.BlockSpec(shape, lambda i: (0,) * len(shape))
    row = lambda w: pl.BlockSpec((tm, w), lambda i: (i, 0))
    colk = pl.BlockSpec((TOP_K, tm), lambda i: (0, i))
    return pl.pallas_call(
        _outproj_kernel,
        grid=(t // tm,),
        in_specs=[row(SB_WIDTH), row(SGU_WIDTH), row(d), row(d), row(d),
                  const(SB_WIDTH, d), const(SGU_WIDTH, d), const(d, d),
                  const(1, d), const(N_EXPERTS, d), const(N_EXPERTS, 1), const(tm, tm)],
        out_specs=[row(d), row(d), colk, colk,
                   pl.BlockSpec((1, N_EXPERTS, LANES), lambda i: (i, 0, 0))],
        out_shape=[
            jax.ShapeDtypeStruct((t, d), F32),
            jax.ShapeDtypeStruct((t, d), BF16),
            jax.ShapeDtypeStruct((TOP_K, t), jnp.int32),
            jax.ShapeDtypeStruct((TOP_K, t), F32),
            jax.ShapeDtypeStruct((t // tm, N_EXPERTS, LANES), F32),
        ],
        compiler_params=_cparams(("arbitrary",)),
        name="outproj_router",
    )(o_a, o_b, sga, sgb, x2, w_out_a, w_out_b, w_out, g_moe, w_router_t, b_router, upper)


def _piece_loops(tables, win, fn):
    n_ref, local_ref, global_ref = tables
    for cls, rows in enumerate(PIECE_ROWS):
        head = win * len(PIECE_ROWS) + cls
        count = n_ref[head]

        def piece(i, off=head * MAX_PIECES, rows=rows):
            fn(pl.multiple_of(local_ref[off + i], ROW_ALIGN),
               pl.multiple_of(global_ref[off + i], ROW_ALIGN), rows)

        def pair(i, _, piece=piece):
            piece(2 * i)
            piece(2 * i + 1)
            return 0

        lax.fori_loop(0, lax.shift_right_logical(count, 1), pair, 0)

        @pl.when(lax.bitwise_and(count, 1) == 1)
        def _(piece=piece, count=count):
            piece(count - 1)


def _piece_tables(run, loff, base):
    big = PIECE_ROWS[0]
    nbig = run // big
    nsmall = (run - nbig * big) // ROW_ALIGN
    i = jnp.arange(MAX_PIECES, dtype=jnp.int32)
    experts = jnp.arange(N_EXPERTS, dtype=jnp.int32)

    def flat(cnt, rows, skip):
        cum = jnp.cumsum(cnt, axis=1)
        owner = jnp.minimum(jnp.sum(i[None, :, None] >= cum[:, None, :], axis=-1), N_EXPERTS - 1)
        onehot = owner[:, :, None] == experts
        pick = lambda tab: jnp.sum(jnp.where(onehot, tab[:, None, :], 0), axis=-1)
        j = i[None, :] - pick(cum - cnt)
        return cum[:, -1], pick(loff + skip) + j * rows, pick(base + skip) + j * rows

    nb, lb, gb = flat(nbig, big, 0)
    ns, lsm, gsm = flat(nsmall, ROW_ALIGN, nbig * big)
    pack = lambda a, b: jnp.stack([a, b], axis=1).reshape(-1).astype(jnp.int32)
    return pack(nb, ns), pack(lb, lsm), pack(gb, gsm)


def _dispatch_kernel(pn_ref, pl_ref, pg_ref, pad_end_ref, lslot_ref, c_ref, xs_ref,
                     loc_sc, zero_sc, sem, zsem, tsem):
    w = pl.program_id(0)
    tm = c_ref.shape[0]
    tb = zero_sc.shape[0]
    nloc = loc_sc.shape[1]

    def unused_blocks(fn):
        def body(b, _):
            fn(pltpu.make_async_copy(
                zero_sc, xs_ref.at[pl.ds(pl.multiple_of(b * tb, tb), tb), :], tsem))
            return 0
        lax.fori_loop(pad_end_ref[N_EXPERTS - 1] // tb, xs_ref.shape[0] // tb, body, 0)

    @pl.when(w == 0)
    def _():
        zero_sc[...] = jnp.zeros_like(zero_sc)

        def pad_copy(e):
            return pltpu.make_async_copy(
                zero_sc, xs_ref.at[pl.ds(pl.multiple_of(pad_end_ref[e] - tb, tb), tb), :], zsem)

        def nonempty(e):
            prev = jnp.where(e > 0, pad_end_ref[jnp.maximum(e - 1, 0)], 0)
            return pad_end_ref[e] > prev

        def zstart(e, _):
            @pl.when(nonempty(e))
            def _():
                pad_copy(e).start()
            return 0

        def zwait(e, _):
            @pl.when(nonempty(e))
            def _():
                pad_copy(e).wait()
            return 0

        lax.fori_loop(0, N_EXPERTS, zstart, 0)
        unused_blocks(lambda copy: copy.start())
        lax.fori_loop(0, N_EXPERTS, zwait, 0)

    ls = [lslot_ref[k:k + 1, :] for k in range(TOP_K)]
    buf = lax.bitwise_and(w, 1)
    for r0 in range(0, nloc, SLOT_CHUNK):
        srow = lax.broadcasted_iota(jnp.int32, (SLOT_CHUNK, tm), 0) + r0
        sel = jnp.zeros((SLOT_CHUNK, tm), F32)
        for k in range(TOP_K):
            sel = jnp.where(srow == ls[k], 1.0, sel)
        loc_sc[buf, r0:r0 + SLOT_CHUNK, :] = jnp.dot(sel.astype(BF16), c_ref[...],
                                                     preferred_element_type=F32)

    tables = (pn_ref, pl_ref, pg_ref)

    def piece_copy(slot):
        def build(local_row, global_row, rows):
            return pltpu.make_async_copy(loc_sc.at[slot, pl.ds(local_row, rows), :],
                                         xs_ref.at[pl.ds(global_row, rows), :], sem.at[slot])
        return build

    _piece_loops(tables, w, lambda *a: piece_copy(buf)(*a).start())

    @pl.when(w > 0)
    def _():
        _piece_loops(tables, w - 1, lambda *a: piece_copy(1 - buf)(*a).wait())

    @pl.when(w == pl.num_programs(0) - 1)
    def _():
        _piece_loops(tables, w, lambda *a: piece_copy(buf)(*a).wait())
        unused_blocks(lambda copy: copy.wait())


def _dispatch(pieces, pad_end, lslot, c, n_slots):
    t, d = c.shape
    tm = TM_PROJ
    grid_spec = pltpu.PrefetchScalarGridSpec(
        num_scalar_prefetch=4,
        grid=(t // tm,),
        in_specs=[
            pl.BlockSpec((TOP_K, tm), lambda i, *_: (0, i)),
            pl.BlockSpec((tm, d), lambda i, *_: (i, 0)),
        ],
        out_specs=pl.BlockSpec(memory_space=pl.ANY),
        scratch_shapes=[pltpu.VMEM((2, LOCAL_SLOTS, d), F32),
                        pltpu.VMEM((TB_EXPERT, d), F32),
                        pltpu.SemaphoreType.DMA((2,)), pltpu.SemaphoreType.DMA(()),
                        pltpu.SemaphoreType.DMA(())],
    )
    return pl.pallas_call(
        _dispatch_kernel,
        grid_spec=grid_spec,
        out_shape=jax.ShapeDtypeStruct((n_slots, d), F32),
        compiler_params=_cparams(("arbitrary",)),
        name="moe_dispatch",
    )(*pieces, pad_end, lslot, c)


def _expert_kernel(be_ref, nb_ref, xs_ref, w1_ref, w2_ref, b1g_ref, b1l_ref, b2_ref, perm_ref,
                   ys_ref, w1g_sc, w1l_sc, w2_sc):
    i = pl.program_id(0)
    active = i < nb_ref[0]
    fresh = jnp.logical_or(i == 0, be_ref[i] != be_ref[jnp.maximum(i - 1, 0)])

    @pl.when(jnp.logical_and(active, fresh))
    def _():
        half = MXU_DIM // 2
        for cb in range(w1_ref.shape[2] // MXU_DIM):
            blk = w1_ref[0, :, cb * MXU_DIM:(cb + 1) * MXU_DIM].astype(BF16)
            de = jnp.dot(blk, perm_ref[...], preferred_element_type=F32)
            w1g_sc[:, cb * half:(cb + 1) * half] = de[:, :half].astype(BF16)
            w1l_sc[:, cb * half:(cb + 1) * half] = de[:, half:].astype(BF16)
        w2_sc[...] = w2_ref[0].astype(BF16)

    @pl.when(active)
    def _():
        xb = xs_ref[...].astype(BF16)
        hg = jnp.dot(xb, w1g_sc[...], preferred_element_type=F32) + b1g_ref[0]
        hl = jnp.dot(xb, w1l_sc[...], preferred_element_type=F32) + b1l_ref[0]
        xg = jnp.minimum(hg, SWIGLU_LIMIT)
        xl = jnp.clip(hl, -SWIGLU_LIMIT, SWIGLU_LIMIT)
        act = xg * _sigmoid(SWIGLU_ALPHA * xg) * (xl + 1.0)
        ys_ref[...] = jnp.dot(act.astype(BF16), w2_sc[...], preferred_element_type=F32) + b2_ref[0]

    @pl.when(jnp.logical_not(active))
    def _():
        ys_ref[...] = jnp.zeros_like(ys_ref)


def _experts(block_e, n_used, xs, w1, w2, b1g, b1l, b2, perm):
    n_slots, d = xs.shape
    f2 = w1.shape[2]
    f = f2 // 2
    tb = TB_EXPERT
    blk = lambda i, be, nb: (jnp.minimum(i, nb[0] - 1), 0)
    wsel = lambda i, be, nb: (be[i], 0, 0)
    grid_spec = pltpu.PrefetchScalarGridSpec(
        num_scalar_prefetch=2,
        grid=(n_slots // tb,),
        in_specs=[
            pl.BlockSpec((tb, d), blk),
            pl.BlockSpec((1, d, f2), wsel),
            pl.BlockSpec((1, f, d), wsel),
            pl.BlockSpec((1, 1, f), wsel),
            pl.BlockSpec((1, 1, f), wsel),
            pl.BlockSpec((1, 1, d), wsel),
            pl.BlockSpec((MXU_DIM, MXU_DIM), lambda i, be, nb: (0, 0)),
        ],
        out_specs=pl.BlockSpec((tb, d), lambda i, be, nb: (i, 0)),
        scratch_shapes=[pltpu.VMEM((d, f), BF16), pltpu.VMEM((d, f), BF16),
                        pltpu.VMEM((f, d), BF16)],
    )
    return pl.pallas_call(
        _expert_kernel,
        grid_spec=grid_spec,
        out_shape=jax.ShapeDtypeStruct((n_slots, d), F32),
        compiler_params=_cparams(("arbitrary",)),
        name="moe_experts",
    )(block_e, n_used, xs, w1, w2, b1g, b1l, b2, perm)


def _combine_kernel(pn_ref, pl_ref, pg_ref, lslot_ref, gate_ref, h_ref, p_ref, ys_ref,
                    gp_ref, wg_ref, wp_ref, gf_ref, out_ref, loc_sc, locb_sc, sem,
                    *, final):
    w = pl.program_id(0)
    tm = h_ref.shape[0]
    nloc = loc_sc.shape[0]

    tables = (pn_ref, pl_ref, pg_ref)

    def piece_copy(local_row, global_row, rows):
        return pltpu.make_async_copy(ys_ref.at[pl.ds(global_row, rows), :],
                                     loc_sc.at[pl.ds(local_row, rows), :], sem)

    @pl.when(w == 0)
    def _():
        loc_sc[...] = jnp.zeros_like(loc_sc)
        _piece_loops(tables, w, lambda *a: piece_copy(*a).start())

    _piece_loops(tables, w, lambda *a: piece_copy(*a).wait())
    for c0 in range(0, nloc, SLOT_CHUNK):
        locb_sc[c0:c0 + SLOT_CHUNK, :] = loc_sc[c0:c0 + SLOT_CHUNK, :].astype(BF16)

    @pl.when(w + 1 < pl.num_programs(0))
    def _():
        _piece_loops(tables, w + 1, lambda *a: piece_copy(*a).start())

    lcol = [lslot_ref[:, k:k + 1] for k in range(TOP_K)]
    gcol = [gate_ref[:, k:k + 1] for k in range(TOP_K)]
    h = h_ref[...]
    for c0 in range(0, nloc, SLOT_CHUNK):
        scol = lax.broadcasted_iota(jnp.int32, (tm, SLOT_CHUNK), 1) + c0
        g = jnp.zeros((tm, SLOT_CHUNK), F32)
        for k in range(TOP_K):
            g = jnp.where(scol == lcol[k], gcol[k], g)
        h = h + jnp.dot(g.astype(BF16), locb_sc[c0:c0 + SLOT_CHUNK, :],
                        preferred_element_type=F32)

    r = _rms(h, gp_ref[...]).astype(BF16)
    pg = _sigmoid(jnp.dot(r, wg_ref[...], preferred_element_type=F32))
    pp = jnp.dot(p_ref[...].astype(BF16), wp_ref[...], preferred_element_type=F32)
    h = h + pg * pp
    out_ref[...] = _rms(h, gf_ref[...]) if final else h


def _combine(pieces, lslot_tk, gate_tk, h1, p2, ys, g_ple, w_ple_gate, w_ple_proj, g_final,
             final):
    t, d = h1.shape
    pd = p2.shape[1]
    tm = TM_PROJ
    const = lambda *shape: pl.BlockSpec(shape, lambda i, *_: (0,) * len(shape))
    row = lambda width: pl.BlockSpec((tm, width), lambda i, *_: (i, 0))
    grid_spec = pltpu.PrefetchScalarGridSpec(
        num_scalar_prefetch=3,
        grid=(t // tm,),
        in_specs=[row(TOP_K), row(TOP_K), row(d), row(pd),
                  pl.BlockSpec(memory_space=pl.ANY),
                  const(1, d), const(d, d), const(pd, d), const(1, d)],
        out_specs=row(d),
        scratch_shapes=[pltpu.VMEM((LOCAL_SLOTS, d), F32), pltpu.VMEM((LOCAL_SLOTS, d), BF16),
                        pltpu.SemaphoreType.DMA(())],
    )
    return pl.pallas_call(
        functools.partial(_combine_kernel, final=final),
        grid_spec=grid_spec,
        out_shape=jax.ShapeDtypeStruct((t, d), F32),
        compiler_params=_cparams(("arbitrary",)),
        name="combine_ple",
    )(*pieces, lslot_tk, gate_tk, h1, p2, ys, g_ple, w_ple_gate, w_ple_proj, g_final)


def kernel(x, p, g_mix, w_in, w_out_a, w_out_b, w_out, sgu_norm, sgu_w, sgu_b, g_moe,
           w_router, b_router, w1, b1, w2, b2, g_ple, w_ple_gate, w_ple_proj, g_final):
    b, s, d = x.shape
    depth = w_in.shape[0]
    t = b * s
    assert s % (TQ_ATTN * ATTN_TILES_PER_STEP) == 0 and t % TM_PROJ == 0

    kk = jnp.arange(KC_ATTN)
    ntri = -(kk[:, None] >= kk[None, :]).astype(BF16)
    tt = jnp.arange(TM_PROJ)
    upper = (tt[:, None] < tt[None, :]).astype(BF16)
    rr = jnp.arange(MXU_DIM)[:, None]
    cc = jnp.arange(MXU_DIM)[None, :]
    half = MXU_DIM // 2
    perm = (rr == jnp.where(cc < half, 2 * cc, 2 * (cc - half) + 1)).astype(BF16)

    tb = TB_EXPERT
    n_win = t // TM_PROJ
    n_blocks = -(-(t * TOP_K + n_win * N_EXPERTS * (ROW_ALIGN - 1) + N_EXPERTS * (tb - 1)) // tb)
    n_slots = n_blocks * tb

    h = x.reshape(t, d)
    for i in range(depth):
        sgu_bias = jnp.repeat(sgu_b[i].T, SGU_GROUP_DIM, axis=1)
        qkv, o_b, sga, sgb = _inproj(h, g_mix[i][None], w_in[i].astype(BF16), sgu_w[i],
                                     sgu_bias, sgu_norm[i][None])
        o_a = _attention(qkv.reshape(b, s, 3 * SB_WIDTH), ntri).reshape(t, SB_WIDTH)
        h1, c, lslot, gate, cnt = _outproj(
            o_a, o_b, sga, sgb, h, w_out_a[i].astype(BF16), w_out_b[i].astype(BF16),
            w_out[i].astype(BF16), g_moe[i][None], w_router[i].T, b_router[i][:, None], upper)

        counts = cnt[:, :, 0].astype(jnp.int32)
        run = ((counts + ROW_ALIGN - 1) // ROW_ALIGN) * ROW_ALIGN
        loff = jnp.cumsum(run, axis=1) - run
        region = ((jnp.sum(run, axis=0) + tb - 1) // tb) * tb
        pad_end = jnp.cumsum(region).astype(jnp.int32)
        base = (pad_end - region)[None, :] + jnp.cumsum(run, axis=0) - run
        pieces = _piece_tables(run, loff, base)
        starts = jnp.arange(n_blocks, dtype=jnp.int32) * tb
        block_e = jnp.minimum(jnp.sum(starts[:, None] >= pad_end[None, :], axis=1),
                              N_EXPERTS - 1).astype(jnp.int32)
        n_used = (pad_end[-1:] // tb).astype(jnp.int32)

        xs = _dispatch(pieces, pad_end, lslot, c, n_slots)
        ys = _experts(block_e, n_used, xs, w1[i], w2[i],
                      b1[i][:, None, 0::2], b1[i][:, None, 1::2], b2[i][:, None, :], perm)
        h = _combine(pieces, lslot.T, gate.T, h1, p[i].reshape(t, -1), ys, g_ple[i][None],
                     w_ple_gate[i].astype(BF16), w_ple_proj[i].astype(BF16), g_final[None],
                     final=(i == depth - 1))
    return h.reshape(b, s, d)
```

```python
import functools
import math

import jax
import jax.numpy as jnp
from jax import lax
from jax.experimental import pallas as pl
from jax.experimental.pallas import tpu as pltpu

F32 = jnp.float32
BF16 = jnp.bfloat16

EPS = 1e-6
CHUNK = 64
SB_HEADS = 8
SB_HEAD_DIM = 64
SB_WIDTH = SB_HEADS * SB_HEAD_DIM
SGU_GROUPS = 8
SGU_WIDTH = 512
SGU_GROUP_DIM = SGU_WIDTH // SGU_GROUPS
SGU_BLOCK = 128
N_EXPERTS = 32
TOP_K = 4
SWIGLU_ALPHA = 1.702
SWIGLU_LIMIT = 7.0
LOG2E = 1.4426950408889634
UNDERFLOW_LOG2 = -160.0

LANES = 128
MXU_DIM = 256
VMEM_LIMIT_BYTES = 56 * 1024 * 1024

TM_PROJ = 512
TQ_ATTN = 512
ATTN_TILES_PER_STEP = 2
KC_ATTN = MXU_DIM
TB_EXPERT = 512
ROW_ALIGN = 8
PIECE_ROWS = (32, ROW_ALIGN)
SLOT_CHUNK = 256
LOCAL_SLOTS = -(-(TM_PROJ * TOP_K + N_EXPERTS * (ROW_ALIGN - 1)) // SLOT_CHUNK) * SLOT_CHUNK
MAX_PIECES = max(LOCAL_SLOTS // PIECE_ROWS[0],
                 N_EXPERTS * (PIECE_ROWS[0] // PIECE_ROWS[1] - 1))


def _cparams(sem):
    return pltpu.CompilerParams(dimension_semantics=sem,
                                vmem_limit_bytes=VMEM_LIMIT_BYTES)


def _rms(x, g):
    ms = jnp.mean(x * x, axis=-1, keepdims=True)
    return x * lax.rsqrt(ms + EPS) * g


def _sigmoid(x):
    return 0.5 * jnp.tanh(0.5 * x) + 0.5


def _inproj_kernel(x_ref, g_ref, w_ref, sw_ref, sbias_ref, sn_ref,
                   qkv_ref, ob_ref, sga_ref, sgb_ref):
    tm = x_ref.shape[0]
    a = _rms(x_ref[...], g_ref[...]).astype(BF16)

    def proj(c0, width):
        return jnp.dot(a, w_ref[:, c0:c0 + width], preferred_element_type=F32)

    qkv_ref[:, 0:SB_WIDTH] = (proj(0, SB_WIDTH) * (LOG2E / math.sqrt(SB_HEAD_DIM))).astype(BF16)
    qkv_ref[:, SB_WIDTH:2 * SB_WIDTH] = proj(SB_WIDTH, SB_WIDTH).astype(BF16)
    qkv_ref[:, 2 * SB_WIDTH:3 * SB_WIDTH] = proj(2 * SB_WIDTH, SB_WIDTH).astype(BF16)

    c_u = 3 * SB_WIDTH
    gu = jax.nn.gelu(proj(c_u, SGU_WIDTH))
    gv = jax.nn.gelu(proj(c_u + SGU_WIDTH, SGU_WIDTH))
    mu = jnp.mean(gv, axis=-1, keepdims=True)
    d = gv - mu
    var = jnp.mean(d * d, axis=-1, keepdims=True)
    vn = (d * lax.rsqrt(var + EPS) * sn_ref[...]).astype(BF16)

    rr = lax.broadcasted_iota(jnp.int32, (SGU_BLOCK, SGU_BLOCK), 0) // CHUNK
    cc = lax.broadcasted_iota(jnp.int32, (SGU_BLOCK, SGU_BLOCK), 1) // CHUNK
    cmask = cc <= rr
    lane = lax.broadcasted_iota(jnp.int32, (SGU_BLOCK, LANES), 1)
    lo_mask = lane < SGU_GROUP_DIM
    wcat = []
    for j in range(SGU_GROUPS // 2):
        w0 = jnp.where(cmask, sw_ref[2 * j], 0.0).astype(BF16)
        w1 = jnp.where(cmask, sw_ref[2 * j + 1], 0.0).astype(BF16)
        wcat.append(jnp.concatenate([w0, w1], axis=1))

    zero = jnp.zeros((), BF16)
    for blk in range(tm // SGU_BLOCK):
        r0 = blk * SGU_BLOCK
        for j in range(SGU_GROUPS // 2):
            vp = vn[r0:r0 + SGU_BLOCK, j * LANES:(j + 1) * LANES]
            rhs = jnp.concatenate([jnp.where(lo_mask, vp, zero),
                                   jnp.where(lo_mask, zero, vp)], axis=0)
            sv = jnp.dot(wcat[j], rhs, preferred_element_type=F32)
            sv = sv + sbias_ref[:, j * LANES:(j + 1) * LANES]
            ob_ref[r0:r0 + SGU_BLOCK, j * LANES:(j + 1) * LANES] = (
                gu[r0:r0 + SGU_BLOCK, j * LANES:(j + 1) * LANES] * sv).astype(BF16)

    c_g = c_u + 2 * SGU_WIDTH
    d_model = sga_ref.shape[1]
    for c0 in range(0, d_model, 512):
        sga_ref[:, c0:c0 + 512] = _sigmoid(proj(c_g + c0, 512)).astype(BF16)
        sgb_ref[:, c0:c0 + 512] = _sigmoid(proj(c_g + d_model + c0, 512)).astype(BF16)


def _inproj(x2, g_mix, w_in, sgu_w, sgu_bias, sgu_norm):
    t, d = x2.shape
    ncol = w_in.shape[1]
    tm = TM_PROJ
    const = lambda *shape: pl.BlockSpec(shape, lambda i: (0,) * len(shape))
    return pl.pallas_call(
        _inproj_kernel,
        grid=(t // tm,),
        in_specs=[
            pl.BlockSpec((tm, d), lambda i: (i, 0)),
            const(1, d),
            const(d, ncol),
            const(SGU_GROUPS, SGU_BLOCK, SGU_BLOCK),
            const(SGU_BLOCK, SGU_WIDTH),
            const(1, SGU_WIDTH),
        ],
        out_specs=[
            pl.BlockSpec((tm, 3 * SB_WIDTH), lambda i: (i, 0)),
            pl.BlockSpec((tm, SGU_WIDTH), lambda i: (i, 0)),
            pl.BlockSpec((tm, d), lambda i: (i, 0)),
            pl.BlockSpec((tm, d), lambda i: (i, 0)),
        ],
        out_shape=[
            jax.ShapeDtypeStruct((t, 3 * SB_WIDTH), BF16),
            jax.ShapeDtypeStruct((t, SGU_WIDTH), BF16),
            jax.ShapeDtypeStruct((t, d), BF16),
            jax.ShapeDtypeStruct((t, d), BF16),
        ],
        compiler_params=_cparams(("arbitrary",)),
        name="inproj_sgu",
    )(x2, g_mix, w_in, sgu_w, sgu_bias, sgu_norm)


def _attn_kernel(q_ref, k_ref, v_ref, ntri_ref, o_ref, carry_sc, acc_sc):
    tq = TQ_ATTN
    kc = KC_ATTN
    head0 = lax.broadcasted_iota(jnp.int32, (kc, LANES), 1) < SB_HEAD_DIM
    zero = jnp.zeros((), BF16)
    tri2 = (lax.broadcasted_iota(jnp.int32, (2 * kc, kc), 1)
            < lax.bitwise_and(lax.broadcasted_iota(jnp.int32, (2 * kc, kc), 0), kc - 1))

    def tile(sub, _):
        qi = pl.program_id(2) * (q_ref.shape[1] // tq) + sub
        r0 = pl.multiple_of(sub * tq, tq)
        q = q_ref[0, pl.ds(r0, tq), :]
        qs = [jnp.concatenate([jnp.where(head0, q[rb * kc:(rb + 1) * kc], zero),
                               jnp.where(head0, zero, q[rb * kc:(rb + 1) * kc])], axis=0)
              for rb in range(tq // kc)]
        carry_sc[...] = jnp.zeros_like(carry_sc)
        acc_sc[...] = jnp.zeros_like(acc_sc)

        def scores(j, rb, causal, keep):
            k0 = pl.multiple_of(j * kc, kc)
            z = lax.dot_general(qs[rb], k_ref[0, pl.ds(k0, kc), :], (((1,), (1,)), ((), ())),
                                preferred_element_type=F32)
            sp = jnp.maximum(z, 0.0) + jnp.log2(1.0 + jnp.exp2(-jnp.abs(z)))
            if causal:
                sp = jnp.where(tri2, sp, 0.0)
            if keep is not None:
                sp = jnp.where(keep, sp, 0.0)
            cum = jnp.dot(sp.astype(BF16), ntri_ref[...], preferred_element_type=F32)
            return k0, rb, causal, keep, z, cum

        def weigh(part):
            k0, rb, causal, keep, z, cum = part
            carry = carry_sc[rb]
            w = jnp.exp2(z + cum + jnp.concatenate([carry] * (kc // LANES), axis=1))
            carry_sc[rb] = carry + jnp.broadcast_to(cum[:, 0:1], carry.shape)
            if causal:
                w = jnp.where(tri2, w, 0.0)
            if keep is not None:
                w = jnp.where(keep, w, 0.0)
            acc_sc[rb] += jnp.dot(w.astype(BF16), v_ref[0, pl.ds(k0, kc), :],
                                  preferred_element_type=F32)

        ndiag = tq // kc
        npast = qi * ndiag

        def past_part(rb, p):
            return scores(jnp.maximum(npast - 1 - p, 0), rb, False, p < npast)

        head = []
        for c in reversed(range(ndiag)):
            head.append(scores(npast + c, c, True, None))
            head += [scores(npast + c, rb, False, None) for rb in range(c + 1, ndiag)]
        for rb in range(ndiag - 1):
            head += [past_part(rb, p) for p in range(ndiag - 1 - rb)]
        for part in head:
            weigh(part)

        def live(t):
            return jnp.logical_and(t < npast, jnp.max(carry_sc[...]) > UNDERFLOW_LOG2)

        def past(t):
            for part in [past_part(rb, ndiag - 1 - rb + t) for rb in range(ndiag)]:
                weigh(part)
            return t + 1

        lax.while_loop(live, past, 0)
        for rb in range(ndiag):
            acc = acc_sc[rb]
            o_ref[0, pl.ds(r0 + rb * kc, kc), :] = jnp.where(head0, acc[:kc],
                                                             acc[kc:]).astype(o_ref.dtype)
        return 0

    lax.fori_loop(0, q_ref.shape[1] // tq, tile, 0)


def _attention(qkv3, ntri):
    b, s, _ = qkv3.shape
    tq = TQ_ATTN * ATTN_TILES_PER_STEP
    npair = SB_WIDTH // LANES
    return pl.pallas_call(
        _attn_kernel,
        grid=(b, npair, s // tq),
        in_specs=[
            pl.BlockSpec((1, tq, LANES), lambda bi, p, i: (bi, i, p)),
            pl.BlockSpec((1, s, LANES), lambda bi, p, i: (bi, 0, npair + p)),
            pl.BlockSpec((1, s, LANES), lambda bi, p, i: (bi, 0, 2 * npair + p)),
            pl.BlockSpec((KC_ATTN, KC_ATTN), lambda bi, p, i: (0, 0)),
        ],
        out_specs=pl.BlockSpec((1, tq, LANES), lambda bi, p, i: (bi, i, p)),
        out_shape=jax.ShapeDtypeStruct((b, s, SB_WIDTH), BF16),
        scratch_shapes=[pltpu.VMEM((TQ_ATTN // KC_ATTN, 2 * KC_ATTN, LANES), F32),
                        pltpu.VMEM((TQ_ATTN // KC_ATTN, 2 * KC_ATTN, LANES), F32)],
        compiler_params=_cparams(("arbitrary", "arbitrary", "arbitrary")),
        name="stickbreak_attn",
    )(qkv3, qkv3, qkv3, ntri)


def _outproj_kernel(oa_ref, ob_ref, sga_ref, sgb_ref, x_ref, woa_ref, wob_ref, wo_ref,
                    g_ref, wr_ref, br_ref, upper_ref,
                    h_ref, c_ref, lslot_ref, gate_ref, cnt_ref):
    tm = x_ref.shape[0]
    ma = jnp.dot(oa_ref[...], woa_ref[...], preferred_element_type=F32)
    mb = jnp.dot(ob_ref[...], wob_ref[...], preferred_element_type=F32)
    merged = sga_ref[...].astype(F32) * ma + sgb_ref[...].astype(F32) * mb
    h = x_ref[...] + jnp.dot(merged.astype(BF16), wo_ref[...], preferred_element_type=F32)
    h_ref[...] = h
    c = _rms(h, g_ref[...])
    c_hi = c.astype(BF16)
    c_ref[...] = c_hi

    c_lo = (c - c_hi.astype(F32)).astype(BF16)
    wr = wr_ref[...]
    wr_hi = wr.astype(BF16)
    wr_lo = (wr - wr_hi.astype(F32)).astype(BF16)
    nt = (((1,), (1,)), ((), ()))
    logits = (lax.dot_general(wr_hi, c_hi, nt, preferred_element_type=F32)
              + lax.dot_general(wr_hi, c_lo, nt, preferred_element_type=F32)
              + lax.dot_general(wr_lo, c_hi, nt, preferred_element_type=F32)) + br_ref[...]
    eid = lax.broadcasted_iota(jnp.int32, (N_EXPERTS, tm), 0).astype(F32)
    work = logits
    vals, sels = [], []
    for _ in range(TOP_K):
        m = jnp.max(work, axis=0, keepdims=True)
        ik = jnp.min(jnp.where(work == m, eid, float(N_EXPERTS)), axis=0, keepdims=True)
        sel = eid == ik
        vals.append(m)
        sels.append(sel)
        work = jnp.where(sel, -jnp.inf, work)
    es = [jnp.exp(v - vals[0]) for v in vals]
    inv = 1.0 / (es[0] + es[1] + es[2] + es[3])
    onehot = jnp.zeros((N_EXPERTS, tm), F32)
    for sel in sels:
        onehot = onehot + jnp.where(sel, 1.0, 0.0)

    prefix = jnp.dot(onehot.astype(BF16), upper_ref[...], preferred_element_type=F32)
    n = jnp.sum(onehot, axis=1, keepdims=True)
    n_al = jnp.ceil(n * (1.0 / ROW_ALIGN)) * ROW_ALIGN
    er = lax.broadcasted_iota(jnp.int32, (N_EXPERTS, N_EXPERTS), 0)
    ec = lax.broadcasted_iota(jnp.int32, (N_EXPERTS, N_EXPERTS), 1)
    run_off = jnp.dot(jnp.where(ec < er, 1.0, 0.0), jnp.broadcast_to(n_al, (N_EXPERTS, LANES)),
                      precision=lax.Precision.HIGHEST, preferred_element_type=F32)
    slot = prefix + run_off[:, 0:1]
    for k in range(TOP_K):
        gate_ref[k:k + 1, :] = es[k] * inv
        lslot_ref[k:k + 1, :] = jnp.sum(jnp.where(sels[k], slot, 0.0), axis=0,
                                        keepdims=True).astype(jnp.int32)
    cnt_ref[0] = jnp.broadcast_to(n, (N_EXPERTS, LANES))


def _outproj(o_a, o_b, sga, sgb, x2, w_out_a, w_out_b, w_out, g_moe, w_router_t, b_router, upper):
    t, d = x2.shape
    tm = TM_PROJ
    const = lambda *shape: pl.BlockSpec(shape, lambda i: (0,) * len(shape))
    row = lambda w: pl.BlockSpec((tm, w), lambda i: (i, 0))
    colk = pl.BlockSpec((TOP_K, tm), lambda i: (0, i))
    return pl.pallas_call(
        _outproj_kernel,
        grid=(t // tm,),
        in_specs=[row(SB_WIDTH), row(SGU_WIDTH), row(d), row(d), row(d),
                  const(SB_WIDTH, d), const(SGU_WIDTH, d), const(d, d),
                  const(1, d), const(N_EXPERTS, d), const(N_EXPERTS, 1), const(tm, tm)],
        out_specs=[row(d), row(d), colk, colk,
                   pl.BlockSpec((1, N_EXPERTS, LANES), lambda i: (i, 0, 0))],
        out_shape=[
            jax.ShapeDtypeStruct((t, d), F32),
            jax.ShapeDtypeStruct((t, d), BF16),
            jax.ShapeDtypeStruct((TOP_K, t), jnp.int32),
            jax.ShapeDtypeStruct((TOP_K, t), F32),
            jax.ShapeDtypeStruct((t // tm, N_EXPERTS, LANES), F32),
        ],
        compiler_params=_cparams(("arbitrary",)),
        name="outproj_router",
    )(o_a, o_b, sga, sgb, x2, w_out_a, w_out_b, w_out, g_moe, w_router_t, b_router, upper)


def _piece_loops(tables, win, make_copy, start):
    n_ref, local_ref, global_ref = tables
    for cls, rows in enumerate(PIECE_ROWS):
        head = win * len(PIECE_ROWS) + cls
        count = n_ref[head]

        def piece(i, priority, off=head * MAX_PIECES, rows=rows):
            copy = make_copy(pl.multiple_of(local_ref[off + i], ROW_ALIGN),
                             pl.multiple_of(global_ref[off + i], ROW_ALIGN), rows)
            if start:
                copy.start(priority=priority)
            else:
                copy.wait()

        def pair(i, _, piece=piece):
            piece(2 * i, 0)
            piece(2 * i + 1, 1)
            return 0

        lax.fori_loop(0, lax.shift_right_logical(count, 1), pair, 0)

        @pl.when(lax.bitwise_and(count, 1) == 1)
        def _(piece=piece, count=count):
            piece(count - 1, 0)


def _piece_tables(run, loff, base):
    big = PIECE_ROWS[0]
    nbig = run // big
    nsmall = (run - nbig * big) // ROW_ALIGN
    i = jnp.arange(MAX_PIECES, dtype=jnp.int32)
    experts = jnp.arange(N_EXPERTS, dtype=jnp.int32)

    def flat(cnt, rows, skip):
        cum = jnp.cumsum(cnt, axis=1)
        owner = jnp.minimum(jnp.sum(i[None, :, None] >= cum[:, None, :], axis=-1), N_EXPERTS - 1)
        onehot = owner[:, :, None] == experts
        pick = lambda tab: jnp.sum(jnp.where(onehot, tab[:, None, :], 0), axis=-1)
        j = i[None, :] - pick(cum - cnt)
        return cum[:, -1], pick(loff + skip) + j * rows, pick(base + skip) + j * rows

    nb, lb, gb = flat(nbig, big, 0)
    ns, lsm, gsm = flat(nsmall, ROW_ALIGN, nbig * big)
    pack = lambda a, b: jnp.stack([a, b], axis=1).reshape(-1).astype(jnp.int32)
    return pack(nb, ns), pack(lb, lsm), pack(gb, gsm)


def _dispatch_kernel(pn_ref, pl_ref, pg_ref, pad_end_ref, lslot_ref, c_ref, xs_ref,
                     loc_sc, zero_sc, sem, zsem, tsem):
    w = pl.program_id(0)
    tm = c_ref.shape[0]
    tb = zero_sc.shape[0]
    nloc = loc_sc.shape[1]

    def unused_blocks(fn):
        def body(b, _):
            fn(pltpu.make_async_copy(
                zero_sc, xs_ref.at[pl.ds(pl.multiple_of(b * tb, tb), tb), :], tsem))
            return 0
        lax.fori_loop(pad_end_ref[N_EXPERTS - 1] // tb, xs_ref.shape[0] // tb, body, 0)

    @pl.when(w == 0)
    def _():
        zero_sc[...] = jnp.zeros_like(zero_sc)

        def pad_copy(e):
            return pltpu.make_async_copy(
                zero_sc, xs_ref.at[pl.ds(pl.multiple_of(pad_end_ref[e] - tb, tb), tb), :], zsem)

        def nonempty(e):
            prev = jnp.where(e > 0, pad_end_ref[jnp.maximum(e - 1, 0)], 0)
            return pad_end_ref[e] > prev

        def zstart(e, _):
            @pl.when(nonempty(e))
            def _():
                pad_copy(e).start()
            return 0

        def zwait(e, _):
            @pl.when(nonempty(e))
            def _():
                pad_copy(e).wait()
            return 0

        lax.fori_loop(0, N_EXPERTS, zstart, 0)
        unused_blocks(lambda copy: copy.start())
        lax.fori_loop(0, N_EXPERTS, zwait, 0)

    ls = [lslot_ref[k:k + 1, :] for k in range(TOP_K)]
    buf = lax.bitwise_and(w, 1)
    for r0 in range(0, nloc, SLOT_CHUNK):
        srow = lax.broadcasted_iota(jnp.int32, (SLOT_CHUNK, tm), 0) + r0
        sel = jnp.zeros((SLOT_CHUNK, tm), F32)
        for k in range(TOP_K):
            sel = jnp.where(srow == ls[k], 1.0, sel)
        loc_sc[buf, r0:r0 + SLOT_CHUNK, :] = jnp.dot(sel.astype(BF16), c_ref[...],
                                                     preferred_element_type=F32)

    tables = (pn_ref, pl_ref, pg_ref)

    def piece_copy(slot):
        def build(local_row, global_row, rows):
            return pltpu.make_async_copy(loc_sc.at[slot, pl.ds(local_row, rows), :],
                                         xs_ref.at[pl.ds(global_row, rows), :], sem.at[slot])
        return build

    _piece_loops(tables, w, piece_copy(buf), True)

    @pl.when(w > 0)
    def _():
        _piece_loops(tables, w - 1, piece_copy(1 - buf), False)

    @pl.when(w == pl.num_programs(0) - 1)
    def _():
        _piece_loops(tables, w, piece_copy(buf), False)
        unused_blocks(lambda copy: copy.wait())


def _dispatch(pieces, pad_end, lslot, c, n_slots):
    t, d = c.shape
    tm = TM_PROJ
    grid_spec = pltpu.PrefetchScalarGridSpec(
        num_scalar_prefetch=4,
        grid=(t // tm,),
        in_specs=[
            pl.BlockSpec((TOP_K, tm), lambda i, *_: (0, i)),
            pl.BlockSpec((tm, d), lambda i, *_: (i, 0)),
        ],
        out_specs=pl.BlockSpec(memory_space=pl.ANY),
        scratch_shapes=[pltpu.VMEM((2, LOCAL_SLOTS, d), F32),
                        pltpu.VMEM((TB_EXPERT, d), F32),
                        pltpu.SemaphoreType.DMA((2,)), pltpu.SemaphoreType.DMA(()),
                        pltpu.SemaphoreType.DMA(())],
    )
    return pl.pallas_call(
        _dispatch_kernel,
        grid_spec=grid_spec,
        out_shape=jax.ShapeDtypeStruct((n_slots, d), F32),
        compiler_params=_cparams(("arbitrary",)),
        name="moe_dispatch",
    )(*pieces, pad_end, lslot, c)


def _expert_kernel(be_ref, nb_ref, xs_ref, w1_ref, w2_ref, b1g_ref, b1l_ref, b2_ref, perm_ref,
                   ys_ref, w1g_sc, w1l_sc, w2_sc):
    i = pl.program_id(0)
    active = i < nb_ref[0]
    fresh = jnp.logical_or(i == 0, be_ref[i] != be_ref[jnp.maximum(i - 1, 0)])

    @pl.when(jnp.logical_and(active, fresh))
    def _():
        half = MXU_DIM // 2
        for cb in range(w1_ref.shape[2] // MXU_DIM):
            blk = w1_ref[0, :, cb * MXU_DIM:(cb + 1) * MXU_DIM].astype(BF16)
            de = jnp.dot(blk, perm_ref[...], preferred_element_type=F32)
            w1g_sc[:, cb * half:(cb + 1) * half] = de[:, :half].astype(BF16)
            w1l_sc[:, cb * half:(cb + 1) * half] = de[:, half:].astype(BF16)
        w2_sc[...] = w2_ref[0].astype(BF16)

    @pl.when(active)
    def _():
        xb = xs_ref[...].astype(BF16)
        hg = jnp.dot(xb, w1g_sc[...], preferred_element_type=F32) + b1g_ref[0]
        hl = jnp.dot(xb, w1l_sc[...], preferred_element_type=F32) + b1l_ref[0]
        xg = jnp.minimum(hg, SWIGLU_LIMIT)
        xl = jnp.clip(hl, -SWIGLU_LIMIT, SWIGLU_LIMIT)
        act = xg * _sigmoid(SWIGLU_ALPHA * xg) * (xl + 1.0)
        ys_ref[...] = jnp.dot(act.astype(BF16), w2_sc[...], preferred_element_type=F32) + b2_ref[0]

    @pl.when(jnp.logical_not(active))
    def _():
        ys_ref[...] = jnp.zeros_like(ys_ref)


def _experts(block_e, n_used, xs, w1, w2, b1g, b1l, b2, perm):
    n_slots, d = xs.shape
    f2 = w1.shape[2]
    f = f2 // 2
    tb = TB_EXPERT
    blk = lambda i, be, nb: (jnp.minimum(i, nb[0] - 1), 0)
    wsel = lambda i, be, nb: (be[i], 0, 0)
    grid_spec = pltpu.PrefetchScalarGridSpec(
        num_scalar_prefetch=2,
        grid=(n_slots // tb,),
        in_specs=[
            pl.BlockSpec((tb, d), blk),
            pl.BlockSpec((1, d, f2), wsel),
            pl.BlockSpec((1, f, d), wsel),
            pl.BlockSpec((1, 1, f), wsel),
            pl.BlockSpec((1, 1, f), wsel),
            pl.BlockSpec((1, 1, d), wsel),
            pl.BlockSpec((MXU_DIM, MXU_DIM), lambda i, be, nb: (0, 0)),
        ],
        out_specs=pl.BlockSpec((tb, d), lambda i, be, nb: (i, 0)),
        scratch_shapes=[pltpu.VMEM((d, f), BF16), pltpu.VMEM((d, f), BF16),
                        pltpu.VMEM((f, d), BF16)],
    )
    return pl.pallas_call(
        _expert_kernel,
        grid_spec=grid_spec,
        out_shape=jax.ShapeDtypeStruct((n_slots, d), F32),
        compiler_params=_cparams(("arbitrary",)),
        name="moe_experts",
    )(block_e, n_used, xs, w1, w2, b1g, b1l, b2, perm)


def _combine_kernel(pn_ref, pl_ref, pg_ref, lslot_ref, gate_ref, h_ref, p_ref, ys_ref,
                    gp_ref, wg_ref, wp_ref, gf_ref, out_ref, loc_sc, locb_sc, sem,
                    *, final):
    w = pl.program_id(0)
    tm = h_ref.shape[0]
    nloc = loc_sc.shape[0]

    tables = (pn_ref, pl_ref, pg_ref)

    def piece_copy(local_row, global_row, rows):
        return pltpu.make_async_copy(ys_ref.at[pl.ds(global_row, rows), :],
                                     loc_sc.at[pl.ds(local_row, rows), :], sem)

    @pl.when(w == 0)
    def _():
        loc_sc[...] = jnp.zeros_like(loc_sc)
        _piece_loops(tables, w, piece_copy, True)

    _piece_loops(tables, w, piece_copy, False)
    for c0 in range(0, nloc, SLOT_CHUNK):
        locb_sc[c0:c0 + SLOT_CHUNK, :] = loc_sc[c0:c0 + SLOT_CHUNK, :].astype(BF16)

    @pl.when(w + 1 < pl.num_programs(0))
    def _():
        _piece_loops(tables, w + 1, piece_copy, True)

    lcol = [lslot_ref[:, k:k + 1] for k in range(TOP_K)]
    gcol = [gate_ref[:, k:k + 1] for k in range(TOP_K)]
    h = h_ref[...]
    for c0 in range(0, nloc, SLOT_CHUNK):
        scol = lax.broadcasted_iota(jnp.int32, (tm, SLOT_CHUNK), 1) + c0
        g = jnp.zeros((tm, SLOT_CHUNK), F32)
        for k in range(TOP_K):
            g = jnp.where(scol == lcol[k], gcol[k], g)
        h = h + jnp.dot(g.astype(BF16), locb_sc[c0:c0 + SLOT_CHUNK, :],
                        preferred_element_type=F32)

    r = _rms(h, gp_ref[...]).astype(BF16)
    pg = _sigmoid(jnp.dot(r, wg_ref[...], preferred_element_type=F32))
    pp = jnp.dot(p_ref[...].astype(BF16), wp_ref[...], preferred_element_type=F32)
    h = h + pg * pp
    out_ref[...] = _rms(h, gf_ref[...]) if final else h


def _combine(pieces, lslot_tk, gate_tk, h1, p2, ys, g_ple, w_ple_gate, w_ple_proj, g_final,
             final):
    t, d = h1.shape
    pd = p2.shape[1]
    tm = TM_PROJ
    const = lambda *shape: pl.BlockSpec(shape, lambda i, *_: (0,) * len(shape))
    row = lambda width: pl.BlockSpec((tm, width), lambda i, *_: (i, 0))
    grid_spec = pltpu.PrefetchScalarGridSpec(
        num_scalar_prefetch=3,
        grid=(t // tm,),
        in_specs=[row(TOP_K), row(TOP_K), row(d), row(pd),
                  pl.BlockSpec(memory_space=pl.ANY),
                  const(1, d), const(d, d), const(pd, d), const(1, d)],
        out_specs=row(d),
        scratch_shapes=[pltpu.VMEM((LOCAL_SLOTS, d), F32), pltpu.VMEM((LOCAL_SLOTS, d), BF16),
                        pltpu.SemaphoreType.DMA(())],
    )
    return pl.pallas_call(
        functools.partial(_combine_kernel, final=final),
        grid_spec=grid_spec,
        out_shape=jax.ShapeDtypeStruct((t, d), F32),
        compiler_params=_cparams(("arbitrary",)),
        name="combine_ple",
    )(*pieces, lslot_tk, gate_tk, h1, p2, ys, g_ple, w_ple_gate, w_ple_proj, g_final)


def kernel(x, p, g_mix, w_in, w_out_a, w_out_b, w_out, sgu_norm, sgu_w, sgu_b, g_moe,
           w_router, b_router, w1, b1, w2, b2, g_ple, w_ple_gate, w_ple_proj, g_final):
    b, s, d = x.shape
    depth = w_in.shape[0]
    t = b * s
    assert s % (TQ_ATTN * ATTN_TILES_PER_STEP) == 0 and t % TM_PROJ == 0

    kk = jnp.arange(KC_ATTN)
    ntri = -(kk[:, None] >= kk[None, :]).astype(BF16)
    tt = jnp.arange(TM_PROJ)
    upper = (tt[:, None] < tt[None, :]).astype(BF16)
    rr = jnp.arange(MXU_DIM)[:, None]
    cc = jnp.arange(MXU_DIM)[None, :]
    half = MXU_DIM // 2
    perm = (rr == jnp.where(cc < half, 2 * cc, 2 * (cc - half) + 1)).astype(BF16)

    tb = TB_EXPERT
    n_win = t // TM_PROJ
    n_blocks = -(-(t * TOP_K + n_win * N_EXPERTS * (ROW_ALIGN - 1) + N_EXPERTS * (tb - 1)) // tb)
    n_slots = n_blocks * tb

    h = x.reshape(t, d)
    for i in range(depth):
        sgu_bias = jnp.repeat(sgu_b[i].T, SGU_GROUP_DIM, axis=1)
        qkv, o_b, sga, sgb = _inproj(h, g_mix[i][None], w_in[i].astype(BF16), sgu_w[i],
                                     sgu_bias, sgu_norm[i][None])
        o_a = _attention(qkv.reshape(b, s, 3 * SB_WIDTH), ntri).reshape(t, SB_WIDTH)
        h1, c, lslot, gate, cnt = _outproj(
            o_a, o_b, sga, sgb, h, w_out_a[i].astype(BF16), w_out_b[i].astype(BF16),
            w_out[i].astype(BF16), g_moe[i][None], w_router[i].T, b_router[i][:, None], upper)

        counts = cnt[:, :, 0].astype(jnp.int32)
        run = ((counts + ROW_ALIGN - 1) // ROW_ALIGN) * ROW_ALIGN
        loff = jnp.cumsum(run, axis=1) - run
        region = ((jnp.sum(run, axis=0) + tb - 1) // tb) * tb
        pad_end = jnp.cumsum(region).astype(jnp.int32)
        base = (pad_end - region)[None, :] + jnp.cumsum(run, axis=0) - run
        pieces = _piece_tables(run, loff, base)
        starts = jnp.arange(n_blocks, dtype=jnp.int32) * tb
        block_e = jnp.minimum(jnp.sum(starts[:, None] >= pad_end[None, :], axis=1),
                              N_EXPERTS - 1).astype(jnp.int32)
        n_used = (pad_end[-1:] // tb).astype(jnp.int32)

        xs = _dispatch(pieces, pad_end, lslot, c, n_slots)
        ys = _experts(block_e, n_used, xs, w1[i], w2[i],
                      b1[i][:, None, 0::2], b1[i][:, None, 1::2], b2[i][:, None, :], perm)
        h = _combine(pieces, lslot.T, gate.T, h1, p[i].reshape(t, -1), ys, g_ple[i][None],
                     w_ple_gate[i].astype(BF16), w_ple_proj[i].astype(BF16), g_final[None],
                     final=(i == depth - 1))
    return h.reshape(b, s, d)
```
